```python
import jax, jax.numpy as jnp
from jax import lax
import numpy as np

D_MODEL = 1024
BATCH = 4
SEQ = 4096
DEPTH = 4
DEC_BATCH = 128
DEC_SEQ = 4
PAST_LEN = 8192
PAGE_SIZE = 128

N_A_LAYERS = DEPTH // 2
N_B_LAYERS = DEPTH - N_A_LAYERS
CHUNK = 128
D_GATE = D_MODEL
N_GROUPS_A = 8
GROUP_DIM_A = D_GATE // N_GROUPS_A
HEAD_DIM = 64
N_HEADS = D_MODEL // HEAD_DIM
N_KV_HEADS = 4
Q_PER_KV = N_HEADS // N_KV_HEADS
WINDOW = 128
N_EXPERTS = 32
TOP_K = 4
D_EXPERT = D_MODEL
SWIGLU_LIMIT = 7.0
SWIGLU_ALPHA = 1.702
PLE_DIM = 256
DEEPNORM_ALPHA = (2 * DEPTH) ** 0.25
DEEPNORM_BETA = (8 * DEPTH) ** -0.25
LN_EPS = 1e-5
NEG_INF = -1e30

kernel_name = "yoco_gmlp_swa_sink_moe_decoder_step"


def layer_norm(x, g, b):
    xf = x.astype(jnp.float32)
    mu = jnp.mean(xf, axis=-1, keepdims=True)
    var = jnp.mean(jnp.square(xf - mu), axis=-1, keepdims=True)
    y = (xf - mu) * lax.rsqrt(var + LN_EPS)
    return (y * g.astype(jnp.float32) + b.astype(jnp.float32)).astype(x.dtype)


def alibi_slopes():
    h = jnp.arange(1, N_HEADS + 1, dtype=jnp.float32)
    return (2.0 ** (-8.0 * h / N_HEADS)).reshape(N_KV_HEADS, Q_PER_KV)


def gmlp_mixer(x, chunk_len, w_in, b_in, ln_g, ln_b, w_s, b_s, w_out):
    bsz, seq, _ = x.shape
    h = jax.nn.gelu(jnp.einsum('bsd,de->bse', x, w_in) + b_in, approximate=False)
    u, v = jnp.split(h, 2, axis=-1)
    v = layer_norm(v, ln_g, ln_b)
    vc = v.reshape(bsz, seq // chunk_len, chunk_len, N_GROUPS_A, GROUP_DIM_A)
    causal = jnp.tril(jnp.ones((chunk_len, chunk_len), dtype=bool))
    ws = jnp.where(causal[None], w_s[:, :chunk_len, :chunk_len], 0.0)
    s = jnp.einsum('gts,bnsgc->bntgc', ws, vc) + b_s[:, :chunk_len].T[None, None, :, :, None]
    gated = u * s.reshape(bsz, seq, D_GATE)
    return jnp.einsum('bse,ed->bsd', gated, w_out), v


def moe(x, router_w, router_b, w_up, b_up, w_dn, b_dn):
    shp = x.shape
    xt = x.reshape(-1, D_MODEL)
    logits = (xt @ router_w + router_b).astype(jnp.float32)
    top_val, top_idx = lax.top_k(logits, TOP_K)
    top_w = jax.nn.softmax(top_val, axis=-1)
    gates = jnp.einsum('nk,nke->ne', top_w,
                       jax.nn.one_hot(top_idx, N_EXPERTS, dtype=jnp.float32)).astype(x.dtype)

    def expert(acc, params):
        wu, bu, wd, bd, g = params
        h = xt @ wu + bu
        x_glu, x_lin = jnp.split(h, 2, axis=-1)
        x_glu = jnp.minimum(x_glu, SWIGLU_LIMIT)
        x_lin = jnp.clip(x_lin, -SWIGLU_LIMIT, SWIGLU_LIMIT)
        act = x_glu * jax.nn.sigmoid(SWIGLU_ALPHA * x_glu) * (x_lin + 1.0)
        return acc + g[:, None] * (act @ wd + bd), None

    y, _ = lax.scan(expert, jnp.zeros_like(xt), (w_up, b_up, w_dn, b_dn, gates.T))
    return y.reshape(shp)


def ple_add(x, p, w_proj, w_gate):
    return x + jax.nn.sigmoid(x @ w_gate) * (p @ w_proj)


def shared_kv(x, g, b, w_kv):
    h = jnp.einsum('bsd,de->bse', layer_norm(x, g, b), w_kv)
    k, v = jnp.split(h, 2, axis=-1)
    bsz, seq = x.shape[:2]
    return (k.reshape(bsz, seq, N_KV_HEADS, HEAD_DIM),
            v.reshape(bsz, seq, N_KV_HEADS, HEAD_DIM))


def sink_window_attention(q, k, v, dist, valid, slopes, sinks):
    s = jnp.einsum('bnqkgd,bnskd->bnkgqs', q, k).astype(jnp.float32) * (HEAD_DIM ** -0.5)
    s = s - slopes[None, None, :, :, None, None] * dist.astype(jnp.float32)[None, :, None, None]
    s = jnp.where(valid[None, :, None, None], s, NEG_INF)
    sink = jnp.broadcast_to(sinks.astype(jnp.float32)[None, None, :, :, None, None],
                            s.shape[:-1] + (1,))
    p = jax.nn.softmax(jnp.concatenate([s, sink], axis=-1), axis=-1)[..., :-1]
    return jnp.einsum('bnkgqs,bnskd->bnqkgd', p.astype(v.dtype), v)


def swa_prompt(x, k, v, w_q, sinks, w_o, slopes):
    bsz, seq, _ = x.shape
    nb = seq // WINDOW
    q = jnp.einsum('bsd,de->bse', x, w_q).reshape(bsz, nb, WINDOW, N_KV_HEADS, Q_PER_KV, HEAD_DIM)
    kb = k.reshape(bsz, nb, WINDOW, N_KV_HEADS, HEAD_DIM)
    vb = v.reshape(bsz, nb, WINDOW, N_KV_HEADS, HEAD_DIM)
    pad = jnp.zeros_like(kb[:, :1])
    kk = jnp.concatenate([jnp.concatenate([pad, kb[:, :-1]], axis=1), kb], axis=2)
    vv = jnp.concatenate([jnp.concatenate([pad, vb[:, :-1]], axis=1), vb], axis=2)
    qi = jnp.arange(WINDOW)[:, None]
    kj = jnp.arange(2 * WINDOW)[None, :]
    dist = qi + WINDOW - kj
    key_pos = jnp.arange(nb)[:, None, None] * WINDOW + kj[None] - WINDOW
    valid = (dist >= 0)[None] & (dist < WINDOW)[None] & (key_pos >= 0)
    dist_b = jnp.broadcast_to(dist[None], valid.shape)
    o = sink_window_attention(q, kk, vv, dist_b, valid, slopes, sinks.reshape(N_KV_HEADS, Q_PER_KV))
    return o.reshape(bsz, seq, N_HEADS * HEAD_DIM) @ w_o


def swa_sample(x, k_buf, v_buf, w_q, sinks, w_o, slopes):
    bsz, t, _ = x.shape
    q = jnp.einsum('bsd,de->bse', x, w_q).reshape(bsz, 1, t, N_KV_HEADS, Q_PER_KV, HEAD_DIM)
    dist = jnp.arange(t)[:, None] + WINDOW - jnp.arange(WINDOW + t)[None, :]
    valid = (dist >= 0) & (dist < WINDOW)
    o = sink_window_attention(q, k_buf[:, None], v_buf[:, None], dist[None], valid[None],
                              slopes, sinks.reshape(N_KV_HEADS, Q_PER_KV))
    return o.reshape(bsz, t, N_HEADS * HEAD_DIM) @ w_o


def setup_inputs(seed: int = 0) -> dict:
    key = jax.random.key(seed)
    ks = iter(jax.random.split(key, 48))

    def nrm(shape, scale):
        return jax.random.normal(next(ks), shape, jnp.float32) * scale

    hq = N_HEADS * HEAD_DIM
    return {
        "x_prompt": nrm((BATCH, SEQ, D_MODEL), 1.0),
        "x_sample": nrm((DEC_BATCH, DEC_SEQ, D_MODEL), 1.0),
        "cache_k": nrm((DEC_BATCH, WINDOW, N_KV_HEADS, HEAD_DIM), 1.0),
        "cache_v": nrm((DEC_BATCH, WINDOW, N_KV_HEADS, HEAD_DIM), 1.0),
        "p_prompt": nrm((DEPTH, BATCH, SEQ, PLE_DIM), 1.0),
        "p_sample": nrm((DEPTH, DEC_BATCH, DEC_SEQ, PLE_DIM), 1.0),
        "ln1_g": 1.0 + nrm((DEPTH, D_MODEL), 0.02),
        "ln1_b": nrm((DEPTH, D_MODEL), 0.02),
        "ln2_g": 1.0 + nrm((DEPTH, D_MODEL), 0.02),
        "ln2_b": nrm((DEPTH, D_MODEL), 0.02),
        "a_w_in": nrm((N_A_LAYERS, D_MODEL, 2 * D_GATE), D_MODEL ** -0.5),
        "a_b_in": nrm((N_A_LAYERS, 2 * D_GATE), 0.02),
        "a_ln_g": 1.0 + nrm((N_A_LAYERS, D_GATE), 0.02),
        "a_ln_b": nrm((N_A_LAYERS, D_GATE), 0.02),
        "a_w_s": nrm((N_A_LAYERS, N_GROUPS_A, CHUNK, CHUNK), CHUNK ** -0.5),
        "a_b_s": 1.0 + nrm((N_A_LAYERS, N_GROUPS_A, CHUNK), 0.1),
        "a_w_out": nrm((N_A_LAYERS, D_GATE, D_MODEL), D_GATE ** -0.5 * DEEPNORM_BETA),
        "kv_ln_g": 1.0 + nrm((D_MODEL,), 0.02),
        "kv_ln_b": nrm((D_MODEL,), 0.02),
        "w_kv": nrm((D_MODEL, 2 * N_KV_HEADS * HEAD_DIM), D_MODEL ** -0.5),
        "b_w_q": nrm((N_B_LAYERS, D_MODEL, hq), D_MODEL ** -0.5),
        "b_sinks": nrm((N_B_LAYERS, N_HEADS), 0.5),
        "b_w_o": nrm((N_B_LAYERS, hq, D_MODEL), hq ** -0.5 * DEEPNORM_BETA),
        "router_w": nrm((DEPTH, D_MODEL, N_EXPERTS), D_MODEL ** -0.5),
        "router_b": nrm((DEPTH, N_EXPERTS), 0.01),
        "exp_w_up": nrm((DEPTH, N_EXPERTS, D_MODEL, 2 * D_EXPERT), D_MODEL ** -0.5),
        "exp_b_up": nrm((DEPTH, N_EXPERTS, 2 * D_EXPERT), 0.02),
        "exp_w_dn": nrm((DEPTH, N_EXPERTS, D_EXPERT, D_MODEL), D_EXPERT ** -0.5 * DEEPNORM_BETA),
        "exp_b_dn": nrm((DEPTH, N_EXPERTS, D_MODEL), 0.02),
        "ple_w_proj": nrm((DEPTH, PLE_DIM, D_MODEL), PLE_DIM ** -0.5),
        "ple_w_gate": nrm((DEPTH, D_MODEL, D_MODEL), D_MODEL ** -0.5),
    }


def reference(x_prompt, x_sample, cache_k, cache_v, p_prompt, p_sample,
              ln1_g, ln1_b, ln2_g, ln2_b,
              a_w_in, a_b_in, a_ln_g, a_ln_b, a_w_s, a_b_s, a_w_out,
              kv_ln_g, kv_ln_b, w_kv, b_w_q, b_sinks, b_w_o,
              router_w, router_b, exp_w_up, exp_b_up, exp_w_dn, exp_b_dn,
              ple_w_proj, ple_w_gate):
    slopes = alibi_slopes()
    xp, xs = x_prompt, x_sample
    chunk_v = []
    kp = vp = k_buf = v_buf = None
    for i in range(DEPTH):
        if i < N_A_LAYERS:
            mp, _ = gmlp_mixer(xp, CHUNK, a_w_in[i], a_b_in[i], a_ln_g[i], a_ln_b[i],
                               a_w_s[i], a_b_s[i], a_w_out[i])
            ms, v_rows = gmlp_mixer(xs, DEC_SEQ, a_w_in[i], a_b_in[i], a_ln_g[i], a_ln_b[i],
                                    a_w_s[i], a_b_s[i], a_w_out[i])
            chunk_v.append(v_rows)
        else:
            j = i - N_A_LAYERS
            mp = swa_prompt(xp, kp, vp, b_w_q[j], b_sinks[j], b_w_o[j], slopes)
            ms = swa_sample(xs, k_buf, v_buf, b_w_q[j], b_sinks[j], b_w_o[j], slopes)
        xp = layer_norm(DEEPNORM_ALPHA * xp + mp, ln1_g[i], ln1_b[i])
        xs = layer_norm(DEEPNORM_ALPHA * xs + ms, ln1_g[i], ln1_b[i])
        fp = moe(xp, router_w[i], router_b[i], exp_w_up[i], exp_b_up[i], exp_w_dn[i], exp_b_dn[i])
        fs = moe(xs, router_w[i], router_b[i], exp_w_up[i], exp_b_up[i], exp_w_dn[i], exp_b_dn[i])
        xp = layer_norm(DEEPNORM_ALPHA * xp + fp, ln2_g[i], ln2_b[i])
        xs = layer_norm(DEEPNORM_ALPHA * xs + fs, ln2_g[i], ln2_b[i])
        xp = ple_add(xp, p_prompt[i], ple_w_proj[i], ple_w_gate[i])
        xs = ple_add(xs, p_sample[i], ple_w_proj[i], ple_w_gate[i])
        if i == N_A_LAYERS - 1:
            kp, vp = shared_kv(xp, kv_ln_g, kv_ln_b, w_kv)
            ks_new, vs_new = shared_kv(xs, kv_ln_g, kv_ln_b, w_kv)
            k_buf = jnp.concatenate([cache_k, ks_new], axis=1)
            v_buf = jnp.concatenate([cache_v, vs_new], axis=1)
    return (xp, xs, jnp.stack(chunk_v), kp[:, -WINDOW:], vp[:, -WINDOW:],
            k_buf[:, -WINDOW:], v_buf[:, -WINDOW:])
```

```python
import functools

import jax
import jax.numpy as jnp
from jax import lax
from jax.experimental import pallas as pl
from jax.experimental.pallas import tpu as pltpu

F32 = jnp.float32
BF16 = jnp.bfloat16

D_MODEL = 1024
DEPTH = 4
N_A_LAYERS = 2
CHUNK = 128
N_GROUPS_A = 8
GROUP_DIM_A = D_MODEL // N_GROUPS_A
HEAD_DIM = 64
N_HEADS = 16
N_KV_HEADS = 4
Q_PER_KV = 4
KV_DIM = N_KV_HEADS * HEAD_DIM
WINDOW = 128
N_EXPERTS = 32
TOP_K = 4
D_EXPERT = 1024
SWIGLU_LIMIT = 7.0
SWIGLU_ALPHA = 1.702
PLE_DIM = 256
DEEPNORM_ALPHA = (2 * DEPTH) ** 0.25
LN_EPS = 1e-5
NEG_INF = -1e30

TM_TOKEN = 256
TM_ROUTER = 512
TM_EXPERT = 256
SAMPLE_SEQ_BLOCK = 8
KEY_PAD = 256
VMEM_LIMIT = 48 * 1024 * 1024


def _ln(x, g, b):
    mu = jnp.mean(x, axis=-1, keepdims=True)
    xc = x - mu
    var = jnp.mean(xc * xc, axis=-1, keepdims=True)
    return xc * lax.rsqrt(var + LN_EPS) * g + b


def _dot(a, b):
    return jnp.dot(a, b, preferred_element_type=F32)


def _dot_nt(a, b):
    return lax.dot_general(a, b, (((1,), (1,)), ((), ())), preferred_element_type=F32)


def _full(shape):
    n = len(shape)
    return pl.BlockSpec(shape, lambda *_: (0,) * n)


def _gmlp_kernel(x_ref, win_ref, bin_ref, lng_ref, lnb_ref, ws_ref, bs_ref, wout_ref,
                 g1_ref, b1_ref, x1_ref, v_ref, *, tm):
    x = x_ref[...]
    h = _dot(x.astype(BF16), win_ref[...]) + bin_ref[...]
    h = 0.5 * h * (1.0 + lax.erf(h * (2.0 ** -0.5)))
    u = h[:, :D_MODEL]
    v = _ln(h[:, D_MODEL:], lng_ref[...], lnb_ref[...])
    v_ref[...] = v
    vb = v.astype(BF16)
    rows = []
    for c in range(tm // CHUNK):
        cols = []
        for g in range(N_GROUPS_A):
            blk = vb[c * CHUNK:(c + 1) * CHUNK, g * GROUP_DIM_A:(g + 1) * GROUP_DIM_A]
            cols.append(_dot(ws_ref[g], blk))
        rows.append(jnp.concatenate(cols, axis=1) + bs_ref[...])
    s = jnp.concatenate(rows, axis=0)
    gated = (u * s).astype(BF16)
    m = _dot(gated, wout_ref[...])
    x1_ref[...] = _ln(DEEPNORM_ALPHA * x + m, g1_ref[...], b1_ref[...])


def _gmlp_layer(x, n_prompt, w_in, b_in, ln_g, ln_b, ws2, bs2, w_out, g1, b1, tm=TM_TOKEN):
    n = x.shape[0]
    p_tiles = n_prompt // tm
    n_tiles = n // tm
    s_tiles = n_tiles - p_tiles
    kind = lambda i: jnp.where(i >= p_tiles, 1, 0)
    x1, v = pl.pallas_call(
        functools.partial(_gmlp_kernel, tm=tm),
        grid=(n_tiles,),
        in_specs=[
            pl.BlockSpec((tm, D_MODEL), lambda i: (i, 0)),
            _full((D_MODEL, 2 * D_MODEL)),
            _full((1, 2 * D_MODEL)),
            _full((1, D_MODEL)),
            _full((1, D_MODEL)),
            pl.BlockSpec((None, N_GROUPS_A, CHUNK, CHUNK), lambda i: (kind(i), 0, 0, 0)),
            pl.BlockSpec((None, CHUNK, D_MODEL), lambda i: (kind(i), 0, 0)),
            _full((D_MODEL, D_MODEL)),
            _full((1, D_MODEL)),
            _full((1, D_MODEL)),
        ],
        out_specs=[
            pl.BlockSpec((tm, D_MODEL), lambda i: (i, 0)),
            pl.BlockSpec((tm, D_MODEL), lambda i: (jnp.maximum(i - (p_tiles - 1), 0), 0)),
        ],
        out_shape=[
            jax.ShapeDtypeStruct((n, D_MODEL), F32),
            jax.ShapeDtypeStruct(((s_tiles + 1) * tm, D_MODEL), F32),
        ],
        compiler_params=pltpu.CompilerParams(
            dimension_semantics=("arbitrary",), vmem_limit_bytes=VMEM_LIMIT),
        name="gmlp_mixer",
    )(x, w_in, b_in, ln_g, ln_b, ws2, bs2, w_out, g1, b1)
    return x1, v[tm:]


def _swa_prompt_kernel(sink_ref, x_ref, k_ref, v_ref, bias_ref, wq_ref, wo_ref, g1_ref, b1_ref,
                       x1_ref, *, tq):
    i = pl.program_id(1)
    x = x_ref[...]
    qb = (_dot(x.astype(BF16), wq_ref[...]) * (HEAD_DIM ** -0.5)).astype(BF16)
    blocks = []
    for j in range(tq // WINDOW):
        blk = i * (tq // WINDOW) + j
        kstart = pl.multiple_of(blk * WINDOW, WINDOW)
        kb = k_ref[pl.ds(kstart, 2 * WINDOW), :]
        vb = v_ref[pl.ds(kstart, 2 * WINDOW), :]
        sel = jnp.where(blk == 0, 0, 1)
        outs = []
        for h in range(N_HEADS):
            kh = h // Q_PER_KV
            qh = qb[j * WINDOW:(j + 1) * WINDOW, h * HEAD_DIM:(h + 1) * HEAD_DIM]
            s = _dot_nt(qh, kb[:, kh * HEAD_DIM:(kh + 1) * HEAD_DIM])
            l = s + bias_ref[sel, h]
            sink = sink_ref[h]
            m = jnp.maximum(jnp.max(l, axis=1, keepdims=True), sink)
            p = jnp.exp(l - m)
            den = jnp.sum(p, axis=1, keepdims=True) + jnp.exp(sink - m)
            o = _dot(p.astype(BF16), vb[:, kh * HEAD_DIM:(kh + 1) * HEAD_DIM])
            outs.append(o / den)
        blocks.append(jnp.concatenate(outs, axis=1))
    attn = jnp.concatenate(blocks, axis=0)
    m_out = _dot(attn.astype(BF16), wo_ref[...])
    x1_ref[...] = _ln(DEEPNORM_ALPHA * x + m_out, g1_ref[...], b1_ref[...])


def _swa_prompt_layer(x, bsz, seq, kpad, vpad, bias, sinks, w_q, w_o, g1, b1, tq=TM_TOKEN):
    nq = seq // tq
    return pl.pallas_call(
        functools.partial(_swa_prompt_kernel, tq=tq),
        grid=(bsz, nq),
        in_specs=[
            pl.BlockSpec(memory_space=pltpu.SMEM),
            pl.BlockSpec((tq, D_MODEL), lambda b, i: (b * nq + i, 0)),
            pl.BlockSpec((None, seq + WINDOW, KV_DIM), lambda b, i: (b, 0, 0)),
            pl.BlockSpec((None, seq + WINDOW, KV_DIM), lambda b, i: (b, 0, 0)),
            _full((2, N_HEADS, WINDOW, 2 * WINDOW)),
            _full((D_MODEL, D_MODEL)),
            _full((D_MODEL, D_MODEL)),
            _full((1, D_MODEL)),
            _full((1, D_MODEL)),
        ],
        out_specs=pl.BlockSpec((tq, D_MODEL), lambda b, i: (b * nq + i, 0)),
        out_shape=jax.ShapeDtypeStruct((bsz * seq, D_MODEL), F32),
        compiler_params=pltpu.CompilerParams(
            dimension_semantics=("arbitrary", "arbitrary"), vmem_limit_bytes=VMEM_LIMIT),
        name="swa_prompt",
    )(sinks, x, kpad, vpad, bias, w_q, w_o, g1, b1)


def _swa_sample_kernel(x_ref, k_ref, v_ref, bias_ref, sink_ref, mask_ref, wq_ref, wo_ref,
                       g1_ref, b1_ref, x1_ref, *, sb, t):
    x = x_ref[...]
    q = _dot(x.astype(BF16), wq_ref[...]) * (HEAD_DIM ** -0.5)
    mask = mask_ref[...]
    bias = bias_ref[...]
    sink = sink_ref[...]
    outs = []
    for s in range(sb):
        qs = q[s * t:(s + 1) * t, :]
        parts = []
        for g in range(Q_PER_KV):
            qg = qs[:, g * KV_DIM:(g + 1) * KV_DIM]
            parts.append(jnp.concatenate([qg] * N_KV_HEADS, axis=0) * mask)
        qexp = jnp.concatenate(parts, axis=0).astype(BF16)
        l = _dot_nt(qexp, k_ref[s]) + bias
        m = jnp.maximum(jnp.max(l, axis=1, keepdims=True), sink)
        p = jnp.exp(l - m)
        den = jnp.sum(p, axis=1, keepdims=True) + jnp.exp(sink - m)
        r = _dot(p.astype(BF16), v_ref[s]) / den
        og = []
        for g in range(Q_PER_KV):
            rg = r[g * N_KV_HEADS * t:(g + 1) * N_KV_HEADS * t, :] * mask
            acc = rg[0:t]
            for kh in range(1, N_KV_HEADS):
                acc = acc + rg[kh * t:(kh + 1) * t]
            og.append(acc)
        outs.append(jnp.concatenate(og, axis=1))
    attn = jnp.concatenate(outs, axis=0)
    m_out = _dot(attn.astype(BF16), wo_ref[...])
    x1_ref[...] = _ln(DEEPNORM_ALPHA * x + m_out, g1_ref[...], b1_ref[...])


def _swa_sample_layer(x, n_seq, t, kbuf, vbuf, bias, sink_col, mask, w_q, w_o, g1, b1,
                      sb=SAMPLE_SEQ_BLOCK):
    rows = sb * t
    hr = N_HEADS * t
    return pl.pallas_call(
        functools.partial(_swa_sample_kernel, sb=sb, t=t),
        grid=(n_seq // sb,),
        in_specs=[
            pl.BlockSpec((rows, D_MODEL), lambda i: (i, 0)),
            pl.BlockSpec((sb, KEY_PAD, KV_DIM), lambda i: (i, 0, 0)),
            pl.BlockSpec((sb, KEY_PAD, KV_DIM), lambda i: (i, 0, 0)),
            _full((hr, KEY_PAD)),
            _full((hr, 1)),
            _full((N_KV_HEADS * t, KV_DIM)),
            _full((D_MODEL, D_MODEL)),
            _full((D_MODEL, D_MODEL)),
            _full((1, D_MODEL)),
            _full((1, D_MODEL)),
        ],
        out_specs=pl.BlockSpec((rows, D_MODEL), lambda i: (i, 0)),
        out_shape=jax.ShapeDtypeStruct((n_seq * t, D_MODEL), F32),
        compiler_params=pltpu.CompilerParams(
            dimension_semantics=("arbitrary",), vmem_limit_bytes=VMEM_LIMIT),
        name="swa_sample",
    )(x, kbuf, vbuf, bias, sink_col, mask, w_q, w_o, g1, b1)


def _router_kernel(x_ref, wt_ref, b_ref, tri_ref, idx_ref, gate_ref, rank_ref, cnt_ref, carry,
                   *, tm):
    i = pl.program_id(0)

    @pl.when(i == 0)
    def _():
        carry[...] = jnp.zeros_like(carry)

    logits = lax.dot_general(wt_ref[...], x_ref[...], (((1,), (1,)), ((), ())),
                             precision=lax.Precision.HIGHEST,
                             preferred_element_type=F32) + b_ref[...]
    eidx = lax.broadcasted_iota(jnp.int32, (N_EXPERTS, tm), 0)
    l = logits
    vals, idxs, sels = [], [], []
    for _ in range(TOP_K):
        m = jnp.max(l, axis=0, keepdims=True)
        idx = jnp.min(jnp.where(l == m, eidx, N_EXPERTS), axis=0, keepdims=True)
        sel = eidx == idx
        vals.append(m)
        idxs.append(idx)
        sels.append(sel)
        l = jnp.where(sel, -jnp.inf, l)
    exps = [jnp.exp(v - vals[0]) for v in vals]
    den = exps[0] + exps[1] + exps[2] + exps[3]
    onehot = jnp.zeros((N_EXPERTS, tm), F32)
    for sel in sels:
        onehot = onehot + sel.astype(F32)
    incl = _dot(onehot.astype(BF16), tri_ref[...])
    excl = incl - onehot + carry[...]
    for k in range(TOP_K):
        idx_ref[k:k + 1, :] = idxs[k]
        gate_ref[k:k + 1, :] = exps[k] / den
        rank = jnp.sum(jnp.where(sels[k], excl, 0.0), axis=0, keepdims=True)
        rank_ref[k:k + 1, :] = rank.astype(jnp.int32)
    carry[...] = carry[...] + jnp.sum(onehot, axis=1, keepdims=True)
    cnt_ref[...] = jnp.broadcast_to(carry[...], cnt_ref.shape)


def _route(x1, router_wt, router_b, tri, tm=TM_ROUTER):
    n = x1.shape[0]
    spec_kn = pl.BlockSpec((TOP_K, tm), lambda i: (0, i))
    idx, gate, rank, cnt = pl.pallas_call(
        functools.partial(_router_kernel, tm=tm),
        grid=(n // tm,),
        in_specs=[
            pl.BlockSpec((tm, D_MODEL), lambda i: (i, 0)),
            _full((N_EXPERTS, D_MODEL)),
            _full((N_EXPERTS, 1)),
            _full((tm, tm)),
        ],
        out_specs=[spec_kn, spec_kn, spec_kn, _full((N_EXPERTS, 128))],
        out_shape=[
            jax.ShapeDtypeStruct((TOP_K, n), jnp.int32),
            jax.ShapeDtypeStruct((TOP_K, n), F32),
            jax.ShapeDtypeStruct((TOP_K, n), jnp.int32),
            jax.ShapeDtypeStruct((N_EXPERTS, 128), F32),
        ],
        scratch_shapes=[pltpu.VMEM((N_EXPERTS, 1), F32)],
        compiler_params=pltpu.CompilerParams(
            dimension_semantics=("arbitrary",), vmem_limit_bytes=VMEM_LIMIT),
        name="moe_router",
    )(x1, router_wt, router_b, tri)
    return idx, gate, rank, cnt[:, 0].astype(jnp.int32)


def _row_copy_out(x_ref, xs_hbm, sem, t, p):
    return pltpu.make_async_copy(x_ref.at[pl.ds(t, 1), :], xs_hbm.at[pl.ds(p, 1), :], sem)


def _dispatch_kernel(pos_ref, x_ref, xs_in_hbm, xs_hbm, sem, *, tm):
    del xs_in_hbm

    def issue(t, c):
        for k in range(TOP_K):
            _row_copy_out(x_ref, xs_hbm, sem, t, pos_ref[0, 0, k * tm + t]).start()
        return c

    lax.fori_loop(0, tm, issue, 0)

    def drain(t, c):
        for k in range(TOP_K):
            _row_copy_out(x_ref, xs_hbm, sem, t, pos_ref[0, 0, k * tm + t]).wait()
        return c

    lax.fori_loop(0, tm, drain, 0)


def _dispatch(x1, pos3, xs_buf, tm=TM_TOKEN):
    n = x1.shape[0]
    return pl.pallas_call(
        functools.partial(_dispatch_kernel, tm=tm),
        grid=(n // tm,),
        in_specs=[
            pl.BlockSpec((1, 1, TOP_K * tm), lambda i: (i, 0, 0), memory_space=pltpu.SMEM),
            pl.BlockSpec((tm, D_MODEL), lambda i: (i, 0)),
            pl.BlockSpec(memory_space=pl.ANY),
        ],
        out_specs=pl.BlockSpec(memory_space=pl.ANY),
        out_shape=jax.ShapeDtypeStruct(xs_buf.shape, xs_buf.dtype),
        scratch_shapes=[pltpu.SemaphoreType.DMA],
        input_output_aliases={2: 0},
        compiler_params=pltpu.CompilerParams(
            dimension_semantics=("arbitrary",), vmem_limit_bytes=VMEM_LIMIT,
            has_side_effects=True),
        name="moe_dispatch",
    )(pos3, x1, xs_buf)


def _expert_kernel(te_ref, nv_ref, xs_ref, wup_ref, bup_ref, wdn_ref, bdn_ref, y_ref,
                   wup_bf, wdn_bf):
    i = pl.program_id(0)
    e = te_ref[i]
    prev = te_ref[jnp.maximum(i - 1, 0)]

    @pl.when((i == 0) | (e != prev))
    def _():
        rb = 128

        def cast_up(r, c):
            r0 = pl.multiple_of(r * rb, rb)
            wup_bf[pl.ds(r0, rb), :] = wup_ref[pl.ds(r0, rb), :].astype(BF16)
            return c

        def cast_dn(r, c):
            r0 = pl.multiple_of(r * rb, rb)
            wdn_bf[pl.ds(r0, rb), :] = wdn_ref[pl.ds(r0, rb), :].astype(BF16)
            return c

        lax.fori_loop(0, D_MODEL // rb, cast_up, 0)
        lax.fori_loop(0, D_EXPERT // rb, cast_dn, 0)

    @pl.when(i < nv_ref[0])
    def _():
        x = xs_ref[...].astype(BF16)
        h = _dot(x, wup_bf[...]) + bup_ref[...]
        glu = jnp.minimum(h[:, :D_EXPERT], SWIGLU_LIMIT)
        lin = jnp.clip(h[:, D_EXPERT:], -SWIGLU_LIMIT, SWIGLU_LIMIT)
        act = glu * jax.nn.sigmoid(SWIGLU_ALPHA * glu) * (lin + 1.0)
        y_ref[...] = _dot(act.astype(BF16), wdn_bf[...]) + bdn_ref[...]

    @pl.when(i >= nv_ref[0])
    def _():
        y_ref[...] = jnp.zeros_like(y_ref)


def _experts(xs, tile_expert, n_valid, w_up, b_up, w_dn, b_dn, layer, tm=TM_EXPERT):
    rows = xs.shape[0]
    row_map = lambda i, te, nv: (jnp.minimum(i, nv[0] - 1), 0)
    grid_spec = pltpu.PrefetchScalarGridSpec(
        num_scalar_prefetch=2,
        grid=(rows // tm,),
        in_specs=[
            pl.BlockSpec((tm, D_MODEL), row_map),
            pl.BlockSpec((None, None, D_MODEL, 2 * D_EXPERT), lambda i, te, nv: (layer, te[i], 0, 0)),
            pl.BlockSpec((None, None, 1, 2 * D_EXPERT), lambda i, te, nv: (layer, te[i], 0, 0)),
            pl.BlockSpec((None, None, D_EXPERT, D_MODEL), lambda i, te, nv: (layer, te[i], 0, 0)),
            pl.BlockSpec((None, None, 1, D_MODEL), lambda i, te, nv: (layer, te[i], 0, 0)),
        ],
        out_specs=pl.BlockSpec((tm, D_MODEL), lambda i, te, nv: (i, 0)),
        scratch_shapes=[
            pltpu.VMEM((D_MODEL, 2 * D_EXPERT), BF16),
            pltpu.VMEM((D_EXPERT, D_MODEL), BF16),
        ],
    )
    return pl.pallas_call(
        _expert_kernel,
        grid_spec=grid_spec,
        out_shape=jax.ShapeDtypeStruct((rows, D_MODEL), F32),
        compiler_params=pltpu.CompilerParams(
            dimension_semantics=("arbitrary",), vmem_limit_bytes=VMEM_LIMIT),
        name="moe_experts",
    )(tile_expert, n_valid, xs, w_up, b_up, w_dn, b_dn)


def _row_copy_in(ys_hbm, ybuf, sem, k, t, p):
    return pltpu.make_async_copy(ys_hbm.at[pl.ds(p, 1), :], ybuf.at[k, pl.ds(t, 1), :], sem)


def _combine_kernel(pos_ref, x1_ref, gate_ref, p_ref, wg_ref, wp_ref, g2_ref, b2_ref, *rest,
                    tm, emit_kv):
    if emit_kv:
        kvg_ref, kvb_ref, wkv_ref, ys_hbm, x3_ref, kv_ref, ybuf, sem = rest
    else:
        ys_hbm, x3_ref, ybuf, sem = rest

    def issue(t, c):
        for k in range(TOP_K):
            _row_copy_in(ys_hbm, ybuf, sem, k, t, pos_ref[0, 0, k * tm + t]).start()
        return c

    lax.fori_loop(0, tm, issue, 0)

    def drain(t, c):
        for k in range(TOP_K):
            _row_copy_in(ys_hbm, ybuf, sem, k, t, pos_ref[0, 0, k * tm + t]).wait()
        return c

    lax.fori_loop(0, tm, drain, 0)

    gate = gate_ref[...]
    f = gate[:, 0:1] * ybuf[0]
    for k in range(1, TOP_K):
        f = f + gate[:, k:k + 1] * ybuf[k]
    x2 = _ln(DEEPNORM_ALPHA * x1_ref[...] + f, g2_ref[...], b2_ref[...])
    pg = jax.nn.sigmoid(_dot(x2.astype(BF16), wg_ref[...]))
    pp = _dot(p_ref[...].astype(BF16), wp_ref[...])
    x3 = x2 + pg * pp
    x3_ref[...] = x3
    if emit_kv:
        kv_ref[...] = _dot(_ln(x3, kvg_ref[...], kvb_ref[...]).astype(BF16), wkv_ref[...])


def _combine(x1, pos3, gate_t, ys, p, w_gate, w_proj, g2, b2, kv_params=None, tm=TM_TOKEN):
    n = x1.shape[0]
    emit_kv = kv_params is not None
    row = lambda w: pl.BlockSpec((tm, w), lambda i: (i, 0))
    in_specs = [
        pl.BlockSpec((1, 1, TOP_K * tm), lambda i: (i, 0, 0), memory_space=pltpu.SMEM),
        row(D_MODEL),
        row(TOP_K),
        row(PLE_DIM),
        _full((D_MODEL, D_MODEL)),
        _full((PLE_DIM, D_MODEL)),
        _full((1, D_MODEL)),
        _full((1, D_MODEL)),
    ]
    args = [pos3, x1, gate_t, p, w_gate, w_proj, g2, b2]
    out_specs = [row(D_MODEL)]
    out_shape = [jax.ShapeDtypeStruct((n, D_MODEL), F32)]
    if emit_kv:
        in_specs += [_full((1, D_MODEL)), _full((1, D_MODEL)), _full((D_MODEL, 2 * KV_DIM))]
        args += list(kv_params)
        out_specs.append(row(2 * KV_DIM))
        out_shape.append(jax.ShapeDtypeStruct((n, 2 * KV_DIM), F32))
    in_specs.append(pl.BlockSpec(memory_space=pl.ANY))
    args.append(ys)
    outs = pl.pallas_call(
        functools.partial(_combine_kernel, tm=tm, emit_kv=emit_kv),
        grid=(n // tm,),
        in_specs=in_specs,
        out_specs=out_specs,
        out_shape=out_shape,
        scratch_shapes=[pltpu.VMEM((TOP_K, tm, D_MODEL), F32), pltpu.SemaphoreType.DMA],
        compiler_params=pltpu.CompilerParams(
            dimension_semantics=("arbitrary",), vmem_limit_bytes=VMEM_LIMIT),
        name="moe_combine",
    )(*args)
    return outs if emit_kv else (outs[0], None)


def _tile_pos(pos, tm):
    n = pos.shape[1]
    return pos.reshape(TOP_K, n // tm, tm).transpose(1, 0, 2).reshape(n // tm, 1, TOP_K * tm)


def _moe_layer(x1, xs_buf, p, layer, router_wt, router_b, tri, w_up, b_up, w_dn, b_dn,
               w_gate, w_proj, g2, b2, kv_params=None):
    idx, gate, rank, counts = _route(x1, router_wt, router_b, tri)
    padded = ((counts + TM_EXPERT - 1) // TM_EXPERT) * TM_EXPERT
    ends = jnp.cumsum(padded)
    starts = ends - padded
    onehot = idx[:, :, None] == jnp.arange(N_EXPERTS, dtype=jnp.int32)
    pos = jnp.sum(jnp.where(onehot, starts, 0), axis=-1) + rank
    n_tiles = xs_buf.shape[0] // TM_EXPERT
    n_valid = (ends[-1] // TM_EXPERT).astype(jnp.int32)
    tile_start = jnp.minimum(jnp.arange(n_tiles, dtype=jnp.int32), n_valid - 1) * TM_EXPERT
    tile_expert = jnp.sum(tile_start[:, None] >= ends[None, :], axis=1).astype(jnp.int32)
    pos3 = _tile_pos(pos, TM_TOKEN)
    xs_buf = _dispatch(x1, pos3, xs_buf)
    ys = _experts(xs_buf, tile_expert, n_valid.reshape(1), w_up, b_up, w_dn, b_dn, layer)
    x3, kv = _combine(x1, pos3, gate.T, ys, p, w_gate, w_proj, g2, b2, kv_params)
    return x3, kv, xs_buf


def _alibi_slopes():
    h = jnp.arange(1, N_HEADS + 1, dtype=F32)
    return 2.0 ** (-8.0 * h / N_HEADS)


def _prompt_bias():
    qi = jnp.arange(WINDOW)[:, None]
    kj = jnp.arange(2 * WINDOW)[None, :]
    dist = qi + WINDOW - kj
    valid = (dist >= 0) & (dist < WINDOW)
    slopes = _alibi_slopes()[:, None, None]
    tables = []
    for first in (True, False):
        ok = valid & (kj >= WINDOW) if first else valid
        tables.append(jnp.where(ok[None], -slopes * dist.astype(F32)[None], NEG_INF))
    return jnp.stack(tables).astype(F32)


def _sample_tables(t, sinks):
    g = jnp.arange(Q_PER_KV)[:, None, None]
    kh = jnp.arange(N_KV_HEADS)[None, :, None]
    tok = jnp.arange(t)[None, None, :]
    head = jnp.broadcast_to(kh * Q_PER_KV + g, (Q_PER_KV, N_KV_HEADS, t)).reshape(-1)
    tok = jnp.broadcast_to(tok, (Q_PER_KV, N_KV_HEADS, t)).reshape(-1)
    kj = jnp.arange(KEY_PAD)[None, :]
    dist = tok[:, None] + WINDOW - kj
    valid = (dist >= 0) & (dist < WINDOW) & (kj < WINDOW + t)
    slopes = _alibi_slopes()[head][:, None]
    bias = jnp.where(valid, -slopes * dist.astype(F32), NEG_INF).astype(F32)
    sink_col = sinks[head][:, None].astype(F32)
    row_kh = jnp.repeat(jnp.arange(N_KV_HEADS), t)[:, None]
    lane_kh = (jnp.arange(KV_DIM) // HEAD_DIM)[None, :]
    mask = (row_kh == lane_kh).astype(F32)
    return bias, sink_col, mask


def _gkd_perm():
    g = jnp.arange(Q_PER_KV)[:, None, None]
    kh = jnp.arange(N_KV_HEADS)[None, :, None]
    d = jnp.arange(HEAD_DIM)[None, None, :]
    return ((kh * Q_PER_KV + g) * HEAD_DIM + d).reshape(-1)


def _spatial_tables(w_s, b_s, t):
    tril = jnp.tril(jnp.ones((CHUNK, CHUNK), dtype=bool))
    ws_p = jnp.where(tril[None], w_s, 0.0)
    bs_p = jnp.repeat(b_s.T, GROUP_DIM_A, axis=1)
    r = jnp.arange(CHUNK)
    same = (r[:, None] // t) == (r[None, :] // t)
    small = jnp.where(tril[None, :t, :t], w_s[:, :t, :t], 0.0)
    ws_s = jnp.where(same[None], small[:, r[:, None] % t, r[None, :] % t], 0.0)
    bs_s = jnp.repeat(b_s[:, r % t].T, GROUP_DIM_A, axis=1)
    return jnp.stack([ws_p, ws_s]).astype(BF16), jnp.stack([bs_p, bs_s]).astype(F32)


def kernel(x_prompt, x_sample, cache_k, cache_v, p_prompt, p_sample, ln1_g, ln1_b, ln2_g, ln2_b,
           a_w_in, a_b_in, a_ln_g, a_ln_b, a_w_s, a_b_s, a_w_out, kv_ln_g, kv_ln_b, w_kv,
           b_w_q, b_sinks, b_w_o, router_w, router_b, exp_w_up, exp_b_up, exp_w_dn, exp_b_dn,
           ple_w_proj, ple_w_gate):
    bsz, seq, _ = x_prompt.shape
    n_seq, t, _ = x_sample.shape
    n_p = bsz * seq
    n_s = n_seq * t
    n = n_p + n_s
    depth = ln1_g.shape[0]
    n_a = a_w_in.shape[0]

    x = jnp.concatenate([x_prompt.reshape(n_p, D_MODEL), x_sample.reshape(n_s, D_MODEL)], axis=0)
    p_all = jnp.concatenate([p_prompt.reshape(depth, n_p, PLE_DIM),
                             p_sample.reshape(depth, n_s, PLE_DIM)], axis=1)
    row = lambda v: v.reshape(1, -1).astype(F32)

    xs_buf = jnp.zeros((n * TOP_K + N_EXPERTS * TM_EXPERT, D_MODEL), F32)
    tri = (jnp.arange(TM_ROUTER)[:, None] <= jnp.arange(TM_ROUTER)[None, :]).astype(BF16)
    b_up4 = exp_b_up.reshape(depth, N_EXPERTS, 1, 2 * D_EXPERT)
    b_dn4 = exp_b_dn.reshape(depth, N_EXPERTS, 1, D_MODEL)
    perm = _gkd_perm()
    prompt_bias = _prompt_bias()

    chunk_v = []
    kv_params = (row(kv_ln_g), row(kv_ln_b), w_kv.astype(BF16))
    kp = vp = k_buf = v_buf = None
    kpad = vpad = kbuf_pad = vbuf_pad = None
    for i in range(depth):
        if i < n_a:
            ws2, bs2 = _spatial_tables(a_w_s[i], a_b_s[i], t)
            x1, v_rows = _gmlp_layer(x, n_p, a_w_in[i].astype(BF16), row(a_b_in[i]),
                                     row(a_ln_g[i]), row(a_ln_b[i]), ws2, bs2,
                                     a_w_out[i].astype(BF16), row(ln1_g[i]), row(ln1_b[i]))
            chunk_v.append(v_rows.reshape(n_seq, t, D_MODEL))
        else:
            j = i - n_a
            wq = b_w_q[j].astype(BF16)
            wo = b_w_o[j].astype(BF16)
            x1_p = _swa_prompt_layer(x[:n_p], bsz, seq, kpad, vpad, prompt_bias,
                                     b_sinks[j].astype(F32), wq, wo, row(ln1_g[i]), row(ln1_b[i]))
            s_bias, s_sink, s_mask = _sample_tables(t, b_sinks[j])
            x1_s = _swa_sample_layer(x[n_p:], n_seq, t, kbuf_pad, vbuf_pad, s_bias, s_sink, s_mask,
                                     wq[:, perm], wo[perm, :], row(ln1_g[i]), row(ln1_b[i]))
            x1 = jnp.concatenate([x1_p, x1_s], axis=0)
        x, kv, xs_buf = _moe_layer(
            x1, xs_buf, p_all[i], i, router_w[i].T, router_b[i].reshape(N_EXPERTS, 1), tri,
            exp_w_up, b_up4, exp_w_dn, b_dn4, ple_w_gate[i].astype(BF16),
            ple_w_proj[i].astype(BF16), row(ln2_g[i]), row(ln2_b[i]),
            kv_params if i == n_a - 1 else None)
        if i == n_a - 1:
            k_all, v_all = kv[:, :KV_DIM], kv[:, KV_DIM:]
            kp = k_all[:n_p].reshape(bsz, seq, KV_DIM)
            vp = v_all[:n_p].reshape(bsz, seq, KV_DIM)
            k_buf = jnp.concatenate([cache_k.reshape(n_seq, WINDOW, KV_DIM),
                                     k_all[n_p:].reshape(n_seq, t, KV_DIM)], axis=1)
            v_buf = jnp.concatenate([cache_v.reshape(n_seq, WINDOW, KV_DIM),
                                     v_all[n_p:].reshape(n_seq, t, KV_DIM)], axis=1)
            front = ((0, 0), (WINDOW, 0), (0, 0))
            kpad = jnp.pad(kp, front).astype(BF16)
            vpad = jnp.pad(vp, front).astype(BF16)
            tail = ((0, 0), (0, KEY_PAD - WINDOW - t), (0, 0))
            kbuf_pad = jnp.pad(k_buf, tail).astype(BF16)
            vbuf_pad = jnp.pad(v_buf, tail).astype(BF16)

    heads = (N_KV_HEADS, HEAD_DIM)
    return (x[:n_p].reshape(bsz, seq, D_MODEL),
            x[n_p:].reshape(n_seq, t, D_MODEL),
            jnp.stack(chunk_v),
            kp[:, -WINDOW:].reshape(bsz, WINDOW, *heads),
            vp[:, -WINDOW:].reshape(bsz, WINDOW, *heads),
            k_buf[:, -WINDOW:].reshape(n_seq, WINDOW, *heads),
            v_buf[:, -WINDOW:].reshape(n_seq, WINDOW, *heads))
```

```python
import functools

import jax
import jax.numpy as jnp
from jax import lax
from jax.experimental import pallas as pl
from jax.experimental.pallas import tpu as pltpu

F32 = jnp.float32
BF16 = jnp.bfloat16

D_MODEL = 1024
DEPTH = 4
N_A_LAYERS = 2
CHUNK = 128
N_GROUPS_A = 8
GROUP_DIM_A = D_MODEL // N_GROUPS_A
HEAD_DIM = 64
N_HEADS = 16
N_KV_HEADS = 4
Q_PER_KV = 4
KV_DIM = N_KV_HEADS * HEAD_DIM
WINDOW = 128
N_EXPERTS = 32
TOP_K = 4
D_EXPERT = 1024
SWIGLU_LIMIT = 7.0
SWIGLU_ALPHA = 1.702
PLE_DIM = 256
DEEPNORM_ALPHA = (2 * DEPTH) ** 0.25
LN_EPS = 1e-5
NEG_INF = -1e30

TM_TOKEN = 256
TM_DISPATCH = 512
DISPATCH_CHUNK = 128
TM_ROUTER = 512
TM_EXPERT = 256
SAMPLE_SEQ_BLOCK = 8
KEY_PAD = 256
VMEM_LIMIT = 48 * 1024 * 1024


def _ln(x, g, b):
    mu = jnp.mean(x, axis=-1, keepdims=True)
    xc = x - mu
    var = jnp.mean(xc * xc, axis=-1, keepdims=True)
    return xc * lax.rsqrt(var + LN_EPS) * g + b


def _dot(a, b):
    return jnp.dot(a, b, preferred_element_type=F32)


def _dot_nt(a, b):
    return lax.dot_general(a, b, (((1,), (1,)), ((), ())), preferred_element_type=F32)


def _full(shape):
    n = len(shape)
    return pl.BlockSpec(shape, lambda *_: (0,) * n)


def _gmlp_kernel(xa_ref, xb_ref, win_ref, bin_ref, lng_ref, lnb_ref, ws_ref, bs_ref, wout_ref,
                 g1_ref, b1_ref, x1_ref, v_ref, *, tm, p_tiles):
    x = jnp.where(pl.program_id(0) >= p_tiles, xb_ref[...], xa_ref[...])
    h = _dot(x.astype(BF16), win_ref[...]) + bin_ref[...]
    h = 0.5 * h * (1.0 + lax.erf(h * (2.0 ** -0.5)))
    u = h[:, :D_MODEL]
    v = _ln(h[:, D_MODEL:], lng_ref[...], lnb_ref[...])
    v_ref[...] = v
    vb = v.astype(BF16)
    rows = []
    for c in range(tm // CHUNK):
        cols = []
        for g in range(N_GROUPS_A):
            blk = vb[c * CHUNK:(c + 1) * CHUNK, g * GROUP_DIM_A:(g + 1) * GROUP_DIM_A]
            cols.append(_dot(ws_ref[g], blk))
        rows.append(jnp.concatenate(cols, axis=1) + bs_ref[...])
    s = jnp.concatenate(rows, axis=0)
    gated = (u * s).astype(BF16)
    m = _dot(gated, wout_ref[...])
    x1_ref[...] = _ln(DEEPNORM_ALPHA * x + m, g1_ref[...], b1_ref[...])


def _gmlp_layer(xa, xb, xb_row0, n_prompt, n_sample, w_in, b_in, ln_g, ln_b, ws2, bs2, w_out,
                g1, b1, tm=TM_TOKEN):
    n = n_prompt + n_sample
    p_tiles = n_prompt // tm
    n_tiles = n // tm
    s_tiles = n_tiles - p_tiles
    b_tile0 = xb_row0 // tm
    kind = lambda i: jnp.where(i >= p_tiles, 1, 0)
    x1, v = pl.pallas_call(
        functools.partial(_gmlp_kernel, tm=tm, p_tiles=p_tiles),
        grid=(n_tiles,),
        in_specs=[
            pl.BlockSpec((tm, D_MODEL), lambda i: (jnp.minimum(i, p_tiles - 1), 0)),
            pl.BlockSpec((tm, D_MODEL), lambda i: (jnp.maximum(i - p_tiles, 0) + b_tile0, 0)),
            _full((D_MODEL, 2 * D_MODEL)),
            _full((1, 2 * D_MODEL)),
            _full((1, D_MODEL)),
            _full((1, D_MODEL)),
            pl.BlockSpec((None, N_GROUPS_A, CHUNK, CHUNK), lambda i: (kind(i), 0, 0, 0)),
            pl.BlockSpec((None, CHUNK, D_MODEL), lambda i: (kind(i), 0, 0)),
            _full((D_MODEL, D_MODEL)),
            _full((1, D_MODEL)),
            _full((1, D_MODEL)),
        ],
        out_specs=[
            pl.BlockSpec((tm, D_MODEL), lambda i: (i, 0)),
            pl.BlockSpec((tm, D_MODEL), lambda i: (jnp.maximum(i - (p_tiles - 1), 0), 0)),
        ],
        out_shape=[
            jax.ShapeDtypeStruct((n, D_MODEL), F32),
            jax.ShapeDtypeStruct(((s_tiles + 1) * tm, D_MODEL), F32),
        ],
        compiler_params=pltpu.CompilerParams(
            dimension_semantics=("arbitrary",), vmem_limit_bytes=VMEM_LIMIT),
        name="gmlp_mixer",
    )(xa, xb, w_in, b_in, ln_g, ln_b, ws2, bs2, w_out, g1, b1)
    return x1, v[tm:]


def _swa_prompt_kernel(sink_ref, x_ref, k_ref, v_ref, bias_ref, wq_ref, wo_ref, g1_ref, b1_ref,
                       x1_ref, *, tq):
    i = pl.program_id(1)
    x = x_ref[...]
    qb = (_dot(x.astype(BF16), wq_ref[...]) * (HEAD_DIM ** -0.5)).astype(BF16)
    blocks = []
    for j in range(tq // WINDOW):
        blk = i * (tq // WINDOW) + j
        kstart = pl.multiple_of(blk * WINDOW, WINDOW)
        kb = k_ref[pl.ds(kstart, 2 * WINDOW), :]
        vb = v_ref[pl.ds(kstart, 2 * WINDOW), :]
        sel = jnp.where(blk == 0, 0, 1)
        outs = []
        for h in range(N_HEADS):
            kh = h // Q_PER_KV
            qh = qb[j * WINDOW:(j + 1) * WINDOW, h * HEAD_DIM:(h + 1) * HEAD_DIM]
            s = _dot_nt(qh, kb[:, kh * HEAD_DIM:(kh + 1) * HEAD_DIM])
            l = s + bias_ref[sel, h]
            sink = sink_ref[h]
            m = jnp.maximum(jnp.max(l, axis=1, keepdims=True), sink)
            p = jnp.exp(l - m)
            den = jnp.sum(p, axis=1, keepdims=True) + jnp.exp(sink - m)
            o = _dot(p.astype(BF16), vb[:, kh * HEAD_DIM:(kh + 1) * HEAD_DIM])
            outs.append(o / den)
        blocks.append(jnp.concatenate(outs, axis=1))
    attn = jnp.concatenate(blocks, axis=0)
    m_out = _dot(attn.astype(BF16), wo_ref[...])
    x1_ref[...] = _ln(DEEPNORM_ALPHA * x + m_out, g1_ref[...], b1_ref[...])


def _swa_prompt_layer(x, bsz, seq, kpad, vpad, bias, sinks, w_q, w_o, g1, b1, tq=TM_TOKEN):
    nq = seq // tq
    return pl.pallas_call(
        functools.partial(_swa_prompt_kernel, tq=tq),
        grid=(bsz, nq),
        in_specs=[
            pl.BlockSpec(memory_space=pltpu.SMEM),
            pl.BlockSpec((tq, D_MODEL), lambda b, i: (b * nq + i, 0)),
            pl.BlockSpec((None, seq + WINDOW, KV_DIM), lambda b, i: (b, 0, 0)),
            pl.BlockSpec((None, seq + WINDOW, KV_DIM), lambda b, i: (b, 0, 0)),
            _full((2, N_HEADS, WINDOW, 2 * WINDOW)),
            _full((D_MODEL, D_MODEL)),
            _full((D_MODEL, D_MODEL)),
            _full((1, D_MODEL)),
            _full((1, D_MODEL)),
        ],
        out_specs=pl.BlockSpec((tq, D_MODEL), lambda b, i: (b * nq + i, 0)),
        out_shape=jax.ShapeDtypeStruct((bsz * seq, D_MODEL), F32),
        compiler_params=pltpu.CompilerParams(
            dimension_semantics=("arbitrary", "arbitrary"), vmem_limit_bytes=VMEM_LIMIT),
        name="swa_prompt",
    )(sinks, x, kpad, vpad, bias, w_q, w_o, g1, b1)


def _swa_sample_kernel(x_ref, k_ref, v_ref, bias_ref, sink_ref, mask_ref, wq_ref, wo_ref,
                       g1_ref, b1_ref, x1_ref, *, sb, t):
    x = x_ref[...]
    q = _dot(x.astype(BF16), wq_ref[...]) * (HEAD_DIM ** -0.5)
    mask = mask_ref[...]
    bias = bias_ref[...]
    sink = sink_ref[...]
    outs = []
    for s in range(sb):
        qs = q[s * t:(s + 1) * t, :]
        parts = []
        for g in range(Q_PER_KV):
            qg = qs[:, g * KV_DIM:(g + 1) * KV_DIM]
            parts.append(jnp.concatenate([qg] * N_KV_HEADS, axis=0) * mask)
        qexp = jnp.concatenate(parts, axis=0).astype(BF16)
        l = _dot_nt(qexp, k_ref[s]) + bias
        m = jnp.maximum(jnp.max(l, axis=1, keepdims=True), sink)
        p = jnp.exp(l - m)
        den = jnp.sum(p, axis=1, keepdims=True) + jnp.exp(sink - m)
        r = _dot(p.astype(BF16), v_ref[s]) / den
        og = []
        for g in range(Q_PER_KV):
            rg = r[g * N_KV_HEADS * t:(g + 1) * N_KV_HEADS * t, :] * mask
            acc = rg[0:t]
            for kh in range(1, N_KV_HEADS):
                acc = acc + rg[kh * t:(kh + 1) * t]
            og.append(acc)
        outs.append(jnp.concatenate(og, axis=1))
    attn = jnp.concatenate(outs, axis=0)
    m_out = _dot(attn.astype(BF16), wo_ref[...])
    x1_ref[...] = _ln(DEEPNORM_ALPHA * x + m_out, g1_ref[...], b1_ref[...])


def _swa_sample_layer(x, row0, n_seq, t, kbuf, vbuf, bias, sink_col, mask, w_q, w_o, g1, b1,
                      sb=SAMPLE_SEQ_BLOCK):
    rows = sb * t
    hr = N_HEADS * t
    blk0 = row0 // rows
    return pl.pallas_call(
        functools.partial(_swa_sample_kernel, sb=sb, t=t),
        grid=(n_seq // sb,),
        in_specs=[
            pl.BlockSpec((rows, D_MODEL), lambda i: (blk0 + i, 0)),
            pl.BlockSpec((sb, KEY_PAD, KV_DIM), lambda i: (i, 0, 0)),
            pl.BlockSpec((sb, KEY_PAD, KV_DIM), lambda i: (i, 0, 0)),
            _full((hr, KEY_PAD)),
            _full((hr, 1)),
            _full((N_KV_HEADS * t, KV_DIM)),
            _full((D_MODEL, D_MODEL)),
            _full((D_MODEL, D_MODEL)),
            _full((1, D_MODEL)),
            _full((1, D_MODEL)),
        ],
        out_specs=pl.BlockSpec((rows, D_MODEL), lambda i: (i, 0)),
        out_shape=jax.ShapeDtypeStruct((n_seq * t, D_MODEL), F32),
        compiler_params=pltpu.CompilerParams(
            dimension_semantics=("arbitrary",), vmem_limit_bytes=VMEM_LIMIT),
        name="swa_sample",
    )(x, kbuf, vbuf, bias, sink_col, mask, w_q, w_o, g1, b1)


def _router_kernel(x_ref, wt_ref, b_ref, tri_ref, idx_ref, gate_ref, rank_ref, cnt_ref, carry,
                   *, tm):
    i = pl.program_id(0)

    @pl.when(i == 0)
    def _():
        carry[...] = jnp.zeros_like(carry)

    logits = lax.dot_general(wt_ref[...], x_ref[...], (((1,), (1,)), ((), ())),
                             precision=lax.Precision.HIGHEST,
                             preferred_element_type=F32) + b_ref[...]
    eidx = lax.broadcasted_iota(jnp.int32, (N_EXPERTS, tm), 0)
    l = logits
    vals, idxs, sels = [], [], []
    for _ in range(TOP_K):
        m = jnp.max(l, axis=0, keepdims=True)
        idx = jnp.min(jnp.where(l == m, eidx, N_EXPERTS), axis=0, keepdims=True)
        sel = eidx == idx
        vals.append(m)
        idxs.append(idx)
        sels.append(sel)
        l = jnp.where(sel, -jnp.inf, l)
    exps = [jnp.exp(v - vals[0]) for v in vals]
    den = exps[0] + exps[1] + exps[2] + exps[3]
    onehot = jnp.zeros((N_EXPERTS, tm), F32)
    for sel in sels:
        onehot = onehot + sel.astype(F32)
    incl = _dot(onehot.astype(BF16), tri_ref[...])
    excl = incl - onehot + carry[...]
    for k in range(TOP_K):
        idx_ref[k:k + 1, :] = idxs[k]
        gate_ref[k:k + 1, :] = exps[k] / den
        rank = jnp.sum(jnp.where(sels[k], excl, 0.0), axis=0, keepdims=True)
        rank_ref[k:k + 1, :] = rank.astype(jnp.int32)
    carry[...] = carry[...] + jnp.sum(onehot, axis=1, keepdims=True)
    cnt_ref[...] = jnp.broadcast_to(carry[...], cnt_ref.shape)


def _route(x1, router_wt, router_b, tri, tm=TM_ROUTER):
    n = x1.shape[0]
    spec_kn = pl.BlockSpec((TOP_K, tm), lambda i: (0, i))
    idx, gate, rank, cnt = pl.pallas_call(
        functools.partial(_router_kernel, tm=tm),
        grid=(n // tm,),
        in_specs=[
            pl.BlockSpec((tm, D_MODEL), lambda i: (i, 0)),
            _full((N_EXPERTS, D_MODEL)),
            _full((N_EXPERTS, 1)),
            _full((tm, tm)),
        ],
        out_specs=[spec_kn, spec_kn, spec_kn, _full((N_EXPERTS, 128))],
        out_shape=[
            jax.ShapeDtypeStruct((TOP_K, n), jnp.int32),
            jax.ShapeDtypeStruct((TOP_K, n), F32),
            jax.ShapeDtypeStruct((TOP_K, n), jnp.int32),
            jax.ShapeDtypeStruct((N_EXPERTS, 128), F32),
        ],
        scratch_shapes=[pltpu.VMEM((N_EXPERTS, 1), F32)],
        compiler_params=pltpu.CompilerParams(
            dimension_semantics=("arbitrary",), vmem_limit_bytes=VMEM_LIMIT),
        name="moe_router",
    )(x1, router_wt, router_b, tri)
    return idx, gate, rank, cnt[:, 0].astype(jnp.int32)


def _row_copy_out(x_ref, xs_hbm, sem, t, p):
    return pltpu.make_async_copy(x_ref.at[pl.ds(t, 1), :], xs_hbm.at[pl.ds(p, 1), :], sem)


def _dispatch_kernel(pos_ref, x_ref, xs_in_hbm, xs_hbm, sem, *, tm, chunk):
    del xs_in_hbm

    def issue(c, carry):
        t0 = c * chunk
        for t in range(chunk):
            for k in range(TOP_K):
                _row_copy_out(x_ref, xs_hbm, sem, t0 + t, pos_ref[0, 0, k * tm + t0 + t]).start()
        return carry

    lax.fori_loop(0, tm // chunk, issue, 0)
    for _ in range(TOP_K):
        pltpu.make_async_copy(x_ref, xs_hbm.at[pl.ds(0, tm), :], sem).wait()


def _dispatch(x1, pos3, xs_buf, tm=TM_DISPATCH, chunk=DISPATCH_CHUNK):
    n = x1.shape[0]
    return pl.pallas_call(
        functools.partial(_dispatch_kernel, tm=tm, chunk=chunk),
        grid=(n // tm,),
        in_specs=[
            pl.BlockSpec((1, 1, TOP_K * tm), lambda i: (i, 0, 0), memory_space=pltpu.SMEM),
            pl.BlockSpec((tm, D_MODEL), lambda i: (i, 0)),
            pl.BlockSpec(memory_space=pl.ANY),
        ],
        out_specs=pl.BlockSpec(memory_space=pl.ANY),
        out_shape=jax.ShapeDtypeStruct(xs_buf.shape, xs_buf.dtype),
        scratch_shapes=[pltpu.SemaphoreType.DMA],
        input_output_aliases={2: 0},
        compiler_params=pltpu.CompilerParams(
            dimension_semantics=("arbitrary",), vmem_limit_bytes=VMEM_LIMIT,
            has_side_effects=True),
        name="moe_dispatch",
    )(pos3, x1, xs_buf)


def _expert_kernel(te_ref, nv_ref, xs_ref, wup_ref, bup_ref, wdn_ref, bdn_ref, y_ref,
                   wup_bf, wdn_bf):
    i = pl.program_id(0)
    e = te_ref[i]
    prev = te_ref[jnp.maximum(i - 1, 0)]

    @pl.when((i == 0) | (e != prev))
    def _():
        rb = 128

        def cast_up(r, c):
            r0 = pl.multiple_of(r * rb, rb)
            wup_bf[pl.ds(r0, rb), :] = wup_ref[pl.ds(r0, rb), :].astype(BF16)
            return c

        def cast_dn(r, c):
            r0 = pl.multiple_of(r * rb, rb)
            wdn_bf[pl.ds(r0, rb), :] = wdn_ref[pl.ds(r0, rb), :].astype(BF16)
            return c

        lax.fori_loop(0, D_MODEL // rb, cast_up, 0)
        lax.fori_loop(0, D_EXPERT // rb, cast_dn, 0)

    @pl.when(i < nv_ref[0])
    def _():
        x = xs_ref[...].astype(BF16)
        h = _dot(x, wup_bf[...]) + bup_ref[...]
        glu = jnp.minimum(h[:, :D_EXPERT], SWIGLU_LIMIT)
        lin = jnp.clip(h[:, D_EXPERT:], -SWIGLU_LIMIT, SWIGLU_LIMIT)
        act = glu * jax.nn.sigmoid(SWIGLU_ALPHA * glu) * (lin + 1.0)
        y_ref[...] = _dot(act.astype(BF16), wdn_bf[...]) + bdn_ref[...]

    @pl.when(i >= nv_ref[0])
    def _():
        y_ref[...] = jnp.zeros_like(y_ref)


def _experts(xs, tile_expert, n_valid, w_up, b_up, w_dn, b_dn, layer, tm=TM_EXPERT):
    rows = xs.shape[0]
    row_map = lambda i, te, nv: (jnp.minimum(i, nv[0] - 1), 0)
    grid_spec = pltpu.PrefetchScalarGridSpec(
        num_scalar_prefetch=2,
        grid=(rows // tm,),
        in_specs=[
            pl.BlockSpec((tm, D_MODEL), row_map),
            pl.BlockSpec((None, None, D_MODEL, 2 * D_EXPERT), lambda i, te, nv: (layer, te[i], 0, 0)),
            pl.BlockSpec((None, None, 1, 2 * D_EXPERT), lambda i, te, nv: (layer, te[i], 0, 0)),
            pl.BlockSpec((None, None, D_EXPERT, D_MODEL), lambda i, te, nv: (layer, te[i], 0, 0)),
            pl.BlockSpec((None, None, 1, D_MODEL), lambda i, te, nv: (layer, te[i], 0, 0)),
        ],
        out_specs=pl.BlockSpec((tm, D_MODEL), lambda i, te, nv: (i, 0)),
        scratch_shapes=[
            pltpu.VMEM((D_MODEL, 2 * D_EXPERT), BF16),
            pltpu.VMEM((D_EXPERT, D_MODEL), BF16),
        ],
    )
    return pl.pallas_call(
        _expert_kernel,
        grid_spec=grid_spec,
        out_shape=jax.ShapeDtypeStruct((rows, D_MODEL), F32),
        compiler_params=pltpu.CompilerParams(
            dimension_semantics=("arbitrary",), vmem_limit_bytes=VMEM_LIMIT),
        name="moe_experts",
    )(tile_expert, n_valid, xs, w_up, b_up, w_dn, b_dn)


def _row_copy_in(ys_hbm, ybuf, sem, k, t, p):
    return pltpu.make_async_copy(ys_hbm.at[pl.ds(p, 1), :], ybuf.at[k, pl.ds(t, 1), :], sem)


def _combine_kernel(pos_ref, posn_ref, x1_ref, gate_ref, pp_ref, ps_ref, wg_ref, wp_ref, g2_ref,
                    b2_ref, *rest, tm, prompt_steps, emit_kv):
    if emit_kv:
        kvg_ref, kvb_ref, wkv_ref, ys_hbm, x3_ref, kv_ref, ybuf_a, ybuf_b, sem = rest
    else:
        ys_hbm, x3_ref, ybuf_a, ybuf_b, sem = rest
    s = pl.program_id(0)
    is_sample = s >= prompt_steps

    def issue(pref, half, ybuf, sm):
        for t in range(tm):
            for k in range(TOP_K):
                _row_copy_in(ys_hbm, ybuf, sm, k, t, pref[0, 0, (half * TOP_K + k) * tm + t]).start()

    def wait_all(ybuf, sm):
        for k in range(TOP_K):
            pltpu.make_async_copy(ys_hbm.at[pl.ds(0, tm), :], ybuf.at[k], sm).wait()

    def half_math(half, ybuf):
        rows = slice(half * tm, (half + 1) * tm)
        gate = gate_ref[rows, :]
        f = gate[:, 0:1] * ybuf[0]
        for k in range(1, TOP_K):
            f = f + gate[:, k:k + 1] * ybuf[k]
        x2 = _ln(DEEPNORM_ALPHA * x1_ref[rows, :] + f, g2_ref[...], b2_ref[...])
        p = jnp.where(is_sample, ps_ref[rows, :], pp_ref[rows, :])
        pg = jax.nn.sigmoid(_dot(x2.astype(BF16), wg_ref[...]))
        pp = _dot(p.astype(BF16), wp_ref[...])
        x3 = x2 + pg * pp
        x3_ref[rows, :] = x3
        if emit_kv:
            kv_ref[rows, :] = _dot(_ln(x3, kvg_ref[...], kvb_ref[...]).astype(BF16), wkv_ref[...])

    @pl.when(s == 0)
    def _():
        def first(t, c):
            for k in range(TOP_K):
                _row_copy_in(ys_hbm, ybuf_a, sem.at[0], k, t, pos_ref[0, 0, k * tm + t]).start()
            return c

        lax.fori_loop(0, tm, first, 0)

    wait_all(ybuf_a, sem.at[0])
    issue(pos_ref, 1, ybuf_b, sem.at[1])
    half_math(0, ybuf_a)
    wait_all(ybuf_b, sem.at[1])
    issue(posn_ref, 0, ybuf_a, sem.at[0])
    half_math(1, ybuf_b)

    @pl.when(s == pl.num_programs(0) - 1)
    def _():
        wait_all(ybuf_a, sem.at[0])


def _combine(x1, pos3, gate_t, ys, p_prompt, p_sample, layer, n_prompt, w_gate, w_proj, g2, b2,
             kv_params=None, tm=TM_TOKEN):
    n = x1.shape[0]
    emit_kv = kv_params is not None
    tm2 = 2 * tm
    steps = n // tm2
    prompt_steps = n_prompt // tm2
    sample_steps = steps - prompt_steps
    row = lambda w: pl.BlockSpec((tm2, w), lambda i: (i, 0))
    smem_pos = lambda f: pl.BlockSpec((1, 1, 2 * TOP_K * tm), lambda i: (f(i), 0, 0),
                                      memory_space=pltpu.SMEM)
    in_specs = [
        smem_pos(lambda i: i),
        smem_pos(lambda i: jnp.minimum(i + 1, steps - 1)),
        row(D_MODEL),
        row(TOP_K),
        pl.BlockSpec((tm2, PLE_DIM),
                     lambda i: (layer * prompt_steps + jnp.minimum(i, prompt_steps - 1), 0)),
        pl.BlockSpec((tm2, PLE_DIM),
                     lambda i: (layer * sample_steps + jnp.maximum(i - prompt_steps, 0), 0)),
        _full((D_MODEL, D_MODEL)),
        _full((PLE_DIM, D_MODEL)),
        _full((1, D_MODEL)),
        _full((1, D_MODEL)),
    ]
    args = [pos3, pos3, x1, gate_t, p_prompt, p_sample, w_gate, w_proj, g2, b2]
    out_specs = [row(D_MODEL)]
    out_shape = [jax.ShapeDtypeStruct((n, D_MODEL), F32)]
    if emit_kv:
        in_specs += [_full((1, D_MODEL)), _full((1, D_MODEL)), _full((D_MODEL, 2 * KV_DIM))]
        args += list(kv_params)
        out_specs.append(row(2 * KV_DIM))
        out_shape.append(jax.ShapeDtypeStruct((n, 2 * KV_DIM), F32))
    in_specs.append(pl.BlockSpec(memory_space=pl.ANY))
    args.append(ys)
    outs = pl.pallas_call(
        functools.partial(_combine_kernel, tm=tm, prompt_steps=prompt_steps, emit_kv=emit_kv),
        grid=(steps,),
        in_specs=in_specs,
        out_specs=out_specs,
        out_shape=out_shape,
        scratch_shapes=[pltpu.VMEM((TOP_K, tm, D_MODEL), F32), pltpu.VMEM((TOP_K, tm, D_MODEL), F32),
                        pltpu.SemaphoreType.DMA((2,))],
        compiler_params=pltpu.CompilerParams(
            dimension_semantics=("arbitrary",), vmem_limit_bytes=VMEM_LIMIT),
        name="moe_combine",
    )(*args)
    return outs if emit_kv else (outs[0], None)


def _tile_pos(pos, tm):
    n = pos.shape[1]
    return pos.reshape(TOP_K, n // tm, tm).transpose(1, 0, 2).reshape(n // tm, 1, TOP_K * tm)


def _pair_pos(pos, tm):
    n = pos.shape[1]
    p = pos.reshape(TOP_K, n // (2 * tm), 2, tm).transpose(1, 2, 0, 3)
    return p.reshape(n // (2 * tm), 1, 2 * TOP_K * tm)


def _moe_layer(x1, xs_buf, p_prompt, p_sample, n_prompt, layer, router_wt, router_b, tri,
               w_up, b_up, w_dn, b_dn, w_gate, w_proj, g2, b2, kv_params=None):
    idx, gate, rank, counts = _route(x1, router_wt, router_b, tri)
    padded = ((counts + TM_EXPERT - 1) // TM_EXPERT) * TM_EXPERT
    ends = jnp.cumsum(padded)
    starts = ends - padded
    onehot = idx[:, :, None] == jnp.arange(N_EXPERTS, dtype=jnp.int32)
    pos = jnp.sum(jnp.where(onehot, starts, 0), axis=-1) + rank
    n_tiles = xs_buf.shape[0] // TM_EXPERT
    n_valid = (ends[-1] // TM_EXPERT).astype(jnp.int32)
    tile_start = jnp.minimum(jnp.arange(n_tiles, dtype=jnp.int32), n_valid - 1) * TM_EXPERT
    tile_expert = jnp.sum(tile_start[:, None] >= ends[None, :], axis=1).astype(jnp.int32)
    xs_buf = _dispatch(x1, _tile_pos(pos, TM_DISPATCH), xs_buf)
    ys = _experts(xs_buf, tile_expert, n_valid.reshape(1), w_up, b_up, w_dn, b_dn, layer)
    x3, kv = _combine(x1, _pair_pos(pos, TM_TOKEN), gate.T, ys, p_prompt, p_sample, layer, n_prompt,
                      w_gate, w_proj, g2, b2, kv_params)
    return x3, kv, xs_buf


def _alibi_slopes():
    h = jnp.arange(1, N_HEADS + 1, dtype=F32)
    return 2.0 ** (-8.0 * h / N_HEADS)


def _prompt_bias():
    qi = jnp.arange(WINDOW)[:, None]
    kj = jnp.arange(2 * WINDOW)[None, :]
    dist = qi + WINDOW - kj
    valid = (dist >= 0) & (dist < WINDOW)
    slopes = _alibi_slopes()[:, None, None]
    tables = []
    for first in (True, False):
        ok = valid & (kj >= WINDOW) if first else valid
        tables.append(jnp.where(ok[None], -slopes * dist.astype(F32)[None], NEG_INF))
    return jnp.stack(tables).astype(F32)


def _sample_tables(t, sinks):
    g = jnp.arange(Q_PER_KV)[:, None, None]
    kh = jnp.arange(N_KV_HEADS)[None, :, None]
    tok = jnp.arange(t)[None, None, :]
    head = jnp.broadcast_to(kh * Q_PER_KV + g, (Q_PER_KV, N_KV_HEADS, t)).reshape(-1)
    tok = jnp.broadcast_to(tok, (Q_PER_KV, N_KV_HEADS, t)).reshape(-1)
    kj = jnp.arange(KEY_PAD)[None, :]
    dist = tok[:, None] + WINDOW - kj
    valid = (dist >= 0) & (dist < WINDOW) & (kj < WINDOW + t)
    slopes = _alibi_slopes()[head][:, None]
    bias = jnp.where(valid, -slopes * dist.astype(F32), NEG_INF).astype(F32)
    sink_col = sinks[head][:, None].astype(F32)
    row_kh = jnp.repeat(jnp.arange(N_KV_HEADS), t)[:, None]
    lane_kh = (jnp.arange(KV_DIM) // HEAD_DIM)[None, :]
    mask = (row_kh == lane_kh).astype(F32)
    return bias, sink_col, mask


def _gkd_cols(w):
    r = w.shape[0]
    w4 = w.reshape(r, N_KV_HEADS, Q_PER_KV, HEAD_DIM)
    return w4.transpose(0, 2, 1, 3).reshape(r, N_HEADS * HEAD_DIM)


def _spatial_tables(w_s, b_s, t):
    tril = jnp.tril(jnp.ones((CHUNK, CHUNK), dtype=bool))
    ws_p = jnp.where(tril[None], w_s, 0.0)
    bs_p = jnp.repeat(b_s.T, GROUP_DIM_A, axis=1)
    r = jnp.arange(CHUNK)
    same = (r[:, None] // t) == (r[None, :] // t)
    small = jnp.where(tril[None, :t, :t], w_s[:, :t, :t], 0.0)
    ws_s = jnp.where(same[None], small[:, r[:, None] % t, r[None, :] % t], 0.0)
    bs_s = jnp.repeat(b_s[:, r % t].T, GROUP_DIM_A, axis=1)
    return jnp.stack([ws_p, ws_s]).astype(BF16), jnp.stack([bs_p, bs_s]).astype(F32)


def kernel(x_prompt, x_sample, cache_k, cache_v, p_prompt, p_sample, ln1_g, ln1_b, ln2_g, ln2_b,
           a_w_in, a_b_in, a_ln_g, a_ln_b, a_w_s, a_b_s, a_w_out, kv_ln_g, kv_ln_b, w_kv,
           b_w_q, b_sinks, b_w_o, router_w, router_b, exp_w_up, exp_b_up, exp_w_dn, exp_b_dn,
           ple_w_proj, ple_w_gate):
    bsz, seq, _ = x_prompt.shape
    n_seq, t, _ = x_sample.shape
    n_p = bsz * seq
    n_s = n_seq * t
    n = n_p + n_s
    depth = ln1_g.shape[0]
    n_a = a_w_in.shape[0]

    xp2 = x_prompt.reshape(n_p, D_MODEL)
    xs2 = x_sample.reshape(n_s, D_MODEL)
    pp2 = p_prompt.reshape(depth * n_p, PLE_DIM)
    ps2 = p_sample.reshape(depth * n_s, PLE_DIM)
    row = lambda v: v.reshape(1, -1).astype(F32)

    xs_buf = jnp.zeros((n * TOP_K + N_EXPERTS * TM_EXPERT, D_MODEL), F32)
    tri = (jnp.arange(TM_ROUTER)[:, None] <= jnp.arange(TM_ROUTER)[None, :]).astype(BF16)
    b_up4 = exp_b_up.reshape(depth, N_EXPERTS, 1, 2 * D_EXPERT)
    b_dn4 = exp_b_dn.reshape(depth, N_EXPERTS, 1, D_MODEL)
    prompt_bias = _prompt_bias()

    chunk_v = []
    kv_params = (row(kv_ln_g), row(kv_ln_b), w_kv.astype(BF16))
    x = None
    kp = vp = k_buf = v_buf = None
    kpad = vpad = kbuf_pad = vbuf_pad = None
    for i in range(depth):
        if i < n_a:
            ws2, bs2 = _spatial_tables(a_w_s[i], a_b_s[i], t)
            xa, xb, xb_row0 = (xp2, xs2, 0) if i == 0 else (x, x, n_p)
            x1, v_rows = _gmlp_layer(xa, xb, xb_row0, n_p, n_s, a_w_in[i].astype(BF16),
                                     row(a_b_in[i]), row(a_ln_g[i]), row(a_ln_b[i]), ws2, bs2,
                                     a_w_out[i].astype(BF16), row(ln1_g[i]), row(ln1_b[i]))
            chunk_v.append(v_rows.reshape(n_seq, t, D_MODEL))
        else:
            j = i - n_a
            wq = b_w_q[j].astype(BF16)
            wo = b_w_o[j].astype(BF16)
            x1_p = _swa_prompt_layer(x, bsz, seq, kpad, vpad, prompt_bias,
                                     b_sinks[j].astype(F32), wq, wo, row(ln1_g[i]), row(ln1_b[i]))
            s_bias, s_sink, s_mask = _sample_tables(t, b_sinks[j])
            x1_s = _swa_sample_layer(x, n_p, n_seq, t, kbuf_pad, vbuf_pad, s_bias, s_sink, s_mask,
                                     _gkd_cols(wq), _gkd_cols(wo.T).T, row(ln1_g[i]), row(ln1_b[i]))
            x1 = jnp.concatenate([x1_p, x1_s], axis=0)
        x, kv, xs_buf = _moe_layer(
            x1, xs_buf, pp2, ps2, n_p, i, router_w[i].T, router_b[i].reshape(N_EXPERTS, 1), tri,
            exp_w_up, b_up4, exp_w_dn, b_dn4, ple_w_gate[i].astype(BF16),
            ple_w_proj[i].astype(BF16), row(ln2_g[i]), row(ln2_b[i]),
            kv_params if i == n_a - 1 else None)
        if i == n_a - 1:
            k_all, v_all = kv[:, :KV_DIM], kv[:, KV_DIM:]
            kp = k_all[:n_p].reshape(bsz, seq, KV_DIM)
            vp = v_all[:n_p].reshape(bsz, seq, KV_DIM)
            k_buf = jnp.concatenate([cache_k.reshape(n_seq, WINDOW, KV_DIM),
                                     k_all[n_p:].reshape(n_seq, t, KV_DIM)], axis=1)
            v_buf = jnp.concatenate([cache_v.reshape(n_seq, WINDOW, KV_DIM),
                                     v_all[n_p:].reshape(n_seq, t, KV_DIM)], axis=1)
            front = ((0, 0), (WINDOW, 0), (0, 0))
            kpad = jnp.pad(kp, front).astype(BF16)
            vpad = jnp.pad(vp, front).astype(BF16)
            tail = ((0, 0), (0, KEY_PAD - WINDOW - t), (0, 0))
            kbuf_pad = jnp.pad(k_buf, tail).astype(BF16)
            vbuf_pad = jnp.pad(v_buf, tail).astype(BF16)

    heads = (N_KV_HEADS, HEAD_DIM)
    return (x[:n_p].reshape(bsz, seq, D_MODEL),
            x[n_p:].reshape(n_seq, t, D_MODEL),
            jnp.stack(chunk_v),
            kp[:, -WINDOW:].reshape(bsz, WINDOW, *heads),
            vp[:, -WINDOW:].reshape(bsz, WINDOW, *heads),
            k_buf[:, -WINDOW:].reshape(n_seq, WINDOW, *heads),
            v_buf[:, -WINDOW:].reshape(n_seq, WINDOW, *heads))
```

```python
import functools

import jax
import jax.numpy as jnp
from jax import lax
from jax.experimental import pallas as pl
from jax.experimental.pallas import tpu as pltpu

F32 = jnp.float32
BF16 = jnp.bfloat16

D_MODEL = 1024
DEPTH = 4
N_A_LAYERS = 2
CHUNK = 128
N_GROUPS_A = 8
GROUP_DIM_A = D_MODEL // N_GROUPS_A
HEAD_DIM = 64
N_HEADS = 16
N_KV_HEADS = 4
Q_PER_KV = 4
KV_DIM = N_KV_HEADS * HEAD_DIM
WINDOW = 128
N_EXPERTS = 32
TOP_K = 4
D_EXPERT = 1024
SWIGLU_LIMIT = 7.0
SWIGLU_ALPHA = 1.702
PLE_DIM = 256
DEEPNORM_ALPHA = (2 * DEPTH) ** 0.25
LN_EPS = 1e-5
NEG_INF = -1e30

TM_TOKEN = 256
TM_DISPATCH = 512
DISPATCH_CHUNK = 128
TM_ROUTER = 512
TM_EXPERT = 256
SAMPLE_SEQ_BLOCK = 8
KEY_PAD = 256
VMEM_LIMIT = 48 * 1024 * 1024


def _ln(x, g, b):
    mu = jnp.mean(x, axis=-1, keepdims=True)
    xc = x - mu
    var = jnp.mean(xc * xc, axis=-1, keepdims=True)
    return xc * lax.rsqrt(var + LN_EPS) * g + b


def _dot(a, b):
    return jnp.dot(a, b, preferred_element_type=F32)


def _dot_nt(a, b):
    return lax.dot_general(a, b, (((1,), (1,)), ((), ())), preferred_element_type=F32)


def _full(shape):
    n = len(shape)
    return pl.BlockSpec(shape, lambda *_: (0,) * n)


def _gmlp_kernel(xa_ref, xb_ref, win_ref, bin_ref, lng_ref, lnb_ref, ws_ref, bs_ref, wout_ref,
                 g1_ref, b1_ref, x1_ref, v_ref, *, tm, p_tiles):
    x = jnp.where(pl.program_id(0) >= p_tiles, xb_ref[...], xa_ref[...])
    h = _dot(x.astype(BF16), win_ref[...]) + bin_ref[...]
    h = 0.5 * h * (1.0 + lax.erf(h * (2.0 ** -0.5)))
    u = h[:, :D_MODEL]
    v = _ln(h[:, D_MODEL:], lng_ref[...], lnb_ref[...])
    v_ref[...] = v
    vb = v.astype(BF16)
    rows = []
    for c in range(tm // CHUNK):
        cols = []
        for g in range(N_GROUPS_A):
            blk = vb[c * CHUNK:(c + 1) * CHUNK, g * GROUP_DIM_A:(g + 1) * GROUP_DIM_A]
            cols.append(_dot(ws_ref[g], blk))
        rows.append(jnp.concatenate(cols, axis=1) + bs_ref[...])
    s = jnp.concatenate(rows, axis=0)
    gated = (u * s).astype(BF16)
    m = _dot(gated, wout_ref[...])
    x1_ref[...] = _ln(DEEPNORM_ALPHA * x + m, g1_ref[...], b1_ref[...])


def _gmlp_layer(xa, xb, xb_row0, n_prompt, n_sample, w_in, b_in, ln_g, ln_b, ws2, bs2, w_out,
                g1, b1, tm=TM_TOKEN):
    n = n_prompt + n_sample
    p_tiles = n_prompt // tm
    n_tiles = n // tm
    s_tiles = n_tiles - p_tiles
    b_tile0 = xb_row0 // tm
    kind = lambda i: jnp.where(i >= p_tiles, 1, 0)
    x1, v = pl.pallas_call(
        functools.partial(_gmlp_kernel, tm=tm, p_tiles=p_tiles),
        grid=(n_tiles,),
        in_specs=[
            pl.BlockSpec((tm, D_MODEL), lambda i: (jnp.minimum(i, p_tiles - 1), 0)),
            pl.BlockSpec((tm, D_MODEL), lambda i: (jnp.maximum(i - p_tiles, 0) + b_tile0, 0)),
            _full((D_MODEL, 2 * D_MODEL)),
            _full((1, 2 * D_MODEL)),
            _full((1, D_MODEL)),
            _full((1, D_MODEL)),
            pl.BlockSpec((None, N_GROUPS_A, CHUNK, CHUNK), lambda i: (kind(i), 0, 0, 0)),
            pl.BlockSpec((None, CHUNK, D_MODEL), lambda i: (kind(i), 0, 0)),
            _full((D_MODEL, D_MODEL)),
            _full((1, D_MODEL)),
            _full((1, D_MODEL)),
        ],
        out_specs=[
            pl.BlockSpec((tm, D_MODEL), lambda i: (i, 0)),
            pl.BlockSpec((tm, D_MODEL), lambda i: (jnp.maximum(i - (p_tiles - 1), 0), 0)),
        ],
        out_shape=[
            jax.ShapeDtypeStruct((n, D_MODEL), F32),
            jax.ShapeDtypeStruct(((s_tiles + 1) * tm, D_MODEL), F32),
        ],
        compiler_params=pltpu.CompilerParams(
            dimension_semantics=("arbitrary",), vmem_limit_bytes=VMEM_LIMIT),
        name="gmlp_mixer",
    )(xa, xb, w_in, b_in, ln_g, ln_b, ws2, bs2, w_out, g1, b1)
    return x1, v[tm:]


def _swa_prompt_kernel(sink_ref, x_ref, k_ref, v_ref, bias_ref, wq_ref, wo_ref, g1_ref, b1_ref,
                       x1_ref, *, tq):
    i = pl.program_id(1)
    x = x_ref[...]
    qb = (_dot(x.astype(BF16), wq_ref[...]) * (HEAD_DIM ** -0.5)).astype(BF16)
    blocks = []
    for j in range(tq // WINDOW):
        blk = i * (tq // WINDOW) + j
        kstart = pl.multiple_of(blk * WINDOW, WINDOW)
        kb = k_ref[pl.ds(kstart, 2 * WINDOW), :]
        vb = v_ref[pl.ds(kstart, 2 * WINDOW), :]
        sel = jnp.where(blk == 0, 0, 1)
        outs = []
        for h in range(N_HEADS):
            kh = h // Q_PER_KV
            qh = qb[j * WINDOW:(j + 1) * WINDOW, h * HEAD_DIM:(h + 1) * HEAD_DIM]
            s = _dot_nt(qh, kb[:, kh * HEAD_DIM:(kh + 1) * HEAD_DIM])
            l = s + bias_ref[sel, h]
            sink = sink_ref[h]
            m = jnp.maximum(jnp.max(l, axis=1, keepdims=True), sink)
            p = jnp.exp(l - m)
            den = jnp.sum(p, axis=1, keepdims=True) + jnp.exp(sink - m)
            o = _dot(p.astype(BF16), vb[:, kh * HEAD_DIM:(kh + 1) * HEAD_DIM])
            outs.append(o / den)
        blocks.append(jnp.concatenate(outs, axis=1))
    attn = jnp.concatenate(blocks, axis=0)
    m_out = _dot(attn.astype(BF16), wo_ref[...])
    x1_ref[...] = _ln(DEEPNORM_ALPHA * x + m_out, g1_ref[...], b1_ref[...])


def _swa_prompt_layer(x, bsz, seq, kpad, vpad, bias, sinks, w_q, w_o, g1, b1, tq=TM_TOKEN):
    nq = seq // tq
    return pl.pallas_call(
        functools.partial(_swa_prompt_kernel, tq=tq),
        grid=(bsz, nq),
        in_specs=[
            pl.BlockSpec(memory_space=pltpu.SMEM),
            pl.BlockSpec((tq, D_MODEL), lambda b, i: (b * nq + i, 0)),
            pl.BlockSpec((None, seq + WINDOW, KV_DIM), lambda b, i: (b, 0, 0)),
            pl.BlockSpec((None, seq + WINDOW, KV_DIM), lambda b, i: (b, 0, 0)),
            _full((2, N_HEADS, WINDOW, 2 * WINDOW)),
            _full((D_MODEL, D_MODEL)),
            _full((D_MODEL, D_MODEL)),
            _full((1, D_MODEL)),
            _full((1, D_MODEL)),
        ],
        out_specs=pl.BlockSpec((tq, D_MODEL), lambda b, i: (b * nq + i, 0)),
        out_shape=jax.ShapeDtypeStruct((bsz * seq, D_MODEL), F32),
        compiler_params=pltpu.CompilerParams(
            dimension_semantics=("arbitrary", "arbitrary"), vmem_limit_bytes=VMEM_LIMIT),
        name="swa_prompt",
    )(sinks, x, kpad, vpad, bias, w_q, w_o, g1, b1)


def _swa_sample_kernel(x_ref, k_ref, v_ref, bias_ref, sink_ref, mask_ref, wq_ref, wo_ref,
                       g1_ref, b1_ref, x1_ref, *, sb, t):
    x = x_ref[...]
    q = _dot(x.astype(BF16), wq_ref[...]) * (HEAD_DIM ** -0.5)
    mask = mask_ref[...]
    bias = bias_ref[...]
    sink = sink_ref[...]
    outs = []
    for s in range(sb):
        qs = q[s * t:(s + 1) * t, :]
        parts = []
        for g in range(Q_PER_KV):
            qg = qs[:, g * KV_DIM:(g + 1) * KV_DIM]
            parts.append(jnp.concatenate([qg] * N_KV_HEADS, axis=0) * mask)
        qexp = jnp.concatenate(parts, axis=0).astype(BF16)
        l = _dot_nt(qexp, k_ref[s]) + bias
        m = jnp.maximum(jnp.max(l, axis=1, keepdims=True), sink)
        p = jnp.exp(l - m)
        den = jnp.sum(p, axis=1, keepdims=True) + jnp.exp(sink - m)
        r = _dot(p.astype(BF16), v_ref[s]) / den
        og = []
        for g in range(Q_PER_KV):
            rg = r[g * N_KV_HEADS * t:(g + 1) * N_KV_HEADS * t, :] * mask
            acc = rg[0:t]
            for kh in range(1, N_KV_HEADS):
                acc = acc + rg[kh * t:(kh + 1) * t]
            og.append(acc)
        outs.append(jnp.concatenate(og, axis=1))
    attn = jnp.concatenate(outs, axis=0)
    m_out = _dot(attn.astype(BF16), wo_ref[...])
    x1_ref[...] = _ln(DEEPNORM_ALPHA * x + m_out, g1_ref[...], b1_ref[...])


def _swa_sample_layer(x, row0, n_seq, t, kbuf, vbuf, bias, sink_col, mask, w_q, w_o, g1, b1,
                      sb=SAMPLE_SEQ_BLOCK):
    rows = sb * t
    hr = N_HEADS * t
    blk0 = row0 // rows
    return pl.pallas_call(
        functools.partial(_swa_sample_kernel, sb=sb, t=t),
        grid=(n_seq // sb,),
        in_specs=[
            pl.BlockSpec((rows, D_MODEL), lambda i: (blk0 + i, 0)),
            pl.BlockSpec((sb, KEY_PAD, KV_DIM), lambda i: (i, 0, 0)),
            pl.BlockSpec((sb, KEY_PAD, KV_DIM), lambda i: (i, 0, 0)),
            _full((hr, KEY_PAD)),
            _full((hr, 1)),
            _full((N_KV_HEADS * t, KV_DIM)),
            _full((D_MODEL, D_MODEL)),
            _full((D_MODEL, D_MODEL)),
            _full((1, D_MODEL)),
            _full((1, D_MODEL)),
        ],
        out_specs=pl.BlockSpec((rows, D_MODEL), lambda i: (i, 0)),
        out_shape=jax.ShapeDtypeStruct((n_seq * t, D_MODEL), F32),
        compiler_params=pltpu.CompilerParams(
            dimension_semantics=("arbitrary",), vmem_limit_bytes=VMEM_LIMIT),
        name="swa_sample",
    )(x, kbuf, vbuf, bias, sink_col, mask, w_q, w_o, g1, b1)


def _router_kernel(x_ref, wt_ref, b_ref, tri_ref, idx_ref, gate_ref, rank_ref, cnt_ref, carry,
                   *, tm):
    i = pl.program_id(0)

    @pl.when(i == 0)
    def _():
        carry[...] = jnp.zeros_like(carry)

    logits = lax.dot_general(wt_ref[...], x_ref[...], (((1,), (1,)), ((), ())),
                             precision=lax.Precision.HIGHEST,
                             preferred_element_type=F32) + b_ref[...]
    eidx = lax.broadcasted_iota(jnp.int32, (N_EXPERTS, tm), 0)
    l = logits
    vals, idxs, sels = [], [], []
    for _ in range(TOP_K):
        m = jnp.max(l, axis=0, keepdims=True)
        idx = jnp.min(jnp.where(l == m, eidx, N_EXPERTS), axis=0, keepdims=True)
        sel = eidx == idx
        vals.append(m)
        idxs.append(idx)
        sels.append(sel)
        l = jnp.where(sel, -jnp.inf, l)
    exps = [jnp.exp(v - vals[0]) for v in vals]
    den = exps[0] + exps[1] + exps[2] + exps[3]
    onehot = jnp.zeros((N_EXPERTS, tm), F32)
    for sel in sels:
        onehot = onehot + sel.astype(F32)
    incl = _dot(onehot.astype(BF16), tri_ref[...])
    excl = incl - onehot + carry[...]
    for k in range(TOP_K):
        idx_ref[k:k + 1, :] = idxs[k]
        gate_ref[k:k + 1, :] = exps[k] / den
        rank = jnp.sum(jnp.where(sels[k], excl, 0.0), axis=0, keepdims=True)
        rank_ref[k:k + 1, :] = rank.astype(jnp.int32)
    carry[...] = carry[...] + jnp.sum(onehot, axis=1, keepdims=True)
    cnt_ref[...] = jnp.broadcast_to(carry[...], cnt_ref.shape)


def _route(x1, router_wt, router_b, tri, tm=TM_ROUTER):
    n = x1.shape[0]
    spec_kn = pl.BlockSpec((TOP_K, tm), lambda i: (0, i))
    idx, gate, rank, cnt = pl.pallas_call(
        functools.partial(_router_kernel, tm=tm),
        grid=(n // tm,),
        in_specs=[
            pl.BlockSpec((tm, D_MODEL), lambda i: (i, 0)),
            _full((N_EXPERTS, D_MODEL)),
            _full((N_EXPERTS, 1)),
            _full((tm, tm)),
        ],
        out_specs=[spec_kn, spec_kn, spec_kn, _full((N_EXPERTS, 128))],
        out_shape=[
            jax.ShapeDtypeStruct((TOP_K, n), jnp.int32),
            jax.ShapeDtypeStruct((TOP_K, n), F32),
            jax.ShapeDtypeStruct((TOP_K, n), jnp.int32),
            jax.ShapeDtypeStruct((N_EXPERTS, 128), F32),
        ],
        scratch_shapes=[pltpu.VMEM((N_EXPERTS, 1), F32)],
        compiler_params=pltpu.CompilerParams(
            dimension_semantics=("arbitrary",), vmem_limit_bytes=VMEM_LIMIT),
        name="moe_router",
    )(x1, router_wt, router_b, tri)
    return idx, gate, rank, cnt[:, 0].astype(jnp.int32)


def _row_copy_out(x_ref, xs_hbm, sem, t, p):
    return pltpu.make_async_copy(x_ref.at[pl.ds(t, 1), :], xs_hbm.at[pl.ds(p, 1), :], sem)


def _dispatch_kernel(pos_ref, x_ref, xs_in_hbm, xs_hbm, sem, *, tm, chunk):
    del xs_in_hbm

    def issue(c, carry):
        t0 = c * chunk
        for t in range(chunk):
            for k in range(TOP_K):
                _row_copy_out(x_ref, xs_hbm, sem, t0 + t,
                              pos_ref[0, 0, k * tm + t0 + t]).start(priority=k % 2)
        return carry

    lax.fori_loop(0, tm // chunk, issue, 0)
    for _ in range(TOP_K):
        pltpu.make_async_copy(x_ref, xs_hbm.at[pl.ds(0, tm), :], sem).wait()


def _dispatch(x1, pos3, xs_buf, tm=TM_DISPATCH, chunk=DISPATCH_CHUNK):
    n = x1.shape[0]
    return pl.pallas_call(
        functools.partial(_dispatch_kernel, tm=tm, chunk=chunk),
        grid=(n // tm,),
        in_specs=[
            pl.BlockSpec((1, 1, TOP_K * tm), lambda i: (i, 0, 0), memory_space=pltpu.SMEM),
            pl.BlockSpec((tm, D_MODEL), lambda i: (i, 0)),
            pl.BlockSpec(memory_space=pl.ANY),
        ],
        out_specs=pl.BlockSpec(memory_space=pl.ANY),
        out_shape=jax.ShapeDtypeStruct(xs_buf.shape, xs_buf.dtype),
        scratch_shapes=[pltpu.SemaphoreType.DMA],
        input_output_aliases={2: 0},
        compiler_params=pltpu.CompilerParams(
            dimension_semantics=("arbitrary",), vmem_limit_bytes=VMEM_LIMIT,
            has_side_effects=True),
        name="moe_dispatch",
    )(pos3, x1, xs_buf)


def _expert_kernel(te_ref, nv_ref, xs_ref, wup_ref, bup_ref, wdn_ref, bdn_ref, y_ref,
                   wup_bf, wdn_bf):
    i = pl.program_id(0)
    e = te_ref[i]
    prev = te_ref[jnp.maximum(i - 1, 0)]

    @pl.when((i == 0) | (e != prev))
    def _():
        rb = 128

        def cast_up(r, c):
            r0 = pl.multiple_of(r * rb, rb)
            wup_bf[pl.ds(r0, rb), :] = wup_ref[pl.ds(r0, rb), :].astype(BF16)
            return c

        def cast_dn(r, c):
            r0 = pl.multiple_of(r * rb, rb)
            wdn_bf[pl.ds(r0, rb), :] = wdn_ref[pl.ds(r0, rb), :].astype(BF16)
            return c

        lax.fori_loop(0, D_MODEL // rb, cast_up, 0)
        lax.fori_loop(0, D_EXPERT // rb, cast_dn, 0)

    @pl.when(i < nv_ref[0])
    def _():
        x = xs_ref[...].astype(BF16)
        h = _dot(x, wup_bf[...]) + bup_ref[...]
        glu = jnp.minimum(h[:, :D_EXPERT], SWIGLU_LIMIT)
        lin = jnp.clip(h[:, D_EXPERT:], -SWIGLU_LIMIT, SWIGLU_LIMIT)
        act = glu * jax.nn.sigmoid(SWIGLU_ALPHA * glu) * (lin + 1.0)
        y_ref[...] = _dot(act.astype(BF16), wdn_bf[...]) + bdn_ref[...]

    @pl.when(i >= nv_ref[0])
    def _():
        y_ref[...] = jnp.zeros_like(y_ref)


def _experts(xs, tile_expert, n_valid, w_up, b_up, w_dn, b_dn, layer, tm=TM_EXPERT):
    rows = xs.shape[0]
    row_map = lambda i, te, nv: (jnp.minimum(i, nv[0] - 1), 0)
    grid_spec = pltpu.PrefetchScalarGridSpec(
        num_scalar_prefetch=2,
        grid=(rows // tm,),
        in_specs=[
            pl.BlockSpec((tm, D_MODEL), row_map),
            pl.BlockSpec((None, None, D_MODEL, 2 * D_EXPERT), lambda i, te, nv: (layer, te[i], 0, 0)),
            pl.BlockSpec((None, None, 1, 2 * D_EXPERT), lambda i, te, nv: (layer, te[i], 0, 0)),
            pl.BlockSpec((None, None, D_EXPERT, D_MODEL), lambda i, te, nv: (layer, te[i], 0, 0)),
            pl.BlockSpec((None, None, 1, D_MODEL), lambda i, te, nv: (layer, te[i], 0, 0)),
        ],
        out_specs=pl.BlockSpec((tm, D_MODEL), lambda i, te, nv: (i, 0)),
        scratch_shapes=[
            pltpu.VMEM((D_MODEL, 2 * D_EXPERT), BF16),
            pltpu.VMEM((D_EXPERT, D_MODEL), BF16),
        ],
    )
    return pl.pallas_call(
        _expert_kernel,
        grid_spec=grid_spec,
        out_shape=jax.ShapeDtypeStruct((rows, D_MODEL), F32),
        compiler_params=pltpu.CompilerParams(
            dimension_semantics=("arbitrary",), vmem_limit_bytes=VMEM_LIMIT),
        name="moe_experts",
    )(tile_expert, n_valid, xs, w_up, b_up, w_dn, b_dn)


def _row_copy_in(ys_hbm, ybuf, sem, k, t, p):
    return pltpu.make_async_copy(ys_hbm.at[pl.ds(p, 1), :], ybuf.at[k, pl.ds(t, 1), :], sem)


def _combine_kernel(pos_ref, posn_ref, x1_ref, gate_ref, pp_ref, ps_ref, wg_ref, wp_ref, g2_ref,
                    b2_ref, *rest, tm, prompt_steps, emit_kv):
    if emit_kv:
        kvg_ref, kvb_ref, wkv_ref, ys_hbm, x3_ref, kv_ref, ybuf_a, ybuf_b, sem = rest
    else:
        ys_hbm, x3_ref, ybuf_a, ybuf_b, sem = rest
    s = pl.program_id(0)
    is_sample = s >= prompt_steps

    def issue(pref, half, ybuf, sm):
        for t in range(tm):
            for k in range(TOP_K):
                _row_copy_in(ys_hbm, ybuf, sm, k, t,
                             pref[0, 0, (half * TOP_K + k) * tm + t]).start(priority=k % 2)

    def wait_all(ybuf, sm):
        for k in range(TOP_K):
            pltpu.make_async_copy(ys_hbm.at[pl.ds(0, tm), :], ybuf.at[k], sm).wait()

    def half_math(half, ybuf):
        rows = slice(half * tm, (half + 1) * tm)
        gate = gate_ref[rows, :]
        f = gate[:, 0:1] * ybuf[0]
        for k in range(1, TOP_K):
            f = f + gate[:, k:k + 1] * ybuf[k]
        x2 = _ln(DEEPNORM_ALPHA * x1_ref[rows, :] + f, g2_ref[...], b2_ref[...])
        p = jnp.where(is_sample, ps_ref[rows, :], pp_ref[rows, :])
        pg = jax.nn.sigmoid(_dot(x2.astype(BF16), wg_ref[...]))
        pp = _dot(p.astype(BF16), wp_ref[...])
        x3 = x2 + pg * pp
        x3_ref[rows, :] = x3
        if emit_kv:
            kv_ref[rows, :] = _dot(_ln(x3, kvg_ref[...], kvb_ref[...]).astype(BF16), wkv_ref[...])

    @pl.when(s == 0)
    def _():
        def first(t, c):
            for k in range(TOP_K):
                _row_copy_in(ys_hbm, ybuf_a, sem.at[0], k, t, pos_ref[0, 0, k * tm + t]).start()
            return c

        lax.fori_loop(0, tm, first, 0)

    wait_all(ybuf_a, sem.at[0])
    issue(pos_ref, 1, ybuf_b, sem.at[1])
    half_math(0, ybuf_a)
    wait_all(ybuf_b, sem.at[1])
    issue(posn_ref, 0, ybuf_a, sem.at[0])
    half_math(1, ybuf_b)

    @pl.when(s == pl.num_programs(0) - 1)
    def _():
        wait_all(ybuf_a, sem.at[0])


def _combine(x1, pos3, gate_t, ys, p_prompt, p_sample, layer, n_prompt, w_gate, w_proj, g2, b2,
             kv_params=None, tm=TM_TOKEN):
    n = x1.shape[0]
    emit_kv = kv_params is not None
    tm2 = 2 * tm
    steps = n // tm2
    prompt_steps = n_prompt // tm2
    sample_steps = steps - prompt_steps
    row = lambda w: pl.BlockSpec((tm2, w), lambda i: (i, 0))
    smem_pos = lambda f: pl.BlockSpec((1, 1, 2 * TOP_K * tm), lambda i: (f(i), 0, 0),
                                      memory_space=pltpu.SMEM)
    in_specs = [
        smem_pos(lambda i: i),
        smem_pos(lambda i: jnp.minimum(i + 1, steps - 1)),
        row(D_MODEL),
        row(TOP_K),
        pl.BlockSpec((tm2, PLE_DIM),
                     lambda i: (layer * prompt_steps + jnp.minimum(i, prompt_steps - 1), 0)),
        pl.BlockSpec((tm2, PLE_DIM),
                     lambda i: (layer * sample_steps + jnp.maximum(i - prompt_steps, 0), 0)),
        _full((D_MODEL, D_MODEL)),
        _full((PLE_DIM, D_MODEL)),
        _full((1, D_MODEL)),
        _full((1, D_MODEL)),
    ]
    args = [pos3, pos3, x1, gate_t, p_prompt, p_sample, w_gate, w_proj, g2, b2]
    out_specs = [row(D_MODEL)]
    out_shape = [jax.ShapeDtypeStruct((n, D_MODEL), F32)]
    if emit_kv:
        in_specs += [_full((1, D_MODEL)), _full((1, D_MODEL)), _full((D_MODEL, 2 * KV_DIM))]
        args += list(kv_params)
        out_specs.append(row(2 * KV_DIM))
        out_shape.append(jax.ShapeDtypeStruct((n, 2 * KV_DIM), F32))
    in_specs.append(pl.BlockSpec(memory_space=pl.ANY))
    args.append(ys)
    outs = pl.pallas_call(
        functools.partial(_combine_kernel, tm=tm, prompt_steps=prompt_steps, emit_kv=emit_kv),
        grid=(steps,),
        in_specs=in_specs,
        out_specs=out_specs,
        out_shape=out_shape,
        scratch_shapes=[pltpu.VMEM((TOP_K, tm, D_MODEL), F32), pltpu.VMEM((TOP_K, tm, D_MODEL), F32),
                        pltpu.SemaphoreType.DMA((2,))],
        compiler_params=pltpu.CompilerParams(
            dimension_semantics=("arbitrary",), vmem_limit_bytes=VMEM_LIMIT),
        name="moe_combine",
    )(*args)
    return outs if emit_kv else (outs[0], None)


def _tile_pos(pos, tm):
    n = pos.shape[1]
    return pos.reshape(TOP_K, n // tm, tm).transpose(1, 0, 2).reshape(n // tm, 1, TOP_K * tm)


def _pair_pos(pos, tm):
    n = pos.shape[1]
    p = pos.reshape(TOP_K, n // (2 * tm), 2, tm).transpose(1, 2, 0, 3)
    return p.reshape(n // (2 * tm), 1, 2 * TOP_K * tm)


def _moe_layer(x1, xs_buf, p_prompt, p_sample, n_prompt, layer, router_wt, router_b, tri,
               w_up, b_up, w_dn, b_dn, w_gate, w_proj, g2, b2, kv_params=None):
    idx, gate, rank, counts = _route(x1, router_wt, router_b, tri)
    padded = ((counts + TM_EXPERT - 1) // TM_EXPERT) * TM_EXPERT
    ends = jnp.cumsum(padded)
    starts = ends - padded
    onehot = idx[:, :, None] == jnp.arange(N_EXPERTS, dtype=jnp.int32)
    pos = jnp.sum(jnp.where(onehot, starts, 0), axis=-1) + rank
    n_tiles = xs_buf.shape[0] // TM_EXPERT
    n_valid = (ends[-1] // TM_EXPERT).astype(jnp.int32)
    tile_start = jnp.minimum(jnp.arange(n_tiles, dtype=jnp.int32), n_valid - 1) * TM_EXPERT
    tile_expert = jnp.sum(tile_start[:, None] >= ends[None, :], axis=1).astype(jnp.int32)
    xs_buf = _dispatch(x1, _tile_pos(pos, TM_DISPATCH), xs_buf)
    ys = _experts(xs_buf, tile_expert, n_valid.reshape(1), w_up, b_up, w_dn, b_dn, layer)
    x3, kv = _combine(x1, _pair_pos(pos, TM_TOKEN), gate.T, ys, p_prompt, p_sample, layer, n_prompt,
                      w_gate, w_proj, g2, b2, kv_params)
    return x3, kv, xs_buf


def _alibi_slopes():
    h = jnp.arange(1, N_HEADS + 1, dtype=F32)
    return 2.0 ** (-8.0 * h / N_HEADS)


def _prompt_bias():
    qi = jnp.arange(WINDOW)[:, None]
    kj = jnp.arange(2 * WINDOW)[None, :]
    dist = qi + WINDOW - kj
    valid = (dist >= 0) & (dist < WINDOW)
    slopes = _alibi_slopes()[:, None, None]
    tables = []
    for first in (True, False):
        ok = valid & (kj >= WINDOW) if first else valid
        tables.append(jnp.where(ok[None], -slopes * dist.astype(F32)[None], NEG_INF))
    return jnp.stack(tables).astype(F32)


def _sample_tables(t, sinks):
    g = jnp.arange(Q_PER_KV)[:, None, None]
    kh = jnp.arange(N_KV_HEADS)[None, :, None]
    tok = jnp.arange(t)[None, None, :]
    head = jnp.broadcast_to(kh * Q_PER_KV + g, (Q_PER_KV, N_KV_HEADS, t)).reshape(-1)
    tok = jnp.broadcast_to(tok, (Q_PER_KV, N_KV_HEADS, t)).reshape(-1)
    kj = jnp.arange(KEY_PAD)[None, :]
    dist = tok[:, None] + WINDOW - kj
    valid = (dist >= 0) & (dist < WINDOW) & (kj < WINDOW + t)
    slopes = _alibi_slopes()[head][:, None]
    bias = jnp.where(valid, -slopes * dist.astype(F32), NEG_INF).astype(F32)
    sink_col = sinks[head][:, None].astype(F32)
    row_kh = jnp.repeat(jnp.arange(N_KV_HEADS), t)[:, None]
    lane_kh = (jnp.arange(KV_DIM) // HEAD_DIM)[None, :]
    mask = (row_kh == lane_kh).astype(F32)
    return bias, sink_col, mask


def _gkd_cols(w):
    r = w.shape[0]
    w4 = w.reshape(r, N_KV_HEADS, Q_PER_KV, HEAD_DIM)
    return w4.transpose(0, 2, 1, 3).reshape(r, N_HEADS * HEAD_DIM)


def _spatial_tables(w_s, b_s, t):
    tril = jnp.tril(jnp.ones((CHUNK, CHUNK), dtype=bool))
    ws_p = jnp.where(tril[None], w_s, 0.0)
    bs_p = jnp.repeat(b_s.T, GROUP_DIM_A, axis=1)
    r = jnp.arange(CHUNK)
    same = (r[:, None] // t) == (r[None, :] // t)
    small = jnp.where(tril[None, :t, :t], w_s[:, :t, :t], 0.0)
    ws_s = jnp.where(same[None], jnp.tile(small, (1, CHUNK // t, CHUNK // t)), 0.0)
    bs_s = jnp.repeat(jnp.tile(b_s[:, :t], (1, CHUNK // t)).T, GROUP_DIM_A, axis=1)
    return jnp.stack([ws_p, ws_s]).astype(BF16), jnp.stack([bs_p, bs_s]).astype(F32)


def kernel(x_prompt, x_sample, cache_k, cache_v, p_prompt, p_sample, ln1_g, ln1_b, ln2_g, ln2_b,
           a_w_in, a_b_in, a_ln_g, a_ln_b, a_w_s, a_b_s, a_w_out, kv_ln_g, kv_ln_b, w_kv,
           b_w_q, b_sinks, b_w_o, router_w, router_b, exp_w_up, exp_b_up, exp_w_dn, exp_b_dn,
           ple_w_proj, ple_w_gate):
    bsz, seq, _ = x_prompt.shape
    n_seq, t, _ = x_sample.shape
    n_p = bsz * seq
    n_s = n_seq * t
    n = n_p + n_s
    depth = ln1_g.shape[0]
    n_a = a_w_in.shape[0]

    xp2 = x_prompt.reshape(n_p, D_MODEL)
    xs2 = x_sample.reshape(n_s, D_MODEL)
    pp2 = p_prompt.reshape(depth * n_p, PLE_DIM)
    ps2 = p_sample.reshape(depth * n_s, PLE_DIM)
    row = lambda v: v.reshape(1, -1).astype(F32)

    xs_buf = jnp.zeros((n * TOP_K + N_EXPERTS * TM_EXPERT, D_MODEL), F32)
    tri = (jnp.arange(TM_ROUTER)[:, None] <= jnp.arange(TM_ROUTER)[None, :]).astype(BF16)
    b_up4 = exp_b_up.reshape(depth, N_EXPERTS, 1, 2 * D_EXPERT)
    b_dn4 = exp_b_dn.reshape(depth, N_EXPERTS, 1, D_MODEL)
    prompt_bias = _prompt_bias()

    chunk_v = []
    kv_params = (row(kv_ln_g), row(kv_ln_b), w_kv.astype(BF16))
    x = None
    kp = vp = k_buf = v_buf = None
    kpad = vpad = kbuf_pad = vbuf_pad = None
    for i in range(depth):
        if i < n_a:
            ws2, bs2 = _spatial_tables(a_w_s[i], a_b_s[i], t)
            xa, xb, xb_row0 = (xp2, xs2, 0) if i == 0 else (x, x, n_p)
            x1, v_rows = _gmlp_layer(xa, xb, xb_row0, n_p, n_s, a_w_in[i].astype(BF16),
                                     row(a_b_in[i]), row(a_ln_g[i]), row(a_ln_b[i]), ws2, bs2,
                                     a_w_out[i].astype(BF16), row(ln1_g[i]), row(ln1_b[i]))
            chunk_v.append(v_rows.reshape(n_seq, t, D_MODEL))
        else:
            j = i - n_a
            wq = b_w_q[j].astype(BF16)
            wo = b_w_o[j].astype(BF16)
            x1_p = _swa_prompt_layer(x, bsz, seq, kpad, vpad, prompt_bias,
                                     b_sinks[j].astype(F32), wq, wo, row(ln1_g[i]), row(ln1_b[i]))
            s_bias, s_sink, s_mask = _sample_tables(t, b_sinks[j])
            x1_s = _swa_sample_layer(x, n_p, n_seq, t, kbuf_pad, vbuf_pad, s_bias, s_sink, s_mask,
                                     _gkd_cols(wq), _gkd_cols(wo.T).T, row(ln1_g[i]), row(ln1_b[i]))
            x1 = jnp.concatenate([x1_p, x1_s], axis=0)
        x, kv, xs_buf = _moe_layer(
            x1, xs_buf, pp2, ps2, n_p, i, router_w[i].T, router_b[i].reshape(N_EXPERTS, 1), tri,
            exp_w_up, b_up4, exp_w_dn, b_dn4, ple_w_gate[i].astype(BF16),
            ple_w_proj[i].astype(BF16), row(ln2_g[i]), row(ln2_b[i]),
            kv_params if i == n_a - 1 else None)
        if i == n_a - 1:
            k_all, v_all = kv[:, :KV_DIM], kv[:, KV_DIM:]
            kp = k_all[:n_p].reshape(bsz, seq, KV_DIM)
            vp = v_all[:n_p].reshape(bsz, seq, KV_DIM)
            k_buf = jnp.concatenate([cache_k.reshape(n_seq, WINDOW, KV_DIM),
                                     k_all[n_p:].reshape(n_seq, t, KV_DIM)], axis=1)
            v_buf = jnp.concatenate([cache_v.reshape(n_seq, WINDOW, KV_DIM),
                                     v_all[n_p:].reshape(n_seq, t, KV_DIM)], axis=1)
            front = ((0, 0), (WINDOW, 0), (0, 0))
            kpad = jnp.pad(kp, front).astype(BF16)
            vpad = jnp.pad(vp, front).astype(BF16)
            tail = ((0, 0), (0, KEY_PAD - WINDOW - t), (0, 0))
            kbuf_pad = jnp.pad(k_buf, tail).astype(BF16)
            vbuf_pad = jnp.pad(v_buf, tail).astype(BF16)

    heads = (N_KV_HEADS, HEAD_DIM)
    return (x[:n_p].reshape(bsz, seq, D_MODEL),
            x[n_p:].reshape(n_seq, t, D_MODEL),
            jnp.stack(chunk_v),
            kp[:, -WINDOW:].reshape(bsz, WINDOW, *heads),
            vp[:, -WINDOW:].reshape(bsz, WINDOW, *heads),
            k_buf[:, -WINDOW:].reshape(n_seq, WINDOW, *heads),
            v_buf[:, -WINDOW:].reshape(n_seq, WINDOW, *heads))
```

```python
import functools

import jax
import jax.numpy as jnp
from jax import lax
from jax.experimental import pallas as pl
from jax.experimental.pallas import tpu as pltpu

F32 = jnp.float32
BF16 = jnp.bfloat16

D_MODEL = 1024
DEPTH = 4
N_A_LAYERS = 2
CHUNK = 128
N_GROUPS_A = 8
GROUP_DIM_A = D_MODEL // N_GROUPS_A
HEAD_DIM = 64
N_HEADS = 16
N_KV_HEADS = 4
Q_PER_KV = 4
KV_DIM = N_KV_HEADS * HEAD_DIM
WINDOW = 128
N_EXPERTS = 32
TOP_K = 4
D_EXPERT = 1024
SWIGLU_LIMIT = 7.0
SWIGLU_ALPHA = 1.702
PLE_DIM = 256
DEEPNORM_ALPHA = (2 * DEPTH) ** 0.25
LN_EPS = 1e-5
NEG_INF = -1e30

TM_TOKEN = 256
TM_DISPATCH = 512
INVERT_CHUNK = 64
TM_ROUTER = 512
TM_EXPERT = 512
SAMPLE_SEQ_BLOCK = 8
KEY_PAD = 256
VMEM_LIMIT = 48 * 1024 * 1024


def _ln(x, g, b):
    mu = jnp.mean(x, axis=-1, keepdims=True)
    xc = x - mu
    var = jnp.mean(xc * xc, axis=-1, keepdims=True)
    return xc * lax.rsqrt(var + LN_EPS) * g + b


def _dot(a, b):
    return jnp.dot(a, b, preferred_element_type=F32)


def _dot_nt(a, b):
    return lax.dot_general(a, b, (((1,), (1,)), ((), ())), preferred_element_type=F32)


def _full(shape):
    n = len(shape)
    return pl.BlockSpec(shape, lambda *_: (0,) * n)


def _gmlp_kernel(xa_ref, xb_ref, win_ref, bin_ref, lng_ref, lnb_ref, ws_ref, bs_ref, wout_ref,
                 g1_ref, b1_ref, x1_ref, v_ref, *, tm, p_tiles):
    x = jnp.where(pl.program_id(0) >= p_tiles, xb_ref[...], xa_ref[...])
    h = _dot(x.astype(BF16), win_ref[...]) + bin_ref[...]
    h = 0.5 * h * (1.0 + lax.erf(h * (2.0 ** -0.5)))
    u = h[:, :D_MODEL]
    v = _ln(h[:, D_MODEL:], lng_ref[...], lnb_ref[...])
    v_ref[...] = v
    vb = v.astype(BF16)
    rows = []
    for c in range(tm // CHUNK):
        cols = []
        for g in range(N_GROUPS_A):
            blk = vb[c * CHUNK:(c + 1) * CHUNK, g * GROUP_DIM_A:(g + 1) * GROUP_DIM_A]
            cols.append(_dot(ws_ref[g], blk))
        rows.append(jnp.concatenate(cols, axis=1) + bs_ref[...])
    s = jnp.concatenate(rows, axis=0)
    gated = (u * s).astype(BF16)
    m = _dot(gated, wout_ref[...])
    x1_ref[...] = _ln(DEEPNORM_ALPHA * x + m, g1_ref[...], b1_ref[...])


def _gmlp_layer(xa, xb, xb_row0, n_prompt, n_sample, w_in, b_in, ln_g, ln_b, ws2, bs2, w_out,
                g1, b1, tm=TM_TOKEN):
    n = n_prompt + n_sample
    p_tiles = n_prompt // tm
    n_tiles = n // tm
    s_tiles = n_tiles - p_tiles
    b_tile0 = xb_row0 // tm
    kind = lambda i: jnp.where(i >= p_tiles, 1, 0)
    x1, v = pl.pallas_call(
        functools.partial(_gmlp_kernel, tm=tm, p_tiles=p_tiles),
        grid=(n_tiles,),
        in_specs=[
            pl.BlockSpec((tm, D_MODEL), lambda i: (jnp.minimum(i, p_tiles - 1), 0)),
            pl.BlockSpec((tm, D_MODEL), lambda i: (jnp.maximum(i - p_tiles, 0) + b_tile0, 0)),
            _full((D_MODEL, 2 * D_MODEL)),
            _full((1, 2 * D_MODEL)),
            _full((1, D_MODEL)),
            _full((1, D_MODEL)),
            pl.BlockSpec((None, N_GROUPS_A, CHUNK, CHUNK), lambda i: (kind(i), 0, 0, 0)),
            pl.BlockSpec((None, CHUNK, D_MODEL), lambda i: (kind(i), 0, 0)),
            _full((D_MODEL, D_MODEL)),
            _full((1, D_MODEL)),
            _full((1, D_MODEL)),
        ],
        out_specs=[
            pl.BlockSpec((tm, D_MODEL), lambda i: (i, 0)),
            pl.BlockSpec((tm, D_MODEL), lambda i: (jnp.maximum(i - (p_tiles - 1), 0), 0)),
        ],
        out_shape=[
            jax.ShapeDtypeStruct((n, D_MODEL), F32),
            jax.ShapeDtypeStruct(((s_tiles + 1) * tm, D_MODEL), F32),
        ],
        compiler_params=pltpu.CompilerParams(
            dimension_semantics=("arbitrary",), vmem_limit_bytes=VMEM_LIMIT),
        name="gmlp_mixer",
    )(xa, xb, w_in, b_in, ln_g, ln_b, ws2, bs2, w_out, g1, b1)
    return x1, v[tm:]


def _swa_prompt_kernel(sink_ref, x_ref, k_ref, v_ref, bias_ref, wq_ref, wo_ref, g1_ref, b1_ref,
                       x1_ref, *, tq):
    i = pl.program_id(1)
    x = x_ref[...]
    qb = (_dot(x.astype(BF16), wq_ref[...]) * (HEAD_DIM ** -0.5)).astype(BF16)
    blocks = []
    for j in range(tq // WINDOW):
        blk = i * (tq // WINDOW) + j
        kstart = pl.multiple_of(blk * WINDOW, WINDOW)
        kb = k_ref[pl.ds(kstart, 2 * WINDOW), :]
        vb = v_ref[pl.ds(kstart, 2 * WINDOW), :]
        sel = jnp.where(blk == 0, 0, 1)
        outs = []
        for h in range(N_HEADS):
            kh = h // Q_PER_KV
            qh = qb[j * WINDOW:(j + 1) * WINDOW, h * HEAD_DIM:(h + 1) * HEAD_DIM]
            s = _dot_nt(qh, kb[:, kh * HEAD_DIM:(kh + 1) * HEAD_DIM])
            l = s + bias_ref[sel, h]
            sink = sink_ref[h]
            m = jnp.maximum(jnp.max(l, axis=1, keepdims=True), sink)
            p = jnp.exp(l - m)
            den = jnp.sum(p, axis=1, keepdims=True) + jnp.exp(sink - m)
            o = _dot(p.astype(BF16), vb[:, kh * HEAD_DIM:(kh + 1) * HEAD_DIM])
            outs.append(o / den)
        blocks.append(jnp.concatenate(outs, axis=1))
    attn = jnp.concatenate(blocks, axis=0)
    m_out = _dot(attn.astype(BF16), wo_ref[...])
    x1_ref[...] = _ln(DEEPNORM_ALPHA * x + m_out, g1_ref[...], b1_ref[...])


def _swa_prompt_layer(x, bsz, seq, kpad, vpad, bias, sinks, w_q, w_o, g1, b1, tq=TM_TOKEN):
    nq = seq // tq
    return pl.pallas_call(
        functools.partial(_swa_prompt_kernel, tq=tq),
        grid=(bsz, nq),
        in_specs=[
            pl.BlockSpec(memory_space=pltpu.SMEM),
            pl.BlockSpec((tq, D_MODEL), lambda b, i: (b * nq + i, 0)),
            pl.BlockSpec((None, seq + WINDOW, KV_DIM), lambda b, i: (b, 0, 0)),
            pl.BlockSpec((None, seq + WINDOW, KV_DIM), lambda b, i: (b, 0, 0)),
            _full((2, N_HEADS, WINDOW, 2 * WINDOW)),
            _full((D_MODEL, D_MODEL)),
            _full((D_MODEL, D_MODEL)),
            _full((1, D_MODEL)),
            _full((1, D_MODEL)),
        ],
        out_specs=pl.BlockSpec((tq, D_MODEL), lambda b, i: (b * nq + i, 0)),
        out_shape=jax.ShapeDtypeStruct((bsz * seq, D_MODEL), F32),
        compiler_params=pltpu.CompilerParams(
            dimension_semantics=("arbitrary", "arbitrary"), vmem_limit_bytes=VMEM_LIMIT),
        name="swa_prompt",
    )(sinks, x, kpad, vpad, bias, w_q, w_o, g1, b1)


def _swa_sample_kernel(x_ref, k_ref, v_ref, bias_ref, sink_ref, mask_ref, wq_ref, wo_ref,
                       g1_ref, b1_ref, x1_ref, *, sb, t):
    x = x_ref[...]
    q = _dot(x.astype(BF16), wq_ref[...]) * (HEAD_DIM ** -0.5)
    mask = mask_ref[...]
    bias = bias_ref[...]
    sink = sink_ref[...]
    outs = []
    for s in range(sb):
        qs = q[s * t:(s + 1) * t, :]
        parts = []
        for g in range(Q_PER_KV):
            qg = qs[:, g * KV_DIM:(g + 1) * KV_DIM]
            parts.append(jnp.concatenate([qg] * N_KV_HEADS, axis=0) * mask)
        qexp = jnp.concatenate(parts, axis=0).astype(BF16)
        l = _dot_nt(qexp, k_ref[s]) + bias
        m = jnp.maximum(jnp.max(l, axis=1, keepdims=True), sink)
        p = jnp.exp(l - m)
        den = jnp.sum(p, axis=1, keepdims=True) + jnp.exp(sink - m)
        r = _dot(p.astype(BF16), v_ref[s]) / den
        og = []
        for g in range(Q_PER_KV):
            rg = r[g * N_KV_HEADS * t:(g + 1) * N_KV_HEADS * t, :] * mask
            acc = rg[0:t]
            for kh in range(1, N_KV_HEADS):
                acc = acc + rg[kh * t:(kh + 1) * t]
            og.append(acc)
        outs.append(jnp.concatenate(og, axis=1))
    attn = jnp.concatenate(outs, axis=0)
    m_out = _dot(attn.astype(BF16), wo_ref[...])
    x1_ref[...] = _ln(DEEPNORM_ALPHA * x + m_out, g1_ref[...], b1_ref[...])


def _swa_sample_layer(x, row0, n_seq, t, kbuf, vbuf, bias, sink_col, mask, w_q, w_o, g1, b1,
                      sb=SAMPLE_SEQ_BLOCK):
    rows = sb * t
    hr = N_HEADS * t
    blk0 = row0 // rows
    return pl.pallas_call(
        functools.partial(_swa_sample_kernel, sb=sb, t=t),
        grid=(n_seq // sb,),
        in_specs=[
            pl.BlockSpec((rows, D_MODEL), lambda i: (blk0 + i, 0)),
            pl.BlockSpec((sb, KEY_PAD, KV_DIM), lambda i: (i, 0, 0)),
            pl.BlockSpec((sb, KEY_PAD, KV_DIM), lambda i: (i, 0, 0)),
            _full((hr, KEY_PAD)),
            _full((hr, 1)),
            _full((N_KV_HEADS * t, KV_DIM)),
            _full((D_MODEL, D_MODEL)),
            _full((D_MODEL, D_MODEL)),
            _full((1, D_MODEL)),
            _full((1, D_MODEL)),
        ],
        out_specs=pl.BlockSpec((rows, D_MODEL), lambda i: (i, 0)),
        out_shape=jax.ShapeDtypeStruct((n_seq * t, D_MODEL), F32),
        compiler_params=pltpu.CompilerParams(
            dimension_semantics=("arbitrary",), vmem_limit_bytes=VMEM_LIMIT),
        name="swa_sample",
    )(x, kbuf, vbuf, bias, sink_col, mask, w_q, w_o, g1, b1)


def _router_kernel(x_ref, wt_ref, b_ref, tri_ref, idx_ref, gate_ref, rank_ref, cnt_ref, carry,
                   *, tm):
    i = pl.program_id(0)

    @pl.when(i == 0)
    def _():
        carry[...] = jnp.zeros_like(carry)

    logits = lax.dot_general(wt_ref[...], x_ref[...], (((1,), (1,)), ((), ())),
                             precision=lax.Precision.HIGHEST,
                             preferred_element_type=F32) + b_ref[...]
    eidx = lax.broadcasted_iota(jnp.int32, (N_EXPERTS, tm), 0)
    l = logits
    vals, idxs, sels = [], [], []
    for _ in range(TOP_K):
        m = jnp.max(l, axis=0, keepdims=True)
        idx = jnp.min(jnp.where(l == m, eidx, N_EXPERTS), axis=0, keepdims=True)
        sel = eidx == idx
        vals.append(m)
        idxs.append(idx)
        sels.append(sel)
        l = jnp.where(sel, -jnp.inf, l)
    exps = [jnp.exp(v - vals[0]) for v in vals]
    den = exps[0] + exps[1] + exps[2] + exps[3]
    onehot = jnp.zeros((N_EXPERTS, tm), F32)
    for sel in sels:
        onehot = onehot + sel.astype(F32)
    incl = _dot(onehot.astype(BF16), tri_ref[...])
    excl = incl - onehot + carry[...]
    for k in range(TOP_K):
        idx_ref[k:k + 1, :] = idxs[k]
        gate_ref[k:k + 1, :] = exps[k] / den
        rank = jnp.sum(jnp.where(sels[k], excl, 0.0), axis=0, keepdims=True)
        rank_ref[k:k + 1, :] = rank.astype(jnp.int32)
    carry[...] = carry[...] + jnp.sum(onehot, axis=1, keepdims=True)
    cnt_ref[...] = jnp.broadcast_to(carry[...], cnt_ref.shape)


def _route(x1, router_wt, router_b, tri, tm=TM_ROUTER):
    n = x1.shape[0]
    spec_kn = pl.BlockSpec((TOP_K, tm), lambda i: (0, i))
    idx, gate, rank, cnt = pl.pallas_call(
        functools.partial(_router_kernel, tm=tm),
        grid=(n // tm,),
        in_specs=[
            pl.BlockSpec((tm, D_MODEL), lambda i: (i, 0)),
            _full((N_EXPERTS, D_MODEL)),
            _full((N_EXPERTS, 1)),
            _full((tm, tm)),
        ],
        out_specs=[spec_kn, spec_kn, spec_kn, _full((N_EXPERTS, 128))],
        out_shape=[
            jax.ShapeDtypeStruct((TOP_K, n), jnp.int32),
            jax.ShapeDtypeStruct((TOP_K, n), F32),
            jax.ShapeDtypeStruct((TOP_K, n), jnp.int32),
            jax.ShapeDtypeStruct((N_EXPERTS, 128), F32),
        ],
        scratch_shapes=[pltpu.VMEM((N_EXPERTS, 1), F32)],
        compiler_params=pltpu.CompilerParams(
            dimension_semantics=("arbitrary",), vmem_limit_bytes=VMEM_LIMIT),
        name="moe_router",
    )(x1, router_wt, router_b, tri)
    return idx, gate, rank, cnt[:, 0].astype(jnp.int32)


def _invert_kernel(pos_ref, init_hbm, inv_hbm, inv_smem, sem, *, tm, chunk):
    i = pl.program_id(0)

    @pl.when(i == 0)
    def _():
        cp = pltpu.make_async_copy(init_hbm, inv_smem, sem)
        cp.start()
        cp.wait()

    base = i * tm

    def body(c, carry):
        t0 = c * chunk
        for t in range(chunk):
            for k in range(TOP_K):
                inv_smem[pos_ref[0, 0, k * tm + t0 + t]] = (base + t0 + t) * TOP_K + k
        return carry

    lax.fori_loop(0, tm // chunk, body, 0)

    @pl.when(i == pl.num_programs(0) - 1)
    def _():
        cp = pltpu.make_async_copy(inv_smem, inv_hbm, sem)
        cp.start()
        cp.wait()


def _invert(pos3, init, tm=TM_DISPATCH, chunk=INVERT_CHUNK):
    n_tiles = pos3.shape[0]
    return pl.pallas_call(
        functools.partial(_invert_kernel, tm=tm, chunk=chunk),
        grid=(n_tiles,),
        in_specs=[
            pl.BlockSpec((1, 1, TOP_K * tm), lambda i: (i, 0, 0), memory_space=pltpu.SMEM),
            pl.BlockSpec(memory_space=pl.ANY),
        ],
        out_specs=pl.BlockSpec(memory_space=pl.ANY),
        out_shape=jax.ShapeDtypeStruct(init.shape, jnp.int32),
        scratch_shapes=[pltpu.SMEM(init.shape, jnp.int32), pltpu.SemaphoreType.DMA],
        compiler_params=pltpu.CompilerParams(
            dimension_semantics=("arbitrary",), vmem_limit_bytes=VMEM_LIMIT),
        name="moe_invert",
    )(pos3, init)


def _expert_kernel(te_ref, meta_ref, dstp_ref, srcc_ref, dstc_ref, srcn_ref, x1_hbm, wup_ref,
                   bup_ref, wdn_ref, bdn_ref, ytok_hbm, xbuf_a, xbuf_b, ybuf_a, ybuf_b, zbuf, wup_bf,
                   wdn_bf, sem, *, n_tokens):
    s = pl.program_id(0)
    n_valid = meta_ref[0]
    e = te_ref[s]
    prev = te_ref[jnp.maximum(s - 1, 0)]
    g_a, g_b, s_a, s_b, s_z = (sem.at[j] for j in range(5))
    hm = TM_EXPERT // 2

    def gather_copy(src_ref, half, xbuf, sm, r):
        src = src_ref[0, 0, half * hm + r]
        return pltpu.make_async_copy(x1_hbm.at[pl.ds(src, 1), :], xbuf.at[pl.ds(r, 1), :], sm)

    def scatter_copy(dst_ref, half, ybuf, sm, r):
        dst = dst_ref[0, 0, half * hm + r]
        return pltpu.make_async_copy(ybuf.at[pl.ds(r, 1), :], ytok_hbm.at[pl.ds(dst, 1), :], sm)

    def issue(copy_fn, map_ref, half, buf, sm):
        for r in range(hm):
            copy_fn(map_ref, half, buf, sm, r).start(priority=r % 2)

    def wait_gather(xbuf, sm):
        pltpu.make_async_copy(x1_hbm.at[pl.ds(0, hm), :], xbuf, sm).wait()

    def wait_scatter(ybuf, sm):
        pltpu.make_async_copy(ybuf, ytok_hbm.at[pl.ds(0, hm), :], sm).wait()

    def compute(xbuf, ybuf):
        x = xbuf[...].astype(BF16)
        h = _dot(x, wup_bf[...]) + bup_ref[...]
        glu = jnp.minimum(h[:, :D_EXPERT], SWIGLU_LIMIT)
        lin = jnp.clip(h[:, D_EXPERT:], -SWIGLU_LIMIT, SWIGLU_LIMIT)
        act = glu * jax.nn.sigmoid(SWIGLU_ALPHA * glu) * (lin + 1.0)
        ybuf[...] = _dot(act.astype(BF16), wdn_bf[...]) + bdn_ref[...]

    @pl.when(s == 0)
    def _():
        ybuf_b[...] = jnp.zeros_like(ybuf_b)
        zbuf[...] = jnp.zeros_like(zbuf)

        def first(r, c):
            gather_copy(srcc_ref, 0, xbuf_a, g_a, r).start()
            return c

        lax.fori_loop(0, hm, first, 0)

    @pl.when(((s == 0) | (e != prev)) & (s < n_valid))
    def _():
        rb = 128

        def cast_up(r, c):
            r0 = pl.multiple_of(r * rb, rb)
            wup_bf[pl.ds(r0, rb), :] = wup_ref[pl.ds(r0, rb), :].astype(BF16)
            return c

        def cast_dn(r, c):
            r0 = pl.multiple_of(r * rb, rb)
            wdn_bf[pl.ds(r0, rb), :] = wdn_ref[pl.ds(r0, rb), :].astype(BF16)
            return c

        lax.fori_loop(0, D_MODEL // rb, cast_up, 0)
        lax.fori_loop(0, D_EXPERT // rb, cast_dn, 0)

    @pl.when(s < n_valid)
    def _():
        wait_gather(xbuf_a, g_a)

        @pl.when(s > 0)
        def _():
            wait_scatter(ybuf_a, s_a)

        issue(scatter_copy, dstp_ref, 1, ybuf_b, s_b)
        issue(gather_copy, srcc_ref, 1, xbuf_b, g_b)
        compute(xbuf_a, ybuf_a)
        wait_gather(xbuf_b, g_b)
        wait_scatter(ybuf_b, s_b)
        issue(scatter_copy, dstc_ref, 0, ybuf_a, s_a)
        issue(gather_copy, srcn_ref, 0, xbuf_a, g_a)
        compute(xbuf_b, ybuf_b)

        @pl.when(s == n_valid - 1)
        def _():
            def last(r, c):
                scatter_copy(dstc_ref, 1, ybuf_b, s_b, r).start()
                return c

            lax.fori_loop(0, hm, last, 0)
            wait_scatter(ybuf_a, s_a)
            wait_scatter(ybuf_b, s_b)
            wait_gather(xbuf_a, g_a)

    @pl.when(s >= n_valid)
    def _():
        base = pl.multiple_of(n_tokens * TOP_K + (s - n_valid) * TM_EXPERT, TM_EXPERT)
        cp = pltpu.make_async_copy(zbuf, ytok_hbm.at[pl.ds(base, TM_EXPERT), :], s_z)
        cp.start()
        cp.wait()


def _experts(x1, src3, dst3, tile_expert, meta, w_up, b_up, w_dn, b_dn, layer, n_rows):
    n_tokens = x1.shape[0]
    steps = src3.shape[0] - 1
    hm = TM_EXPERT // 2
    map_spec = lambda f: pl.BlockSpec((1, 1, TM_EXPERT), lambda i, te, mt: (f(i), 0, 0),
                                      memory_space=pltpu.SMEM)
    wspec = lambda shape: pl.BlockSpec((None, None) + shape, lambda i, te, mt: (layer, te[i], 0, 0))
    grid_spec = pltpu.PrefetchScalarGridSpec(
        num_scalar_prefetch=2,
        grid=(steps,),
        in_specs=[
            map_spec(lambda i: i),
            map_spec(lambda i: i + 1),
            map_spec(lambda i: i + 1),
            map_spec(lambda i: jnp.minimum(i + 2, steps)),
            pl.BlockSpec(memory_space=pl.ANY),
            wspec((D_MODEL, 2 * D_EXPERT)),
            wspec((1, 2 * D_EXPERT)),
            wspec((D_EXPERT, D_MODEL)),
            wspec((1, D_MODEL)),
        ],
        out_specs=pl.BlockSpec(memory_space=pl.ANY),
        scratch_shapes=[
            pltpu.VMEM((hm, D_MODEL), F32), pltpu.VMEM((hm, D_MODEL), F32),
            pltpu.VMEM((hm, D_MODEL), F32), pltpu.VMEM((hm, D_MODEL), F32),
            pltpu.VMEM((TM_EXPERT, D_MODEL), F32),
            pltpu.VMEM((D_MODEL, 2 * D_EXPERT), BF16),
            pltpu.VMEM((D_EXPERT, D_MODEL), BF16),
            pltpu.SemaphoreType.DMA((5,)),
        ],
    )
    return pl.pallas_call(
        functools.partial(_expert_kernel, n_tokens=n_tokens),
        grid_spec=grid_spec,
        out_shape=jax.ShapeDtypeStruct((n_rows, D_MODEL), F32),
        compiler_params=pltpu.CompilerParams(
            dimension_semantics=("arbitrary",), vmem_limit_bytes=VMEM_LIMIT),
        name="moe_experts",
    )(tile_expert, meta, dst3, src3, dst3, src3, x1, w_up, b_up, w_dn, b_dn)


def _combine_kernel(x1_ref, y0_ref, y1_ref, y2_ref, y3_ref, gate_ref, pp_ref, ps_ref, wg_ref, wp_ref,
                    g2_ref, b2_ref, *rest, prompt_steps, emit_kv):
    if emit_kv:
        kvg_ref, kvb_ref, wkv_ref, x3_ref, kv_ref = rest
    else:
        (x3_ref,) = rest
    is_sample = pl.program_id(0) >= prompt_steps
    gate = gate_ref[...]
    f = gate[:, 0:1] * y0_ref[...]
    for k, y_ref in enumerate((y1_ref, y2_ref, y3_ref), start=1):
        f = f + gate[:, k:k + 1] * y_ref[...]
    x2 = _ln(DEEPNORM_ALPHA * x1_ref[...] + f, g2_ref[...], b2_ref[...])
    p = jnp.where(is_sample, ps_ref[...], pp_ref[...])
    pg = jax.nn.sigmoid(_dot(x2.astype(BF16), wg_ref[...]))
    pp = _dot(p.astype(BF16), wp_ref[...])
    x3 = x2 + pg * pp
    x3_ref[...] = x3
    if emit_kv:
        kv_ref[...] = _dot(_ln(x3, kvg_ref[...], kvb_ref[...]).astype(BF16), wkv_ref[...])


def _combine(x1, ytok, gate_t, p_prompt, p_sample, layer, n_prompt, w_gate, w_proj, g2, b2,
             kv_params=None, tm=TM_TOKEN):
    n = x1.shape[0]
    emit_kv = kv_params is not None
    steps = n // tm
    prompt_steps = n_prompt // tm
    sample_steps = steps - prompt_steps
    row = lambda w: pl.BlockSpec((tm, w), lambda i: (i, 0))
    in_specs = [
        row(D_MODEL),
    ] + [
        pl.BlockSpec((tm, D_MODEL), functools.partial(lambda k, i: (k * steps + i, 0), k))
        for k in range(TOP_K)
    ] + [
        row(TOP_K),
        pl.BlockSpec((tm, PLE_DIM),
                     lambda i: (layer * prompt_steps + jnp.minimum(i, prompt_steps - 1), 0)),
        pl.BlockSpec((tm, PLE_DIM),
                     lambda i: (layer * sample_steps + jnp.maximum(i - prompt_steps, 0), 0)),
        _full((D_MODEL, D_MODEL)),
        _full((PLE_DIM, D_MODEL)),
        _full((1, D_MODEL)),
        _full((1, D_MODEL)),
    ]
    args = [x1, ytok, ytok, ytok, ytok, gate_t, p_prompt, p_sample, w_gate, w_proj, g2, b2]
    out_specs = [row(D_MODEL)]
    out_shape = [jax.ShapeDtypeStruct((n, D_MODEL), F32)]
    if emit_kv:
        in_specs += [_full((1, D_MODEL)), _full((1, D_MODEL)), _full((D_MODEL, 2 * KV_DIM))]
        args += list(kv_params)
        out_specs.append(row(2 * KV_DIM))
        out_shape.append(jax.ShapeDtypeStruct((n, 2 * KV_DIM), F32))
    outs = pl.pallas_call(
        functools.partial(_combine_kernel, prompt_steps=prompt_steps, emit_kv=emit_kv),
        grid=(steps,),
        in_specs=in_specs,
        out_specs=out_specs,
        out_shape=out_shape,
        compiler_params=pltpu.CompilerParams(
            dimension_semantics=("arbitrary",), vmem_limit_bytes=VMEM_LIMIT),
        name="moe_combine",
    )(*args)
    return outs if emit_kv else (outs[0], None)


def _tile_pos(pos, tm):
    n = pos.shape[1]
    return pos.reshape(TOP_K, n // tm, tm).transpose(1, 0, 2).reshape(n // tm, 1, TOP_K * tm)


def _moe_layer(x1, p_prompt, p_sample, n_prompt, layer, router_wt, router_b, tri,
               w_up, b_up, w_dn, b_dn, w_gate, w_proj, g2, b2, kv_params=None):
    n = x1.shape[0]
    idx, gate, rank, counts = _route(x1, router_wt, router_b, tri)
    padded = ((counts + TM_EXPERT - 1) // TM_EXPERT) * TM_EXPERT
    ends = jnp.cumsum(padded)
    starts = ends - padded
    onehot = idx[:, :, None] == jnp.arange(N_EXPERTS, dtype=jnp.int32)
    pos = jnp.sum(jnp.where(onehot, starts, 0), axis=-1) + rank
    steps = n * TOP_K // TM_EXPERT + N_EXPERTS
    n_rows = steps * TM_EXPERT
    n_valid = (ends[-1] // TM_EXPERT).astype(jnp.int32)
    tile_ids = jnp.arange(steps, dtype=jnp.int32)
    tile_start = jnp.minimum(tile_ids, n_valid - 1) * TM_EXPERT
    tile_expert = jnp.sum(tile_start[:, None] >= ends[None, :], axis=1).astype(jnp.int32)
    te_onehot = tile_expert[:, None] == jnp.arange(N_EXPERTS, dtype=jnp.int32)
    tile_real_end = jnp.sum(jnp.where(te_onehot, starts + counts, 0), axis=1)
    row_ids = jnp.arange(n_rows, dtype=jnp.int32).reshape(steps, TM_EXPERT)
    is_pad = (row_ids >= tile_real_end[:, None]) & (tile_ids < n_valid)[:, None]
    pad_rank = (jnp.cumsum(is_pad.reshape(-1).astype(jnp.int32)) - 1).reshape(steps, TM_EXPERT)
    pad_base = n * TOP_K + (steps - n_valid) * TM_EXPERT
    init = jnp.where(is_pad, pad_base + pad_rank, 0).reshape(-1)
    inv = _invert(_tile_pos(pos, TM_DISPATCH), init)
    hm = TM_EXPERT // 2
    dummy = inv >= n * TOP_K
    src = jnp.where(dummy, 0, inv >> 2)
    dst = jnp.where(dummy, inv, (inv & (TOP_K - 1)) * n + (inv >> 2))
    first_dst = jnp.concatenate([jnp.zeros((hm,), jnp.int32), n_rows + jnp.arange(hm, dtype=jnp.int32)])
    src3 = jnp.concatenate([jnp.zeros((TM_EXPERT,), jnp.int32), src]).reshape(steps + 1, 1, TM_EXPERT)
    dst3 = jnp.concatenate([first_dst, dst]).reshape(steps + 1, 1, TM_EXPERT)
    meta = n_valid.reshape(1)
    ytok = _experts(x1, src3, dst3, tile_expert, meta, w_up, b_up, w_dn, b_dn, layer, n_rows + hm)
    return _combine(x1, ytok, gate.T, p_prompt, p_sample, layer, n_prompt, w_gate, w_proj, g2, b2,
                    kv_params)


def _alibi_slopes():
    h = jnp.arange(1, N_HEADS + 1, dtype=F32)
    return 2.0 ** (-8.0 * h / N_HEADS)


def _prompt_bias():
    qi = jnp.arange(WINDOW)[:, None]
    kj = jnp.arange(2 * WINDOW)[None, :]
    dist = qi + WINDOW - kj
    valid = (dist >= 0) & (dist < WINDOW)
    slopes = _alibi_slopes()[:, None, None]
    tables = []
    for first in (True, False):
        ok = valid & (kj >= WINDOW) if first else valid
        tables.append(jnp.where(ok[None], -slopes * dist.astype(F32)[None], NEG_INF))
    return jnp.stack(tables).astype(F32)


def _sample_tables(t, sinks):
    g = jnp.arange(Q_PER_KV)[:, None, None]
    kh = jnp.arange(N_KV_HEADS)[None, :, None]
    tok = jnp.arange(t)[None, None, :]
    head = jnp.broadcast_to(kh * Q_PER_KV + g, (Q_PER_KV, N_KV_HEADS, t)).reshape(-1)
    tok = jnp.broadcast_to(tok, (Q_PER_KV, N_KV_HEADS, t)).reshape(-1)
    kj = jnp.arange(KEY_PAD)[None, :]
    dist = tok[:, None] + WINDOW - kj
    valid = (dist >= 0) & (dist < WINDOW) & (kj < WINDOW + t)
    slopes = _alibi_slopes()[head][:, None]
    bias = jnp.where(valid, -slopes * dist.astype(F32), NEG_INF).astype(F32)
    sink_col = sinks[head][:, None].astype(F32)
    row_kh = jnp.repeat(jnp.arange(N_KV_HEADS), t)[:, None]
    lane_kh = (jnp.arange(KV_DIM) // HEAD_DIM)[None, :]
    mask = (row_kh == lane_kh).astype(F32)
    return bias, sink_col, mask


def _gkd_cols(w):
    r = w.shape[0]
    w4 = w.reshape(r, N_KV_HEADS, Q_PER_KV, HEAD_DIM)
    return w4.transpose(0, 2, 1, 3).reshape(r, N_HEADS * HEAD_DIM)


def _spatial_tables(w_s, b_s, t):
    tril = jnp.tril(jnp.ones((CHUNK, CHUNK), dtype=bool))
    ws_p = jnp.where(tril[None], w_s, 0.0)
    bs_p = jnp.repeat(b_s.T, GROUP_DIM_A, axis=1)
    r = jnp.arange(CHUNK)
    same = (r[:, None] // t) == (r[None, :] // t)
    small = jnp.where(tril[None, :t, :t], w_s[:, :t, :t], 0.0)
    ws_s = jnp.where(same[None], jnp.tile(small, (1, CHUNK // t, CHUNK // t)), 0.0)
    bs_s = jnp.repeat(jnp.tile(b_s[:, :t], (1, CHUNK // t)).T, GROUP_DIM_A, axis=1)
    return jnp.stack([ws_p, ws_s]).astype(BF16), jnp.stack([bs_p, bs_s]).astype(F32)


def kernel(x_prompt, x_sample, cache_k, cache_v, p_prompt, p_sample, ln1_g, ln1_b, ln2_g, ln2_b,
           a_w_in, a_b_in, a_ln_g, a_ln_b, a_w_s, a_b_s, a_w_out, kv_ln_g, kv_ln_b, w_kv,
           b_w_q, b_sinks, b_w_o, router_w, router_b, exp_w_up, exp_b_up, exp_w_dn, exp_b_dn,
           ple_w_proj, ple_w_gate):
    bsz, seq, _ = x_prompt.shape
    n_seq, t, _ = x_sample.shape
    n_p = bsz * seq
    n_s = n_seq * t
    n = n_p + n_s
    depth = ln1_g.shape[0]
    n_a = a_w_in.shape[0]

    xp2 = x_prompt.reshape(n_p, D_MODEL)
    xs2 = x_sample.reshape(n_s, D_MODEL)
    pp2 = p_prompt.reshape(depth * n_p, PLE_DIM)
    ps2 = p_sample.reshape(depth * n_s, PLE_DIM)
    row = lambda v: v.reshape(1, -1).astype(F32)

    tri = (jnp.arange(TM_ROUTER)[:, None] <= jnp.arange(TM_ROUTER)[None, :]).astype(BF16)
    b_up4 = exp_b_up.reshape(depth, N_EXPERTS, 1, 2 * D_EXPERT)
    b_dn4 = exp_b_dn.reshape(depth, N_EXPERTS, 1, D_MODEL)
    prompt_bias = _prompt_bias()

    chunk_v = []
    kv_params = (row(kv_ln_g), row(kv_ln_b), w_kv.astype(BF16))
    x = None
    kp = vp = k_buf = v_buf = None
    kpad = vpad = kbuf_pad = vbuf_pad = None
    for i in range(depth):
        if i < n_a:
            ws2, bs2 = _spatial_tables(a_w_s[i], a_b_s[i], t)
            xa, xb, xb_row0 = (xp2, xs2, 0) if i == 0 else (x, x, n_p)
            x1, v_rows = _gmlp_layer(xa, xb, xb_row0, n_p, n_s, a_w_in[i].astype(BF16),
                                     row(a_b_in[i]), row(a_ln_g[i]), row(a_ln_b[i]), ws2, bs2,
                                     a_w_out[i].astype(BF16), row(ln1_g[i]), row(ln1_b[i]))
            chunk_v.append(v_rows.reshape(n_seq, t, D_MODEL))
        else:
            j = i - n_a
            wq = b_w_q[j].astype(BF16)
            wo = b_w_o[j].astype(BF16)
            x1_p = _swa_prompt_layer(x, bsz, seq, kpad, vpad, prompt_bias,
                                     b_sinks[j].astype(F32), wq, wo, row(ln1_g[i]), row(ln1_b[i]))
            s_bias, s_sink, s_mask = _sample_tables(t, b_sinks[j])
            x1_s = _swa_sample_layer(x, n_p, n_seq, t, kbuf_pad, vbuf_pad, s_bias, s_sink, s_mask,
                                     _gkd_cols(wq), _gkd_cols(wo.T).T, row(ln1_g[i]), row(ln1_b[i]))
            x1 = jnp.concatenate([x1_p, x1_s], axis=0)
        x, kv = _moe_layer(
            x1, pp2, ps2, n_p, i, router_w[i].T, router_b[i].reshape(N_EXPERTS, 1), tri,
            exp_w_up, b_up4, exp_w_dn, b_dn4, ple_w_gate[i].astype(BF16),
            ple_w_proj[i].astype(BF16), row(ln2_g[i]), row(ln2_b[i]),
            kv_params if i == n_a - 1 else None)
        if i == n_a - 1:
            k_all, v_all = kv[:, :KV_DIM], kv[:, KV_DIM:]
            kp = k_all[:n_p].reshape(bsz, seq, KV_DIM)
            vp = v_all[:n_p].reshape(bsz, seq, KV_DIM)
            k_buf = jnp.concatenate([cache_k.reshape(n_seq, WINDOW, KV_DIM),
                                     k_all[n_p:].reshape(n_seq, t, KV_DIM)], axis=1)
            v_buf = jnp.concatenate([cache_v.reshape(n_seq, WINDOW, KV_DIM),
                                     v_all[n_p:].reshape(n_seq, t, KV_DIM)], axis=1)
            front = ((0, 0), (WINDOW, 0), (0, 0))
            kpad = jnp.pad(kp, front).astype(BF16)
            vpad = jnp.pad(vp, front).astype(BF16)
            tail = ((0, 0), (0, KEY_PAD - WINDOW - t), (0, 0))
            kbuf_pad = jnp.pad(k_buf, tail).astype(BF16)
            vbuf_pad = jnp.pad(v_buf, tail).astype(BF16)

    heads = (N_KV_HEADS, HEAD_DIM)
    return (x[:n_p].reshape(bsz, seq, D_MODEL),
            x[n_p:].reshape(n_seq, t, D_MODEL),
            jnp.stack(chunk_v),
            kp[:, -WINDOW:].reshape(bsz, WINDOW, *heads),
            vp[:, -WINDOW:].reshape(bsz, WINDOW, *heads),
            k_buf[:, -WINDOW:].reshape(n_seq, WINDOW, *heads),
            v_buf[:, -WINDOW:].reshape(n_seq, WINDOW, *heads))
```

```python
import functools

import jax
import jax.numpy as jnp
from jax import lax
from jax.experimental import pallas as pl
from jax.experimental.pallas import tpu as pltpu

F32 = jnp.float32
BF16 = jnp.bfloat16

D_MODEL = 1024
DEPTH = 4
N_A_LAYERS = 2
CHUNK = 128
N_GROUPS_A = 8
GROUP_DIM_A = D_MODEL // N_GROUPS_A
HEAD_DIM = 64
N_HEADS = 16
N_KV_HEADS = 4
Q_PER_KV = 4
KV_DIM = N_KV_HEADS * HEAD_DIM
WINDOW = 128
N_EXPERTS = 32
TOP_K = 4
D_EXPERT = 1024
SWIGLU_LIMIT = 7.0
SWIGLU_ALPHA = 1.702
PLE_DIM = 256
DEEPNORM_ALPHA = (2 * DEPTH) ** 0.25
LN_EPS = 1e-5
NEG_INF = -1e30

TM_TOKEN = 256
TM_DISPATCH = 512
DISPATCH_CHUNK = 128
TM_ROUTER = 512
TM_EXPERT = 512
SAMPLE_SEQ_BLOCK = 8
KEY_PAD = 256
VMEM_LIMIT = 56 * 1024 * 1024


def _ln(x, g, b):
    mu = jnp.mean(x, axis=-1, keepdims=True)
    xc = x - mu
    var = jnp.mean(xc * xc, axis=-1, keepdims=True)
    return xc * lax.rsqrt(var + LN_EPS) * g + b


def _dot(a, b):
    return jnp.dot(a, b, preferred_element_type=F32)


def _dot_nt(a, b):
    return lax.dot_general(a, b, (((1,), (1,)), ((), ())), preferred_element_type=F32)


def _full(shape):
    n = len(shape)
    return pl.BlockSpec(shape, lambda *_: (0,) * n)


def _gmlp_kernel(xa_ref, xb_ref, win_ref, bin_ref, lng_ref, lnb_ref, ws_ref, bs_ref, wout_ref,
                 g1_ref, b1_ref, x1_ref, v_ref, *, tm, p_tiles):
    x = jnp.where(pl.program_id(0) >= p_tiles, xb_ref[...], xa_ref[...])
    h = _dot(x.astype(BF16), win_ref[...]) + bin_ref[...]
    h = 0.5 * h * (1.0 + lax.erf(h * (2.0 ** -0.5)))
    u = h[:, :D_MODEL]
    v = _ln(h[:, D_MODEL:], lng_ref[...], lnb_ref[...])
    v_ref[...] = v
    vb = v.astype(BF16)
    rows = []
    for c in range(tm // CHUNK):
        cols = []
        for g in range(N_GROUPS_A):
            blk = vb[c * CHUNK:(c + 1) * CHUNK, g * GROUP_DIM_A:(g + 1) * GROUP_DIM_A]
            cols.append(_dot(ws_ref[g], blk))
        rows.append(jnp.concatenate(cols, axis=1) + bs_ref[...])
    s = jnp.concatenate(rows, axis=0)
    gated = (u * s).astype(BF16)
    m = _dot(gated, wout_ref[...])
    x1_ref[...] = _ln(DEEPNORM_ALPHA * x + m, g1_ref[...], b1_ref[...])


def _gmlp_layer(xa, xb, xb_row0, n_prompt, n_sample, w_in, b_in, ln_g, ln_b, ws2, bs2, w_out,
                g1, b1, tm=TM_TOKEN):
    n = n_prompt + n_sample
    p_tiles = n_prompt // tm
    n_tiles = n // tm
    s_tiles = n_tiles - p_tiles
    b_tile0 = xb_row0 // tm
    kind = lambda i: jnp.where(i >= p_tiles, 1, 0)
    x1, v = pl.pallas_call(
        functools.partial(_gmlp_kernel, tm=tm, p_tiles=p_tiles),
        grid=(n_tiles,),
        in_specs=[
            pl.BlockSpec((tm, D_MODEL), lambda i: (jnp.minimum(i, p_tiles - 1), 0)),
            pl.BlockSpec((tm, D_MODEL), lambda i: (jnp.maximum(i - p_tiles, 0) + b_tile0, 0)),
            _full((D_MODEL, 2 * D_MODEL)),
            _full((1, 2 * D_MODEL)),
            _full((1, D_MODEL)),
            _full((1, D_MODEL)),
            pl.BlockSpec((None, N_GROUPS_A, CHUNK, CHUNK), lambda i: (kind(i), 0, 0, 0)),
            pl.BlockSpec((None, CHUNK, D_MODEL), lambda i: (kind(i), 0, 0)),
            _full((D_MODEL, D_MODEL)),
            _full((1, D_MODEL)),
            _full((1, D_MODEL)),
        ],
        out_specs=[
            pl.BlockSpec((tm, D_MODEL), lambda i: (i, 0)),
            pl.BlockSpec((tm, D_MODEL), lambda i: (jnp.maximum(i - (p_tiles - 1), 0), 0)),
        ],
        out_shape=[
            jax.ShapeDtypeStruct((n, D_MODEL), F32),
            jax.ShapeDtypeStruct(((s_tiles + 1) * tm, D_MODEL), F32),
        ],
        compiler_params=pltpu.CompilerParams(
            dimension_semantics=("arbitrary",), vmem_limit_bytes=VMEM_LIMIT),
        name="gmlp_mixer",
    )(xa, xb, w_in, b_in, ln_g, ln_b, ws2, bs2, w_out, g1, b1)
    return x1, v[tm:]


def _swa_prompt_kernel(sink_ref, x_ref, k_ref, v_ref, bias_ref, wq_ref, wo_ref, g1_ref, b1_ref,
                       x1_ref, *, tq):
    i = pl.program_id(1)
    x = x_ref[...]
    qb = (_dot(x.astype(BF16), wq_ref[...]) * (HEAD_DIM ** -0.5)).astype(BF16)
    blocks = []
    for j in range(tq // WINDOW):
        blk = i * (tq // WINDOW) + j
        kstart = pl.multiple_of(blk * WINDOW, WINDOW)
        kb = k_ref[pl.ds(kstart, 2 * WINDOW), :]
        vb = v_ref[pl.ds(kstart, 2 * WINDOW), :]
        sel = jnp.where(blk == 0, 0, 1)
        outs = []
        for h in range(N_HEADS):
            kh = h // Q_PER_KV
            qh = qb[j * WINDOW:(j + 1) * WINDOW, h * HEAD_DIM:(h + 1) * HEAD_DIM]
            s = _dot_nt(qh, kb[:, kh * HEAD_DIM:(kh + 1) * HEAD_DIM])
            l = s + bias_ref[sel, h]
            sink = sink_ref[h]
            m = jnp.maximum(jnp.max(l, axis=1, keepdims=True), sink)
            p = jnp.exp(l - m)
            den = jnp.sum(p, axis=1, keepdims=True) + jnp.exp(sink - m)
            o = _dot(p.astype(BF16), vb[:, kh * HEAD_DIM:(kh + 1) * HEAD_DIM])
            outs.append(o / den)
        blocks.append(jnp.concatenate(outs, axis=1))
    attn = jnp.concatenate(blocks, axis=0)
    m_out = _dot(attn.astype(BF16), wo_ref[...])
    x1_ref[...] = _ln(DEEPNORM_ALPHA * x + m_out, g1_ref[...], b1_ref[...])


def _swa_prompt_layer(x, bsz, seq, kpad, vpad, bias, sinks, w_q, w_o, g1, b1, tq=TM_TOKEN):
    nq = seq // tq
    return pl.pallas_call(
        functools.partial(_swa_prompt_kernel, tq=tq),
        grid=(bsz, nq),
        in_specs=[
            pl.BlockSpec(memory_space=pltpu.SMEM),
            pl.BlockSpec((tq, D_MODEL), lambda b, i: (b * nq + i, 0)),
            pl.BlockSpec((None, seq + WINDOW, KV_DIM), lambda b, i: (b, 0, 0)),
            pl.BlockSpec((None, seq + WINDOW, KV_DIM), lambda b, i: (b, 0, 0)),
            _full((2, N_HEADS, WINDOW, 2 * WINDOW)),
            _full((D_MODEL, D_MODEL)),
            _full((D_MODEL, D_MODEL)),
            _full((1, D_MODEL)),
            _full((1, D_MODEL)),
        ],
        out_specs=pl.BlockSpec((tq, D_MODEL), lambda b, i: (b * nq + i, 0)),
        out_shape=jax.ShapeDtypeStruct((bsz * seq, D_MODEL), F32),
        compiler_params=pltpu.CompilerParams(
            dimension_semantics=("arbitrary", "arbitrary"), vmem_limit_bytes=VMEM_LIMIT),
        name="swa_prompt",
    )(sinks, x, kpad, vpad, bias, w_q, w_o, g1, b1)


def _swa_sample_kernel(x_ref, k_ref, v_ref, bias_ref, sink_ref, mask_ref, wq_ref, wo_ref,
                       g1_ref, b1_ref, x1_ref, *, sb, t):
    x = x_ref[...]
    q = _dot(x.astype(BF16), wq_ref[...]) * (HEAD_DIM ** -0.5)
    mask = mask_ref[...]
    bias = bias_ref[...]
    sink = sink_ref[...]
    outs = []
    for s in range(sb):
        qs = q[s * t:(s + 1) * t, :]
        parts = []
        for g in range(Q_PER_KV):
            qg = qs[:, g * KV_DIM:(g + 1) * KV_DIM]
            parts.append(jnp.concatenate([qg] * N_KV_HEADS, axis=0) * mask)
        qexp = jnp.concatenate(parts, axis=0).astype(BF16)
        l = _dot_nt(qexp, k_ref[s]) + bias
        m = jnp.maximum(jnp.max(l, axis=1, keepdims=True), sink)
        p = jnp.exp(l - m)
        den = jnp.sum(p, axis=1, keepdims=True) + jnp.exp(sink - m)
        r = _dot(p.astype(BF16), v_ref[s]) / den
        og = []
        for g in range(Q_PER_KV):
            rg = r[g * N_KV_HEADS * t:(g + 1) * N_KV_HEADS * t, :] * mask
            acc = rg[0:t]
            for kh in range(1, N_KV_HEADS):
                acc = acc + rg[kh * t:(kh + 1) * t]
            og.append(acc)
        outs.append(jnp.concatenate(og, axis=1))
    attn = jnp.concatenate(outs, axis=0)
    m_out = _dot(attn.astype(BF16), wo_ref[...])
    x1_ref[...] = _ln(DEEPNORM_ALPHA * x + m_out, g1_ref[...], b1_ref[...])


def _swa_sample_layer(x, row0, n_seq, t, kbuf, vbuf, bias, sink_col, mask, w_q, w_o, g1, b1,
                      sb=SAMPLE_SEQ_BLOCK):
    rows = sb * t
    hr = N_HEADS * t
    blk0 = row0 // rows
    return pl.pallas_call(
        functools.partial(_swa_sample_kernel, sb=sb, t=t),
        grid=(n_seq // sb,),
        in_specs=[
            pl.BlockSpec((rows, D_MODEL), lambda i: (blk0 + i, 0)),
            pl.BlockSpec((sb, KEY_PAD, KV_DIM), lambda i: (i, 0, 0)),
            pl.BlockSpec((sb, KEY_PAD, KV_DIM), lambda i: (i, 0, 0)),
            _full((hr, KEY_PAD)),
            _full((hr, 1)),
            _full((N_KV_HEADS * t, KV_DIM)),
            _full((D_MODEL, D_MODEL)),
            _full((D_MODEL, D_MODEL)),
            _full((1, D_MODEL)),
            _full((1, D_MODEL)),
        ],
        out_specs=pl.BlockSpec((rows, D_MODEL), lambda i: (i, 0)),
        out_shape=jax.ShapeDtypeStruct((n_seq * t, D_MODEL), F32),
        compiler_params=pltpu.CompilerParams(
            dimension_semantics=("arbitrary",), vmem_limit_bytes=VMEM_LIMIT),
        name="swa_sample",
    )(x, kbuf, vbuf, bias, sink_col, mask, w_q, w_o, g1, b1)


def _router_kernel(x_ref, wt_ref, b_ref, tri_ref, idx_ref, gate_ref, rank_ref, cnt_ref, carry,
                   *, tm):
    i = pl.program_id(0)

    @pl.when(i == 0)
    def _():
        carry[...] = jnp.zeros_like(carry)

    logits = lax.dot_general(wt_ref[...], x_ref[...], (((1,), (1,)), ((), ())),
                             precision=lax.Precision.HIGHEST,
                             preferred_element_type=F32) + b_ref[...]
    eidx = lax.broadcasted_iota(jnp.int32, (N_EXPERTS, tm), 0)
    l = logits
    vals, idxs, sels = [], [], []
    for _ in range(TOP_K):
        m = jnp.max(l, axis=0, keepdims=True)
        idx = jnp.min(jnp.where(l == m, eidx, N_EXPERTS), axis=0, keepdims=True)
        sel = eidx == idx
        vals.append(m)
        idxs.append(idx)
        sels.append(sel)
        l = jnp.where(sel, -jnp.inf, l)
    exps = [jnp.exp(v - vals[0]) for v in vals]
    den = exps[0] + exps[1] + exps[2] + exps[3]
    onehot = jnp.zeros((N_EXPERTS, tm), F32)
    for sel in sels:
        onehot = onehot + sel.astype(F32)
    incl = _dot(onehot.astype(BF16), tri_ref[...])
    excl = incl - onehot + carry[...]
    for k in range(TOP_K):
        idx_ref[k:k + 1, :] = idxs[k]
        gate_ref[k:k + 1, :] = exps[k] / den
        rank = jnp.sum(jnp.where(sels[k], excl, 0.0), axis=0, keepdims=True)
        rank_ref[k:k + 1, :] = rank.astype(jnp.int32)
    carry[...] = carry[...] + jnp.sum(onehot, axis=1, keepdims=True)
    cnt_ref[...] = jnp.broadcast_to(carry[...], cnt_ref.shape)


def _route(x1, router_wt, router_b, tri, tm=TM_ROUTER):
    n = x1.shape[0]
    spec_kn = pl.BlockSpec((TOP_K, tm), lambda i: (0, i))
    idx, gate, rank, cnt = pl.pallas_call(
        functools.partial(_router_kernel, tm=tm),
        grid=(n // tm,),
        in_specs=[
            pl.BlockSpec((tm, D_MODEL), lambda i: (i, 0)),
            _full((N_EXPERTS, D_MODEL)),
            _full((N_EXPERTS, 1)),
            _full((tm, tm)),
        ],
        out_specs=[spec_kn, spec_kn, spec_kn, _full((N_EXPERTS, 128))],
        out_shape=[
            jax.ShapeDtypeStruct((TOP_K, n), jnp.int32),
            jax.ShapeDtypeStruct((TOP_K, n), F32),
            jax.ShapeDtypeStruct((TOP_K, n), jnp.int32),
            jax.ShapeDtypeStruct((N_EXPERTS, 128), F32),
        ],
        scratch_shapes=[pltpu.VMEM((N_EXPERTS, 1), F32)],
        compiler_params=pltpu.CompilerParams(
            dimension_semantics=("arbitrary",), vmem_limit_bytes=VMEM_LIMIT),
        name="moe_router",
    )(x1, router_wt, router_b, tri)
    return idx, gate, rank, cnt[:, 0].astype(jnp.int32)


def _row_copy_out(x_ref, xs_hbm, sem, t, p):
    return pltpu.make_async_copy(x_ref.at[pl.ds(t, 1), :], xs_hbm.at[pl.ds(p, 1), :], sem)


def _dispatch_kernel(pos_ref, x_ref, xs_in_hbm, xs_hbm, sem, *, tm, chunk):
    del xs_in_hbm

    def issue(c, carry):
        t0 = c * chunk
        for t in range(chunk):
            for k in range(TOP_K):
                _row_copy_out(x_ref, xs_hbm, sem, t0 + t,
                              pos_ref[0, 0, k * tm + t0 + t]).start(priority=k % 2)
        return carry

    lax.fori_loop(0, tm // chunk, issue, 0)
    for _ in range(TOP_K):
        pltpu.make_async_copy(x_ref, xs_hbm.at[pl.ds(0, tm), :], sem).wait()


def _dispatch(x1, pos3, xs_buf, tm=TM_DISPATCH, chunk=DISPATCH_CHUNK):
    n = x1.shape[0]
    return pl.pallas_call(
        functools.partial(_dispatch_kernel, tm=tm, chunk=chunk),
        grid=(n // tm,),
        in_specs=[
            pl.BlockSpec((1, 1, TOP_K * tm), lambda i: (i, 0, 0), memory_space=pltpu.SMEM),
            pl.BlockSpec((tm, D_MODEL), lambda i: (i, 0)),
            pl.BlockSpec(memory_space=pl.ANY),
        ],
        out_specs=pl.BlockSpec(memory_space=pl.ANY),
        out_shape=jax.ShapeDtypeStruct(xs_buf.shape, xs_buf.dtype),
        scratch_shapes=[pltpu.SemaphoreType.DMA],
        input_output_aliases={2: 0},
        compiler_params=pltpu.CompilerParams(
            dimension_semantics=("arbitrary",), vmem_limit_bytes=VMEM_LIMIT,
            has_side_effects=True),
        name="moe_dispatch",
    )(pos3, x1, xs_buf)


def _expert_kernel(te_ref, nv_ref, xs_ref, wup_ref, bup_ref, wdn_ref, bdn_ref, y_ref):
    del te_ref
    i = pl.program_id(0)

    @pl.when(i < nv_ref[0])
    def _():
        h = _dot(xs_ref[...], wup_ref[...]) + bup_ref[...]
        glu = jnp.minimum(h[:, :D_EXPERT], SWIGLU_LIMIT)
        lin = jnp.clip(h[:, D_EXPERT:], -SWIGLU_LIMIT, SWIGLU_LIMIT)
        act = glu * jax.nn.sigmoid(SWIGLU_ALPHA * glu) * (lin + 1.0)
        y_ref[...] = _dot(act, wdn_ref[...]) + bdn_ref[...]

    @pl.when(i >= nv_ref[0])
    def _():
        y_ref[...] = jnp.zeros_like(y_ref)


def _experts(xs, tile_expert, n_valid, w_up, b_up, w_dn, b_dn, layer, tm=TM_EXPERT):
    rows = xs.shape[0]
    row_map = lambda i, te, nv: (jnp.minimum(i, nv[0] - 1), 0)
    wspec = lambda shape: pl.BlockSpec((None, None) + shape, lambda i, te, nv: (layer, te[i], 0, 0))
    grid_spec = pltpu.PrefetchScalarGridSpec(
        num_scalar_prefetch=2,
        grid=(rows // tm,),
        in_specs=[
            pl.BlockSpec((tm, D_MODEL), row_map),
            wspec((D_MODEL, 2 * D_EXPERT)),
            wspec((1, 2 * D_EXPERT)),
            wspec((D_EXPERT, D_MODEL)),
            wspec((1, D_MODEL)),
        ],
        out_specs=pl.BlockSpec((tm, D_MODEL), lambda i, te, nv: (i, 0)),
    )
    return pl.pallas_call(
        _expert_kernel,
        grid_spec=grid_spec,
        out_shape=jax.ShapeDtypeStruct((rows, D_MODEL), F32),
        compiler_params=pltpu.CompilerParams(
            dimension_semantics=("arbitrary",), vmem_limit_bytes=VMEM_LIMIT),
        name="moe_experts",
    )(tile_expert, n_valid, xs, w_up, b_up, w_dn, b_dn)


def _row_copy_in(ys_hbm, ybuf, sem, k, t, p):
    return pltpu.make_async_copy(ys_hbm.at[pl.ds(p, 1), :], ybuf.at[k, pl.ds(t, 1), :], sem)


def _combine_kernel(pos_ref, posn_ref, x1_ref, gate_ref, pp_ref, ps_ref, wg_ref, wp_ref, g2_ref,
                    b2_ref, *rest, tm, prompt_steps, emit_kv):
    if emit_kv:
        kvg_ref, kvb_ref, wkv_ref, ys_hbm, x3_ref, kv_ref, ybuf_a, ybuf_b, sem = rest
    else:
        ys_hbm, x3_ref, ybuf_a, ybuf_b, sem = rest
    s = pl.program_id(0)
    is_sample = s >= prompt_steps

    def issue(pref, half, ybuf, sm):
        for t in range(tm):
            for k in range(TOP_K):
                _row_copy_in(ys_hbm, ybuf, sm, k, t,
                             pref[0, 0, (half * TOP_K + k) * tm + t]).start(priority=k % 2)

    def wait_all(ybuf, sm):
        for k in range(TOP_K):
            pltpu.make_async_copy(ys_hbm.at[pl.ds(0, tm), :], ybuf.at[k], sm).wait()

    def half_math(half, ybuf):
        rows = slice(half * tm, (half + 1) * tm)
        gate = gate_ref[rows, :]
        f = gate[:, 0:1] * ybuf[0]
        for k in range(1, TOP_K):
            f = f + gate[:, k:k + 1] * ybuf[k]
        x2 = _ln(DEEPNORM_ALPHA * x1_ref[rows, :] + f, g2_ref[...], b2_ref[...])
        p = jnp.where(is_sample, ps_ref[rows, :], pp_ref[rows, :])
        pg = jax.nn.sigmoid(_dot(x2.astype(BF16), wg_ref[...]))
        pp = _dot(p.astype(BF16), wp_ref[...])
        x3 = x2 + pg * pp
        x3_ref[rows, :] = x3
        if emit_kv:
            kv_ref[rows, :] = _dot(_ln(x3, kvg_ref[...], kvb_ref[...]).astype(BF16), wkv_ref[...])

    @pl.when(s == 0)
    def _():
        def first(t, c):
            for k in range(TOP_K):
                _row_copy_in(ys_hbm, ybuf_a, sem.at[0], k, t, pos_ref[0, 0, k * tm + t]).start()
            return c

        lax.fori_loop(0, tm, first, 0)

    wait_all(ybuf_a, sem.at[0])
    issue(pos_ref, 1, ybuf_b, sem.at[1])
    half_math(0, ybuf_a)
    wait_all(ybuf_b, sem.at[1])
    issue(posn_ref, 0, ybuf_a, sem.at[0])
    half_math(1, ybuf_b)

    @pl.when(s == pl.num_programs(0) - 1)
    def _():
        wait_all(ybuf_a, sem.at[0])


def _combine(x1, pos3, gate_t, ys, p_prompt, p_sample, layer, n_prompt, w_gate, w_proj, g2, b2,
             kv_params=None, tm=TM_TOKEN):
    n = x1.shape[0]
    emit_kv = kv_params is not None
    tm2 = 2 * tm
    steps = n // tm2
    prompt_steps = n_prompt // tm2
    sample_steps = steps - prompt_steps
    row = lambda w: pl.BlockSpec((tm2, w), lambda i: (i, 0))
    smem_pos = lambda f: pl.BlockSpec((1, 1, 2 * TOP_K * tm), lambda i: (f(i), 0, 0),
                                      memory_space=pltpu.SMEM)
    in_specs = [
        smem_pos(lambda i: i),
        smem_pos(lambda i: jnp.minimum(i + 1, steps - 1)),
        row(D_MODEL),
        row(TOP_K),
        pl.BlockSpec((tm2, PLE_DIM),
                     lambda i: (layer * prompt_steps + jnp.minimum(i, prompt_steps - 1), 0)),
        pl.BlockSpec((tm2, PLE_DIM),
                     lambda i: (layer * sample_steps + jnp.maximum(i - prompt_steps, 0), 0)),
        _full((D_MODEL, D_MODEL)),
        _full((PLE_DIM, D_MODEL)),
        _full((1, D_MODEL)),
        _full((1, D_MODEL)),
    ]
    args = [pos3, pos3, x1, gate_t, p_prompt, p_sample, w_gate, w_proj, g2, b2]
    out_specs = [row(D_MODEL)]
    out_shape = [jax.ShapeDtypeStruct((n, D_MODEL), F32)]
    if emit_kv:
        in_specs += [_full((1, D_MODEL)), _full((1, D_MODEL)), _full((D_MODEL, 2 * KV_DIM))]
        args += list(kv_params)
        out_specs.append(row(2 * KV_DIM))
        out_shape.append(jax.ShapeDtypeStruct((n, 2 * KV_DIM), F32))
    in_specs.append(pl.BlockSpec(memory_space=pl.ANY))
    args.append(ys)
    outs = pl.pallas_call(
        functools.partial(_combine_kernel, tm=tm, prompt_steps=prompt_steps, emit_kv=emit_kv),
        grid=(steps,),
        in_specs=in_specs,
        out_specs=out_specs,
        out_shape=out_shape,
        scratch_shapes=[pltpu.VMEM((TOP_K, tm, D_MODEL), F32), pltpu.VMEM((TOP_K, tm, D_MODEL), F32),
                        pltpu.SemaphoreType.DMA((2,))],
        compiler_params=pltpu.CompilerParams(
            dimension_semantics=("arbitrary",), vmem_limit_bytes=VMEM_LIMIT),
        name="moe_combine",
    )(*args)
    return outs if emit_kv else (outs[0], None)


def _tile_pos(pos, tm):
    n = pos.shape[1]
    return pos.reshape(TOP_K, n // tm, tm).transpose(1, 0, 2).reshape(n // tm, 1, TOP_K * tm)


def _pair_pos(pos, tm):
    n = pos.shape[1]
    p = pos.reshape(TOP_K, n // (2 * tm), 2, tm).transpose(1, 2, 0, 3)
    return p.reshape(n // (2 * tm), 1, 2 * TOP_K * tm)


def _moe_layer(x1, xs_buf, p_prompt, p_sample, n_prompt, layer, router_wt, router_b, tri,
               w_up, b_up, w_dn, b_dn, w_gate, w_proj, g2, b2, kv_params=None):
    idx, gate, rank, counts = _route(x1, router_wt, router_b, tri)
    padded = ((counts + TM_EXPERT - 1) // TM_EXPERT) * TM_EXPERT
    ends = jnp.cumsum(padded)
    starts = ends - padded
    onehot = idx[:, :, None] == jnp.arange(N_EXPERTS, dtype=jnp.int32)
    pos = jnp.sum(jnp.where(onehot, starts, 0), axis=-1) + rank
    n_tiles = xs_buf.shape[0] // TM_EXPERT
    n_valid = (ends[-1] // TM_EXPERT).astype(jnp.int32)
    tile_start = jnp.minimum(jnp.arange(n_tiles, dtype=jnp.int32), n_valid - 1) * TM_EXPERT
    tile_expert = jnp.sum(tile_start[:, None] >= ends[None, :], axis=1).astype(jnp.int32)
    xs_buf = _dispatch(x1, _tile_pos(pos, TM_DISPATCH), xs_buf)
    ys = _experts(xs_buf, tile_expert, n_valid.reshape(1), w_up, b_up, w_dn, b_dn, layer)
    x3, kv = _combine(x1, _pair_pos(pos, TM_TOKEN), gate.T, ys, p_prompt, p_sample, layer, n_prompt,
                      w_gate, w_proj, g2, b2, kv_params)
    return x3, kv, xs_buf


def _alibi_slopes():
    h = jnp.arange(1, N_HEADS + 1, dtype=F32)
    return 2.0 ** (-8.0 * h / N_HEADS)


def _prompt_bias():
    qi = jnp.arange(WINDOW)[:, None]
    kj = jnp.arange(2 * WINDOW)[None, :]
    dist = qi + WINDOW - kj
    valid = (dist >= 0) & (dist < WINDOW)
    slopes = _alibi_slopes()[:, None, None]
    tables = []
    for first in (True, False):
        ok = valid & (kj >= WINDOW) if first else valid
        tables.append(jnp.where(ok[None], -slopes * dist.astype(F32)[None], NEG_INF))
    return jnp.stack(tables).astype(F32)


def _sample_tables(t, sinks):
    g = jnp.arange(Q_PER_KV)[:, None, None]
    kh = jnp.arange(N_KV_HEADS)[None, :, None]
    tok = jnp.arange(t)[None, None, :]
    head = jnp.broadcast_to(kh * Q_PER_KV + g, (Q_PER_KV, N_KV_HEADS, t)).reshape(-1)
    tok = jnp.broadcast_to(tok, (Q_PER_KV, N_KV_HEADS, t)).reshape(-1)
    kj = jnp.arange(KEY_PAD)[None, :]
    dist = tok[:, None] + WINDOW - kj
    valid = (dist >= 0) & (dist < WINDOW) & (kj < WINDOW + t)
    slopes = _alibi_slopes()[head][:, None]
    bias = jnp.where(valid, -slopes * dist.astype(F32), NEG_INF).astype(F32)
    sink_col = sinks[head][:, None].astype(F32)
    row_kh = jnp.repeat(jnp.arange(N_KV_HEADS), t)[:, None]
    lane_kh = (jnp.arange(KV_DIM) // HEAD_DIM)[None, :]
    mask = (row_kh == lane_kh).astype(F32)
    return bias, sink_col, mask


def _gkd_cols(w):
    r = w.shape[0]
    w4 = w.reshape(r, N_KV_HEADS, Q_PER_KV, HEAD_DIM)
    return w4.transpose(0, 2, 1, 3).reshape(r, N_HEADS * HEAD_DIM)


def _spatial_tables(w_s, b_s, t):
    tril = jnp.tril(jnp.ones((CHUNK, CHUNK), dtype=bool))
    ws_p = jnp.where(tril[None], w_s, 0.0)
    bs_p = jnp.repeat(b_s.T, GROUP_DIM_A, axis=1)
    r = jnp.arange(CHUNK)
    same = (r[:, None] // t) == (r[None, :] // t)
    small = jnp.where(tril[None, :t, :t], w_s[:, :t, :t], 0.0)
    ws_s = jnp.where(same[None], jnp.tile(small, (1, CHUNK // t, CHUNK // t)), 0.0)
    bs_s = jnp.repeat(jnp.tile(b_s[:, :t], (1, CHUNK // t)).T, GROUP_DIM_A, axis=1)
    return jnp.stack([ws_p, ws_s]).astype(BF16), jnp.stack([bs_p, bs_s]).astype(F32)


def kernel(x_prompt, x_sample, cache_k, cache_v, p_prompt, p_sample, ln1_g, ln1_b, ln2_g, ln2_b,
           a_w_in, a_b_in, a_ln_g, a_ln_b, a_w_s, a_b_s, a_w_out, kv_ln_g, kv_ln_b, w_kv,
           b_w_q, b_sinks, b_w_o, router_w, router_b, exp_w_up, exp_b_up, exp_w_dn, exp_b_dn,
           ple_w_proj, ple_w_gate):
    bsz, seq, _ = x_prompt.shape
    n_seq, t, _ = x_sample.shape
    n_p = bsz * seq
    n_s = n_seq * t
    n = n_p + n_s
    depth = ln1_g.shape[0]
    n_a = a_w_in.shape[0]

    xp2 = x_prompt.reshape(n_p, D_MODEL)
    xs2 = x_sample.reshape(n_s, D_MODEL)
    pp2 = p_prompt.reshape(depth * n_p, PLE_DIM)
    ps2 = p_sample.reshape(depth * n_s, PLE_DIM)
    row = lambda v: v.reshape(1, -1).astype(F32)

    xs_buf = jnp.zeros((n * TOP_K + N_EXPERTS * TM_EXPERT, D_MODEL), F32)
    tri = (jnp.arange(TM_ROUTER)[:, None] <= jnp.arange(TM_ROUTER)[None, :]).astype(BF16)
    b_up4 = exp_b_up.reshape(depth, N_EXPERTS, 1, 2 * D_EXPERT)
    b_dn4 = exp_b_dn.reshape(depth, N_EXPERTS, 1, D_MODEL)
    prompt_bias = _prompt_bias()

    chunk_v = []
    kv_params = (row(kv_ln_g), row(kv_ln_b), w_kv.astype(BF16))
    x = None
    kp = vp = k_buf = v_buf = None
    kpad = vpad = kbuf_pad = vbuf_pad = None
    for i in range(depth):
        if i < n_a:
            ws2, bs2 = _spatial_tables(a_w_s[i], a_b_s[i], t)
            xa, xb, xb_row0 = (xp2, xs2, 0) if i == 0 else (x, x, n_p)
            x1, v_rows = _gmlp_layer(xa, xb, xb_row0, n_p, n_s, a_w_in[i].astype(BF16),
                                     row(a_b_in[i]), row(a_ln_g[i]), row(a_ln_b[i]), ws2, bs2,
                                     a_w_out[i].astype(BF16), row(ln1_g[i]), row(ln1_b[i]))
            chunk_v.append(v_rows.reshape(n_seq, t, D_MODEL))
        else:
            j = i - n_a
            wq = b_w_q[j].astype(BF16)
            wo = b_w_o[j].astype(BF16)
            x1_p = _swa_prompt_layer(x, bsz, seq, kpad, vpad, prompt_bias,
                                     b_sinks[j].astype(F32), wq, wo, row(ln1_g[i]), row(ln1_b[i]))
            s_bias, s_sink, s_mask = _sample_tables(t, b_sinks[j])
            x1_s = _swa_sample_layer(x, n_p, n_seq, t, kbuf_pad, vbuf_pad, s_bias, s_sink, s_mask,
                                     _gkd_cols(wq), _gkd_cols(wo.T).T, row(ln1_g[i]), row(ln1_b[i]))
            x1 = jnp.concatenate([x1_p, x1_s], axis=0)
        x, kv, xs_buf = _moe_layer(
            x1, xs_buf, pp2, ps2, n_p, i, router_w[i].T, router_b[i].reshape(N_EXPERTS, 1), tri,
            exp_w_up, b_up4, exp_w_dn, b_dn4, ple_w_gate[i].astype(BF16),
            ple_w_proj[i].astype(BF16), row(ln2_g[i]), row(ln2_b[i]),
            kv_params if i == n_a - 1 else None)
        if i == n_a - 1:
            k_all, v_all = kv[:, :KV_DIM], kv[:, KV_DIM:]
            kp = k_all[:n_p].reshape(bsz, seq, KV_DIM)
            vp = v_all[:n_p].reshape(bsz, seq, KV_DIM)
            k_buf = jnp.concatenate([cache_k.reshape(n_seq, WINDOW, KV_DIM),
                                     k_all[n_p:].reshape(n_seq, t, KV_DIM)], axis=1)
            v_buf = jnp.concatenate([cache_v.reshape(n_seq, WINDOW, KV_DIM),
                                     v_all[n_p:].reshape(n_seq, t, KV_DIM)], axis=1)
            front = ((0, 0), (WINDOW, 0), (0, 0))
            kpad = jnp.pad(kp, front).astype(BF16)
            vpad = jnp.pad(vp, front).astype(BF16)
            tail = ((0, 0), (0, KEY_PAD - WINDOW - t), (0, 0))
            kbuf_pad = jnp.pad(k_buf, tail).astype(BF16)
            vbuf_pad = jnp.pad(v_buf, tail).astype(BF16)

    heads = (N_KV_HEADS, HEAD_DIM)
    return (x[:n_p].reshape(bsz, seq, D_MODEL),
            x[n_p:].reshape(n_seq, t, D_MODEL),
            jnp.stack(chunk_v),
            kp[:, -WINDOW:].reshape(bsz, WINDOW, *heads),
            vp[:, -WINDOW:].reshape(bsz, WINDOW, *heads),
            k_buf[:, -WINDOW:].reshape(n_seq, WINDOW, *heads),
            v_buf[:, -WINDOW:].reshape(n_seq, WINDOW, *heads))
```

```python
import functools

import jax
import jax.numpy as jnp
from jax import lax
from jax.experimental import pallas as pl
from jax.experimental.pallas import tpu as pltpu

F32 = jnp.float32
BF16 = jnp.bfloat16

D_MODEL = 1024
DEPTH = 4
N_A_LAYERS = 2
CHUNK = 128
N_GROUPS_A = 8
GROUP_DIM_A = D_MODEL // N_GROUPS_A
HEAD_DIM = 64
N_HEADS = 16
N_KV_HEADS = 4
Q_PER_KV = 4
KV_DIM = N_KV_HEADS * HEAD_DIM
WINDOW = 128
N_EXPERTS = 32
TOP_K = 4
D_EXPERT = 1024
SWIGLU_LIMIT = 7.0
SWIGLU_ALPHA = 1.702
PLE_DIM = 256
DEEPNORM_ALPHA = (2 * DEPTH) ** 0.25
LN_EPS = 1e-5
NEG_INF = -1e30

TM_TOKEN = 256
TM_DISPATCH = 512
DISPATCH_CHUNK = 128
TM_ROUTER = 512
TM_EXPERT = 512
SAMPLE_SEQ_BLOCK = 8
KEY_PAD = 256
VMEM_LIMIT = 56 * 1024 * 1024


def _ln(x, g, b):
    mu = jnp.mean(x, axis=-1, keepdims=True)
    xc = x - mu
    var = jnp.mean(xc * xc, axis=-1, keepdims=True)
    return xc * lax.rsqrt(var + LN_EPS) * g + b


def _dot(a, b):
    return jnp.dot(a, b, preferred_element_type=F32)


def _dot_nt(a, b):
    return lax.dot_general(a, b, (((1,), (1,)), ((), ())), preferred_element_type=F32)


def _full(shape):
    n = len(shape)
    return pl.BlockSpec(shape, lambda *_: (0,) * n)


def _gmlp_math(x, win_ref, bin_ref, lng_ref, lnb_ref, ws_ref, bs_ref, wout_ref, g1_ref, b1_ref):
    h = _dot(x.astype(BF16), win_ref[...]) + bin_ref[...]
    h = 0.5 * h * (1.0 + lax.erf(h * (2.0 ** -0.5)))
    u = h[:, :D_MODEL]
    v = _ln(h[:, D_MODEL:], lng_ref[...], lnb_ref[...])
    vb = v.astype(BF16)
    rows = []
    for c in range(x.shape[0] // CHUNK):
        cols = []
        for g in range(N_GROUPS_A):
            blk = vb[c * CHUNK:(c + 1) * CHUNK, g * GROUP_DIM_A:(g + 1) * GROUP_DIM_A]
            cols.append(_dot(ws_ref[g], blk))
        rows.append(jnp.concatenate(cols, axis=1) + bs_ref[...])
    s = jnp.concatenate(rows, axis=0)
    gated = (u * s).astype(BF16)
    m = _dot(gated, wout_ref[...])
    return _ln(DEEPNORM_ALPHA * x + m, g1_ref[...], b1_ref[...]), v


def _gmlp_kernel(xa_ref, xb_ref, *refs, p_tiles):
    *w_refs, x1_ref, v_ref = refs
    x = jnp.where(pl.program_id(0) >= p_tiles, xb_ref[...], xa_ref[...])
    x1_ref[...], v_ref[...] = _gmlp_math(x, *w_refs)


def _gmlp_weight_specs(kind):
    return [
        _full((D_MODEL, 2 * D_MODEL)),
        _full((1, 2 * D_MODEL)),
        _full((1, D_MODEL)),
        _full((1, D_MODEL)),
        pl.BlockSpec((None, N_GROUPS_A, CHUNK, CHUNK), lambda i: (kind(i), 0, 0, 0)),
        pl.BlockSpec((None, CHUNK, D_MODEL), lambda i: (kind(i), 0, 0)),
        _full((D_MODEL, D_MODEL)),
        _full((1, D_MODEL)),
        _full((1, D_MODEL)),
    ]


def _gmlp_layer(xa, xb, xb_row0, n_prompt, n_sample, weights, tm=TM_TOKEN):
    n = n_prompt + n_sample
    p_tiles = n_prompt // tm
    n_tiles = n // tm
    s_tiles = n_tiles - p_tiles
    b_tile0 = xb_row0 // tm
    kind = lambda i: jnp.where(i >= p_tiles, 1, 0)
    x1, v = pl.pallas_call(
        functools.partial(_gmlp_kernel, p_tiles=p_tiles),
        grid=(n_tiles,),
        in_specs=[
            pl.BlockSpec((tm, D_MODEL), lambda i: (jnp.minimum(i, p_tiles - 1), 0)),
            pl.BlockSpec((tm, D_MODEL), lambda i: (jnp.maximum(i - p_tiles, 0) + b_tile0, 0)),
        ] + _gmlp_weight_specs(kind),
        out_specs=[
            pl.BlockSpec((tm, D_MODEL), lambda i: (i, 0)),
            pl.BlockSpec((tm, D_MODEL), lambda i: (jnp.maximum(i - (p_tiles - 1), 0), 0)),
        ],
        out_shape=[
            jax.ShapeDtypeStruct((n, D_MODEL), F32),
            jax.ShapeDtypeStruct(((s_tiles + 1) * tm, D_MODEL), F32),
        ],
        compiler_params=pltpu.CompilerParams(
            dimension_semantics=("arbitrary",), vmem_limit_bytes=VMEM_LIMIT),
        name="gmlp_mixer",
    )(xa, xb, *weights)
    return x1, v[tm:]


def _swa_prompt_kernel(sink_ref, x_ref, k_ref, v_ref, bias_ref, wq_ref, wo_ref, g1_ref, b1_ref,
                       x1_ref, *, tq):
    i = pl.program_id(1)
    x = x_ref[...]
    qb = (_dot(x.astype(BF16), wq_ref[...]) * (HEAD_DIM ** -0.5)).astype(BF16)
    blocks = []
    for j in range(tq // WINDOW):
        blk = i * (tq // WINDOW) + j
        kstart = pl.multiple_of(blk * WINDOW, WINDOW)
        kb = k_ref[pl.ds(kstart, 2 * WINDOW), :]
        vb = v_ref[pl.ds(kstart, 2 * WINDOW), :]
        sel = jnp.where(blk == 0, 0, 1)
        outs = []
        for h in range(N_HEADS):
            kh = h // Q_PER_KV
            qh = qb[j * WINDOW:(j + 1) * WINDOW, h * HEAD_DIM:(h + 1) * HEAD_DIM]
            s = _dot_nt(qh, kb[:, kh * HEAD_DIM:(kh + 1) * HEAD_DIM])
            l = s + bias_ref[sel, h]
            sink = sink_ref[h]
            m = jnp.maximum(jnp.max(l, axis=1, keepdims=True), sink)
            p = jnp.exp(l - m)
            den = jnp.sum(p, axis=1, keepdims=True) + jnp.exp(sink - m)
            o = _dot(p.astype(BF16), vb[:, kh * HEAD_DIM:(kh + 1) * HEAD_DIM])
            outs.append(o / den)
        blocks.append(jnp.concatenate(outs, axis=1))
    attn = jnp.concatenate(blocks, axis=0)
    m_out = _dot(attn.astype(BF16), wo_ref[...])
    x1_ref[...] = _ln(DEEPNORM_ALPHA * x + m_out, g1_ref[...], b1_ref[...])


def _swa_prompt_layer(x, bsz, seq, kpad, vpad, bias, sinks, w_q, w_o, g1, b1, tq=TM_TOKEN):
    nq = seq // tq
    return pl.pallas_call(
        functools.partial(_swa_prompt_kernel, tq=tq),
        grid=(bsz, nq),
        in_specs=[
            pl.BlockSpec(memory_space=pltpu.SMEM),
            pl.BlockSpec((tq, D_MODEL), lambda b, i: (b * nq + i, 0)),
            pl.BlockSpec((None, seq + WINDOW, KV_DIM), lambda b, i: (b, 0, 0)),
            pl.BlockSpec((None, seq + WINDOW, KV_DIM), lambda b, i: (b, 0, 0)),
            _full((2, N_HEADS, WINDOW, 2 * WINDOW)),
            _full((D_MODEL, D_MODEL)),
            _full((D_MODEL, D_MODEL)),
            _full((1, D_MODEL)),
            _full((1, D_MODEL)),
        ],
        out_specs=pl.BlockSpec((tq, D_MODEL), lambda b, i: (b * nq + i, 0)),
        out_shape=jax.ShapeDtypeStruct((bsz * seq, D_MODEL), F32),
        compiler_params=pltpu.CompilerParams(
            dimension_semantics=("arbitrary", "arbitrary"), vmem_limit_bytes=VMEM_LIMIT),
        name="swa_prompt",
    )(sinks, x, kpad, vpad, bias, w_q, w_o, g1, b1)


def _swa_sample_kernel(x_ref, k_ref, v_ref, bias_ref, sink_ref, mask_ref, wq_ref, wo_ref,
                       g1_ref, b1_ref, x1_ref, *, sb, t):
    x = x_ref[...]
    q = _dot(x.astype(BF16), wq_ref[...]) * (HEAD_DIM ** -0.5)
    mask = mask_ref[...]
    bias = bias_ref[...]
    sink = sink_ref[...]
    outs = []
    for s in range(sb):
        qs = q[s * t:(s + 1) * t, :]
        parts = []
        for g in range(Q_PER_KV):
            qg = qs[:, g * KV_DIM:(g + 1) * KV_DIM]
            parts.append(jnp.concatenate([qg] * N_KV_HEADS, axis=0) * mask)
        qexp = jnp.concatenate(parts, axis=0).astype(BF16)
        l = _dot_nt(qexp, k_ref[s]) + bias
        m = jnp.maximum(jnp.max(l, axis=1, keepdims=True), sink)
        p = jnp.exp(l - m)
        den = jnp.sum(p, axis=1, keepdims=True) + jnp.exp(sink - m)
        r = _dot(p.astype(BF16), v_ref[s]) / den
        og = []
        for g in range(Q_PER_KV):
            rg = r[g * N_KV_HEADS * t:(g + 1) * N_KV_HEADS * t, :] * mask
            acc = rg[0:t]
            for kh in range(1, N_KV_HEADS):
                acc = acc + rg[kh * t:(kh + 1) * t]
            og.append(acc)
        outs.append(jnp.concatenate(og, axis=1))
    attn = jnp.concatenate(outs, axis=0)
    m_out = _dot(attn.astype(BF16), wo_ref[...])
    x1_ref[...] = _ln(DEEPNORM_ALPHA * x + m_out, g1_ref[...], b1_ref[...])


def _swa_sample_layer(x, row0, n_seq, t, kbuf, vbuf, bias, sink_col, mask, w_q, w_o, g1, b1,
                      sb=SAMPLE_SEQ_BLOCK):
    rows = sb * t
    hr = N_HEADS * t
    blk0 = row0 // rows
    return pl.pallas_call(
        functools.partial(_swa_sample_kernel, sb=sb, t=t),
        grid=(n_seq // sb,),
        in_specs=[
            pl.BlockSpec((rows, D_MODEL), lambda i: (blk0 + i, 0)),
            pl.BlockSpec((sb, KEY_PAD, KV_DIM), lambda i: (i, 0, 0)),
            pl.BlockSpec((sb, KEY_PAD, KV_DIM), lambda i: (i, 0, 0)),
            _full((hr, KEY_PAD)),
            _full((hr, 1)),
            _full((N_KV_HEADS * t, KV_DIM)),
            _full((D_MODEL, D_MODEL)),
            _full((D_MODEL, D_MODEL)),
            _full((1, D_MODEL)),
            _full((1, D_MODEL)),
        ],
        out_specs=pl.BlockSpec((rows, D_MODEL), lambda i: (i, 0)),
        out_shape=jax.ShapeDtypeStruct((n_seq * t, D_MODEL), F32),
        compiler_params=pltpu.CompilerParams(
            dimension_semantics=("arbitrary",), vmem_limit_bytes=VMEM_LIMIT),
        name="swa_sample",
    )(x, kbuf, vbuf, bias, sink_col, mask, w_q, w_o, g1, b1)


def _router_kernel(x_ref, wt_ref, b_ref, tri_ref, idx_ref, gate_ref, rank_ref, cnt_ref, carry,
                   *, tm):
    i = pl.program_id(0)

    @pl.when(i == 0)
    def _():
        carry[...] = jnp.zeros_like(carry)

    logits = lax.dot_general(wt_ref[...], x_ref[...], (((1,), (1,)), ((), ())),
                             precision=lax.Precision.HIGHEST,
                             preferred_element_type=F32) + b_ref[...]
    eidx = lax.broadcasted_iota(jnp.int32, (N_EXPERTS, tm), 0)
    l = logits
    vals, idxs, sels = [], [], []
    for _ in range(TOP_K):
        m = jnp.max(l, axis=0, keepdims=True)
        idx = jnp.min(jnp.where(l == m, eidx, N_EXPERTS), axis=0, keepdims=True)
        sel = eidx == idx
        vals.append(m)
        idxs.append(idx)
        sels.append(sel)
        l = jnp.where(sel, -jnp.inf, l)
    exps = [jnp.exp(v - vals[0]) for v in vals]
    den = exps[0] + exps[1] + exps[2] + exps[3]
    onehot = jnp.zeros((N_EXPERTS, tm), F32)
    for sel in sels:
        onehot = onehot + sel.astype(F32)
    incl = _dot(onehot.astype(BF16), tri_ref[...])
    excl = incl - onehot + carry[...]
    for k in range(TOP_K):
        idx_ref[k:k + 1, :] = idxs[k]
        gate_ref[k:k + 1, :] = exps[k] / den
        rank = jnp.sum(jnp.where(sels[k], excl, 0.0), axis=0, keepdims=True)
        rank_ref[k:k + 1, :] = rank.astype(jnp.int32)
    carry[...] = carry[...] + jnp.sum(onehot, axis=1, keepdims=True)
    cnt_ref[...] = jnp.broadcast_to(carry[...], cnt_ref.shape)


def _route(x1, router_wt, router_b, tri, tm=TM_ROUTER):
    n = x1.shape[0]
    spec_kn = pl.BlockSpec((TOP_K, tm), lambda i: (0, i))
    idx, gate, rank, cnt = pl.pallas_call(
        functools.partial(_router_kernel, tm=tm),
        grid=(n // tm,),
        in_specs=[
            pl.BlockSpec((tm, D_MODEL), lambda i: (i, 0)),
            _full((N_EXPERTS, D_MODEL)),
            _full((N_EXPERTS, 1)),
            _full((tm, tm)),
        ],
        out_specs=[spec_kn, spec_kn, spec_kn, _full((N_EXPERTS, 128))],
        out_shape=[
            jax.ShapeDtypeStruct((TOP_K, n), jnp.int32),
            jax.ShapeDtypeStruct((TOP_K, n), F32),
            jax.ShapeDtypeStruct((TOP_K, n), jnp.int32),
            jax.ShapeDtypeStruct((N_EXPERTS, 128), F32),
        ],
        scratch_shapes=[pltpu.VMEM((N_EXPERTS, 1), F32)],
        compiler_params=pltpu.CompilerParams(
            dimension_semantics=("arbitrary",), vmem_limit_bytes=VMEM_LIMIT),
        name="moe_router",
    )(x1, router_wt, router_b, tri)
    return idx, gate, rank, cnt[:, 0].astype(jnp.int32)


def _row_copy_out(x_ref, xs_hbm, sem, t, p):
    return pltpu.make_async_copy(x_ref.at[pl.ds(t, 1), :], xs_hbm.at[pl.ds(p, 1), :], sem)


def _dispatch_kernel(pos_ref, x_ref, xs_in_hbm, xs_hbm, sem, *, tm, chunk):
    del xs_in_hbm

    def issue(c, carry):
        t0 = c * chunk
        for t in range(chunk):
            for k in range(TOP_K):
                _row_copy_out(x_ref, xs_hbm, sem, t0 + t,
                              pos_ref[0, 0, k * tm + t0 + t]).start(priority=k % 2)
        return carry

    lax.fori_loop(0, tm // chunk, issue, 0)
    for _ in range(TOP_K):
        pltpu.make_async_copy(x_ref, xs_hbm.at[pl.ds(0, tm), :], sem).wait()


def _dispatch(x1, pos3, xs_buf, tm=TM_DISPATCH, chunk=DISPATCH_CHUNK):
    n = x1.shape[0]
    return pl.pallas_call(
        functools.partial(_dispatch_kernel, tm=tm, chunk=chunk),
        grid=(n // tm,),
        in_specs=[
            pl.BlockSpec((1, 1, TOP_K * tm), lambda i: (i, 0, 0), memory_space=pltpu.SMEM),
            pl.BlockSpec((tm, D_MODEL), lambda i: (i, 0)),
            pl.BlockSpec(memory_space=pl.ANY),
        ],
        out_specs=pl.BlockSpec(memory_space=pl.ANY),
        out_shape=jax.ShapeDtypeStruct(xs_buf.shape, xs_buf.dtype),
        scratch_shapes=[pltpu.SemaphoreType.DMA],
        input_output_aliases={2: 0},
        compiler_params=pltpu.CompilerParams(
            dimension_semantics=("arbitrary",), vmem_limit_bytes=VMEM_LIMIT,
            has_side_effects=True),
        name="moe_dispatch",
    )(pos3, x1, xs_buf)


def _expert_kernel(te_ref, nv_ref, xs_ref, wup_ref, bup_ref, wdn_ref, bdn_ref, y_ref):
    del te_ref
    i = pl.program_id(0)

    @pl.when(i < nv_ref[0])
    def _():
        h = _dot(xs_ref[...], wup_ref[...]) + bup_ref[...]
        glu = jnp.minimum(h[:, :D_EXPERT], SWIGLU_LIMIT)
        lin = jnp.clip(h[:, D_EXPERT:], -SWIGLU_LIMIT, SWIGLU_LIMIT)
        act = glu * jax.nn.sigmoid(SWIGLU_ALPHA * glu) * (lin + 1.0)
        y_ref[...] = _dot(act, wdn_ref[...]) + bdn_ref[...]

    @pl.when(i >= nv_ref[0])
    def _():
        y_ref[...] = jnp.zeros_like(y_ref)


def _experts(xs, tile_expert, n_valid, w_up, b_up, w_dn, b_dn, layer, tm=TM_EXPERT):
    rows = xs.shape[0]
    row_map = lambda i, te, nv: (jnp.minimum(i, nv[0] - 1), 0)
    wspec = lambda shape: pl.BlockSpec((None, None) + shape, lambda i, te, nv: (layer, te[i], 0, 0))
    grid_spec = pltpu.PrefetchScalarGridSpec(
        num_scalar_prefetch=2,
        grid=(rows // tm,),
        in_specs=[
            pl.BlockSpec((tm, D_MODEL), row_map),
            wspec((D_MODEL, 2 * D_EXPERT)),
            wspec((1, 2 * D_EXPERT)),
            wspec((D_EXPERT, D_MODEL)),
            wspec((1, D_MODEL)),
        ],
        out_specs=pl.BlockSpec((tm, D_MODEL), lambda i, te, nv: (i, 0)),
    )
    return pl.pallas_call(
        _expert_kernel,
        grid_spec=grid_spec,
        out_shape=jax.ShapeDtypeStruct((rows, D_MODEL), F32),
        compiler_params=pltpu.CompilerParams(
            dimension_semantics=("arbitrary",), vmem_limit_bytes=VMEM_LIMIT),
        name="moe_experts",
    )(tile_expert, n_valid, xs, w_up, b_up, w_dn, b_dn)


def _row_copy_in(ys_hbm, ybuf, sem, k, t, p):
    return pltpu.make_async_copy(ys_hbm.at[pl.ds(p, 1), :], ybuf.at[k, pl.ds(t, 1), :], sem)


def _combine_kernel(pos_ref, posn_ref, x1_ref, gate_ref, pp_ref, ps_ref, wg_ref, wp_ref, g2_ref,
                    b2_ref, *rest, tm, prompt_steps, emit_kv, fuse_gmlp):
    kv_ref = v_ref = None
    if emit_kv:
        kvg_ref, kvb_ref, wkv_ref, ys_hbm, x3_ref, kv_ref, ybuf_a, ybuf_b, sem = rest
    elif fuse_gmlp:
        *gmlp_refs, ys_hbm, x3_ref, v_ref, ybuf_a, ybuf_b, sem = rest
    else:
        ys_hbm, x3_ref, ybuf_a, ybuf_b, sem = rest
    s = pl.program_id(0)
    is_sample = s >= prompt_steps

    def issue(pref, half, ybuf, sm):
        for t in range(tm):
            for k in range(TOP_K):
                _row_copy_in(ys_hbm, ybuf, sm, k, t,
                             pref[0, 0, (half * TOP_K + k) * tm + t]).start(priority=k % 2)

    def wait_all(ybuf, sm):
        for k in range(TOP_K):
            pltpu.make_async_copy(ys_hbm.at[pl.ds(0, tm), :], ybuf.at[k], sm).wait()

    def half_math(half, ybuf):
        rows = slice(half * tm, (half + 1) * tm)
        gate = gate_ref[rows, :]
        f = gate[:, 0:1] * ybuf[0]
        for k in range(1, TOP_K):
            f = f + gate[:, k:k + 1] * ybuf[k]
        x2 = _ln(DEEPNORM_ALPHA * x1_ref[rows, :] + f, g2_ref[...], b2_ref[...])
        p = jnp.where(is_sample, ps_ref[rows, :], pp_ref[rows, :])
        pg = jax.nn.sigmoid(_dot(x2.astype(BF16), wg_ref[...]))
        pp = _dot(p.astype(BF16), wp_ref[...])
        x3 = x2 + pg * pp
        if fuse_gmlp:
            x3_ref[rows, :], v_ref[rows, :] = _gmlp_math(x3, *gmlp_refs)
        else:
            x3_ref[rows, :] = x3
        if emit_kv:
            kv_ref[rows, :] = _dot(_ln(x3, kvg_ref[...], kvb_ref[...]).astype(BF16), wkv_ref[...])

    @pl.when(s == 0)
    def _():
        def first(t, c):
            for k in range(TOP_K):
                _row_copy_in(ys_hbm, ybuf_a, sem.at[0], k, t, pos_ref[0, 0, k * tm + t]).start()
            return c

        lax.fori_loop(0, tm, first, 0)

    wait_all(ybuf_a, sem.at[0])
    issue(pos_ref, 1, ybuf_b, sem.at[1])
    half_math(0, ybuf_a)
    wait_all(ybuf_b, sem.at[1])
    issue(posn_ref, 0, ybuf_a, sem.at[0])
    half_math(1, ybuf_b)

    @pl.when(s == pl.num_programs(0) - 1)
    def _():
        wait_all(ybuf_a, sem.at[0])


def _combine(x1, pos3, gate_t, ys, p_prompt, p_sample, layer, n_prompt, w_gate, w_proj, g2, b2,
             kv_params=None, gmlp_weights=None, tm=TM_TOKEN):
    n = x1.shape[0]
    emit_kv = kv_params is not None
    fuse_gmlp = gmlp_weights is not None
    assert not (emit_kv and fuse_gmlp)
    tm2 = 2 * tm
    steps = n // tm2
    prompt_steps = n_prompt // tm2
    sample_steps = steps - prompt_steps
    row = lambda w: pl.BlockSpec((tm2, w), lambda i: (i, 0))
    smem_pos = lambda f: pl.BlockSpec((1, 1, 2 * TOP_K * tm), lambda i: (f(i), 0, 0),
                                      memory_space=pltpu.SMEM)
    in_specs = [
        smem_pos(lambda i: i),
        smem_pos(lambda i: jnp.minimum(i + 1, steps - 1)),
        row(D_MODEL),
        row(TOP_K),
        pl.BlockSpec((tm2, PLE_DIM),
                     lambda i: (layer * prompt_steps + jnp.minimum(i, prompt_steps - 1), 0)),
        pl.BlockSpec((tm2, PLE_DIM),
                     lambda i: (layer * sample_steps + jnp.maximum(i - prompt_steps, 0), 0)),
        _full((D_MODEL, D_MODEL)),
        _full((PLE_DIM, D_MODEL)),
        _full((1, D_MODEL)),
        _full((1, D_MODEL)),
    ]
    args = [pos3, pos3, x1, gate_t, p_prompt, p_sample, w_gate, w_proj, g2, b2]
    out_specs = [row(D_MODEL)]
    out_shape = [jax.ShapeDtypeStruct((n, D_MODEL), F32)]
    if emit_kv:
        in_specs += [_full((1, D_MODEL)), _full((1, D_MODEL)), _full((D_MODEL, 2 * KV_DIM))]
        args += list(kv_params)
        out_specs.append(row(2 * KV_DIM))
        out_shape.append(jax.ShapeDtypeStruct((n, 2 * KV_DIM), F32))
    if fuse_gmlp:
        in_specs += _gmlp_weight_specs(lambda i: jnp.where(i >= prompt_steps, 1, 0))
        args += list(gmlp_weights)
        out_specs.append(pl.BlockSpec((tm2, D_MODEL),
                                      lambda i: (jnp.maximum(i - (prompt_steps - 1), 0), 0)))
        out_shape.append(jax.ShapeDtypeStruct(((sample_steps + 1) * tm2, D_MODEL), F32))
    in_specs.append(pl.BlockSpec(memory_space=pl.ANY))
    args.append(ys)
    outs = pl.pallas_call(
        functools.partial(_combine_kernel, tm=tm, prompt_steps=prompt_steps, emit_kv=emit_kv,
                          fuse_gmlp=fuse_gmlp),
        grid=(steps,),
        in_specs=in_specs,
        out_specs=out_specs,
        out_shape=out_shape,
        scratch_shapes=[pltpu.VMEM((TOP_K, tm, D_MODEL), F32), pltpu.VMEM((TOP_K, tm, D_MODEL), F32),
                        pltpu.SemaphoreType.DMA((2,))],
        compiler_params=pltpu.CompilerParams(
            dimension_semantics=("arbitrary",), vmem_limit_bytes=VMEM_LIMIT),
        name="moe_combine",
    )(*args)
    if fuse_gmlp:
        return outs[0], outs[1][tm2:]
    return outs if emit_kv else (outs[0], None)


def _tile_pos(pos, tm):
    n = pos.shape[1]
    return pos.reshape(TOP_K, n // tm, tm).transpose(1, 0, 2).reshape(n // tm, 1, TOP_K * tm)


def _pair_pos(pos, tm):
    n = pos.shape[1]
    p = pos.reshape(TOP_K, n // (2 * tm), 2, tm).transpose(1, 2, 0, 3)
    return p.reshape(n // (2 * tm), 1, 2 * TOP_K * tm)


def _moe_layer(x1, xs_buf, p_prompt, p_sample, n_prompt, layer, router_wt, router_b, tri,
               w_up, b_up, w_dn, b_dn, w_gate, w_proj, g2, b2, kv_params=None, gmlp_weights=None):
    idx, gate, rank, counts = _route(x1, router_wt, router_b, tri)
    padded = ((counts + TM_EXPERT - 1) // TM_EXPERT) * TM_EXPERT
    ends = jnp.cumsum(padded)
    starts = ends - padded
    onehot = idx[:, :, None] == jnp.arange(N_EXPERTS, dtype=jnp.int32)
    pos = jnp.sum(jnp.where(onehot, starts, 0), axis=-1) + rank
    n_tiles = xs_buf.shape[0] // TM_EXPERT
    n_valid = (ends[-1] // TM_EXPERT).astype(jnp.int32)
    tile_start = jnp.minimum(jnp.arange(n_tiles, dtype=jnp.int32), n_valid - 1) * TM_EXPERT
    tile_expert = jnp.sum(tile_start[:, None] >= ends[None, :], axis=1).astype(jnp.int32)
    xs_buf = _dispatch(x1, _tile_pos(pos, TM_DISPATCH), xs_buf)
    ys = _experts(xs_buf, tile_expert, n_valid.reshape(1), w_up, b_up, w_dn, b_dn, layer)
    x3, extra = _combine(x1, _pair_pos(pos, TM_TOKEN), gate.T, ys, p_prompt, p_sample, layer, n_prompt,
                         w_gate, w_proj, g2, b2, kv_params, gmlp_weights)
    return x3, extra, xs_buf


def _alibi_slopes():
    h = jnp.arange(1, N_HEADS + 1, dtype=F32)
    return 2.0 ** (-8.0 * h / N_HEADS)


def _prompt_bias():
    qi = jnp.arange(WINDOW)[:, None]
    kj = jnp.arange(2 * WINDOW)[None, :]
    dist = qi + WINDOW - kj
    valid = (dist >= 0) & (dist < WINDOW)
    slopes = _alibi_slopes()[:, None, None]
    tables = []
    for first in (True, False):
        ok = valid & (kj >= WINDOW) if first else valid
        tables.append(jnp.where(ok[None], -slopes * dist.astype(F32)[None], NEG_INF))
    return jnp.stack(tables).astype(F32)


def _sample_tables(t, sinks):
    g = jnp.arange(Q_PER_KV)[:, None, None]
    kh = jnp.arange(N_KV_HEADS)[None, :, None]
    tok = jnp.arange(t)[None, None, :]
    head = jnp.broadcast_to(kh * Q_PER_KV + g, (Q_PER_KV, N_KV_HEADS, t)).reshape(-1)
    tok = jnp.broadcast_to(tok, (Q_PER_KV, N_KV_HEADS, t)).reshape(-1)
    kj = jnp.arange(KEY_PAD)[None, :]
    dist = tok[:, None] + WINDOW - kj
    valid = (dist >= 0) & (dist < WINDOW) & (kj < WINDOW + t)
    slopes = _alibi_slopes()[head][:, None]
    bias = jnp.where(valid, -slopes * dist.astype(F32), NEG_INF).astype(F32)
    sink_col = sinks[head][:, None].astype(F32)
    row_kh = jnp.repeat(jnp.arange(N_KV_HEADS), t)[:, None]
    lane_kh = (jnp.arange(KV_DIM) // HEAD_DIM)[None, :]
    mask = (row_kh == lane_kh).astype(F32)
    return bias, sink_col, mask


def _gkd_cols(w):
    r = w.shape[0]
    w4 = w.reshape(r, N_KV_HEADS, Q_PER_KV, HEAD_DIM)
    return w4.transpose(0, 2, 1, 3).reshape(r, N_HEADS * HEAD_DIM)


def _spatial_tables(w_s, b_s, t):
    tril = jnp.tril(jnp.ones((CHUNK, CHUNK), dtype=bool))
    ws_p = jnp.where(tril[None], w_s, 0.0)
    bs_p = jnp.repeat(b_s.T, GROUP_DIM_A, axis=1)
    r = jnp.arange(CHUNK)
    same = (r[:, None] // t) == (r[None, :] // t)
    small = jnp.where(tril[None, :t, :t], w_s[:, :t, :t], 0.0)
    ws_s = jnp.where(same[None], jnp.tile(small, (1, CHUNK // t, CHUNK // t)), 0.0)
    bs_s = jnp.repeat(jnp.tile(b_s[:, :t], (1, CHUNK // t)).T, GROUP_DIM_A, axis=1)
    return jnp.stack([ws_p, ws_s]).astype(BF16), jnp.stack([bs_p, bs_s]).astype(F32)


def kernel(x_prompt, x_sample, cache_k, cache_v, p_prompt, p_sample, ln1_g, ln1_b, ln2_g, ln2_b,
           a_w_in, a_b_in, a_ln_g, a_ln_b, a_w_s, a_b_s, a_w_out, kv_ln_g, kv_ln_b, w_kv,
           b_w_q, b_sinks, b_w_o, router_w, router_b, exp_w_up, exp_b_up, exp_w_dn, exp_b_dn,
           ple_w_proj, ple_w_gate):
    bsz, seq, _ = x_prompt.shape
    n_seq, t, _ = x_sample.shape
    n_p = bsz * seq
    n_s = n_seq * t
    n = n_p + n_s
    depth = ln1_g.shape[0]
    n_a = a_w_in.shape[0]

    xp2 = x_prompt.reshape(n_p, D_MODEL)
    xs2 = x_sample.reshape(n_s, D_MODEL)
    pp2 = p_prompt.reshape(depth * n_p, PLE_DIM)
    ps2 = p_sample.reshape(depth * n_s, PLE_DIM)
    row = lambda v: v.reshape(1, -1).astype(F32)

    xs_buf = jnp.zeros((n * TOP_K + N_EXPERTS * TM_EXPERT, D_MODEL), F32)
    tri = (jnp.arange(TM_ROUTER)[:, None] <= jnp.arange(TM_ROUTER)[None, :]).astype(BF16)
    b_up4 = exp_b_up.reshape(depth, N_EXPERTS, 1, 2 * D_EXPERT)
    b_dn4 = exp_b_dn.reshape(depth, N_EXPERTS, 1, D_MODEL)
    prompt_bias = _prompt_bias()

    def gmlp_weights(i):
        ws2, bs2 = _spatial_tables(a_w_s[i], a_b_s[i], t)
        return (a_w_in[i].astype(BF16), row(a_b_in[i]), row(a_ln_g[i]), row(a_ln_b[i]), ws2, bs2,
                a_w_out[i].astype(BF16), row(ln1_g[i]), row(ln1_b[i]))

    chunk_v = []
    kv_params = (row(kv_ln_g), row(kv_ln_b), w_kv.astype(BF16))
    x = x1_next = None
    kp = vp = k_buf = v_buf = None
    kpad = vpad = kbuf_pad = vbuf_pad = None
    for i in range(depth):
        fuse_next = i + 1 < n_a
        if i < n_a:
            if i == 0:
                x1, v_rows = _gmlp_layer(xp2, xs2, 0, n_p, n_s, gmlp_weights(0))
                chunk_v.append(v_rows.reshape(n_seq, t, D_MODEL))
            else:
                x1 = x1_next
        else:
            j = i - n_a
            wq = b_w_q[j].astype(BF16)
            wo = b_w_o[j].astype(BF16)
            x1_p = _swa_prompt_layer(x, bsz, seq, kpad, vpad, prompt_bias,
                                     b_sinks[j].astype(F32), wq, wo, row(ln1_g[i]), row(ln1_b[i]))
            s_bias, s_sink, s_mask = _sample_tables(t, b_sinks[j])
            x1_s = _swa_sample_layer(x, n_p, n_seq, t, kbuf_pad, vbuf_pad, s_bias, s_sink, s_mask,
                                     _gkd_cols(wq), _gkd_cols(wo.T).T, row(ln1_g[i]), row(ln1_b[i]))
            x1 = jnp.concatenate([x1_p, x1_s], axis=0)
        out, extra, xs_buf = _moe_layer(
            x1, xs_buf, pp2, ps2, n_p, i, router_w[i].T, router_b[i].reshape(N_EXPERTS, 1), tri,
            exp_w_up, b_up4, exp_w_dn, b_dn4, ple_w_gate[i].astype(BF16),
            ple_w_proj[i].astype(BF16), row(ln2_g[i]), row(ln2_b[i]),
            kv_params if i == n_a - 1 else None,
            gmlp_weights(i + 1) if fuse_next else None)
        if fuse_next:
            x1_next = out
            chunk_v.append(extra.reshape(n_seq, t, D_MODEL))
        else:
            x, kv = out, extra
        if i == n_a - 1:
            k_all, v_all = kv[:, :KV_DIM], kv[:, KV_DIM:]
            kp = k_all[:n_p].reshape(bsz, seq, KV_DIM)
            vp = v_all[:n_p].reshape(bsz, seq, KV_DIM)
            k_buf = jnp.concatenate([cache_k.reshape(n_seq, WINDOW, KV_DIM),
                                     k_all[n_p:].reshape(n_seq, t, KV_DIM)], axis=1)
            v_buf = jnp.concatenate([cache_v.reshape(n_seq, WINDOW, KV_DIM),
                                     v_all[n_p:].reshape(n_seq, t, KV_DIM)], axis=1)
            front = ((0, 0), (WINDOW, 0), (0, 0))
            kpad = jnp.pad(kp, front).astype(BF16)
            vpad = jnp.pad(vp, front).astype(BF16)
            tail = ((0, 0), (0, KEY_PAD - WINDOW - t), (0, 0))
            kbuf_pad = jnp.pad(k_buf, tail).astype(BF16)
            vbuf_pad = jnp.pad(v_buf, tail).astype(BF16)

    heads = (N_KV_HEADS, HEAD_DIM)
    return (x[:n_p].reshape(bsz, seq, D_MODEL),
            x[n_p:].reshape(n_seq, t, D_MODEL),
            jnp.stack(chunk_v),
            kp[:, -WINDOW:].reshape(bsz, WINDOW, *heads),
            vp[:, -WINDOW:].reshape(bsz, WINDOW, *heads),
            k_buf[:, -WINDOW:].reshape(n_seq, WINDOW, *heads),
            v_buf[:, -WINDOW:].reshape(n_seq, WINDOW, *heads))
```

```python
import functools

import jax
import jax.numpy as jnp
from jax import lax
from jax.experimental import pallas as pl
from jax.experimental.pallas import tpu as pltpu

F32 = jnp.float32
BF16 = jnp.bfloat16

D_MODEL = 1024
DEPTH = 4
N_A_LAYERS = 2
CHUNK = 128
N_GROUPS_A = 8
GROUP_DIM_A = D_MODEL // N_GROUPS_A
HEAD_DIM = 64
N_HEADS = 16
N_KV_HEADS = 4
Q_PER_KV = 4
KV_DIM = N_KV_HEADS * HEAD_DIM
WINDOW = 128
N_EXPERTS = 32
TOP_K = 4
D_EXPERT = 1024
SWIGLU_LIMIT = 7.0
SWIGLU_ALPHA = 1.702
PLE_DIM = 256
DEEPNORM_ALPHA = (2 * DEPTH) ** 0.25
LN_EPS = 1e-5
NEG_INF = -1e30

TM_TOKEN = 256
TM_DISPATCH = 512
DISPATCH_CHUNK = 128
TM_ROUTER = 512
TM_EXPERT = 512
SAMPLE_SEQ_BLOCK = 8
KEY_PAD = 256
VMEM_LIMIT = 56 * 1024 * 1024


def _ln(x, g, b):
    mu = jnp.mean(x, axis=-1, keepdims=True)
    xc = x - mu
    var = jnp.mean(xc * xc, axis=-1, keepdims=True)
    return xc * lax.rsqrt(var + LN_EPS) * g + b


def _dot(a, b):
    return jnp.dot(a, b, preferred_element_type=F32)


def _dot_nt(a, b):
    return lax.dot_general(a, b, (((1,), (1,)), ((), ())), preferred_element_type=F32)


def _full(shape):
    n = len(shape)
    return pl.BlockSpec(shape, lambda *_: (0,) * n)


def _gmlp_math(x, win_ref, bin_ref, lng_ref, lnb_ref, ws_ref, bs_ref, wout_ref, g1_ref, b1_ref):
    h = _dot(x.astype(BF16), win_ref[...]) + bin_ref[...]
    h = 0.5 * h * (1.0 + lax.erf(h * (2.0 ** -0.5)))
    u = h[:, :D_MODEL]
    v = _ln(h[:, D_MODEL:], lng_ref[...], lnb_ref[...])
    vb = v.astype(BF16)
    rows = []
    for c in range(x.shape[0] // CHUNK):
        cols = []
        for g in range(N_GROUPS_A):
            blk = vb[c * CHUNK:(c + 1) * CHUNK, g * GROUP_DIM_A:(g + 1) * GROUP_DIM_A]
            cols.append(_dot(ws_ref[g], blk))
        rows.append(jnp.concatenate(cols, axis=1) + bs_ref[...])
    s = jnp.concatenate(rows, axis=0)
    gated = (u * s).astype(BF16)
    m = _dot(gated, wout_ref[...])
    return _ln(DEEPNORM_ALPHA * x + m, g1_ref[...], b1_ref[...]), v


def _gmlp_kernel(xa_ref, xb_ref, *refs, p_tiles):
    *w_refs, x1_ref, v_ref = refs
    x = jnp.where(pl.program_id(0) >= p_tiles, xb_ref[...], xa_ref[...])
    x1_ref[...], v_ref[...] = _gmlp_math(x, *w_refs)


def _gmlp_weight_specs(kind):
    return [
        _full((D_MODEL, 2 * D_MODEL)),
        _full((1, 2 * D_MODEL)),
        _full((1, D_MODEL)),
        _full((1, D_MODEL)),
        pl.BlockSpec((None, N_GROUPS_A, CHUNK, CHUNK), lambda i: (kind(i), 0, 0, 0)),
        pl.BlockSpec((None, CHUNK, D_MODEL), lambda i: (kind(i), 0, 0)),
        _full((D_MODEL, D_MODEL)),
        _full((1, D_MODEL)),
        _full((1, D_MODEL)),
    ]


def _gmlp_layer(xa, xb, xb_row0, n_prompt, n_sample, weights, tm=TM_TOKEN):
    n = n_prompt + n_sample
    p_tiles = n_prompt // tm
    n_tiles = n // tm
    s_tiles = n_tiles - p_tiles
    b_tile0 = xb_row0 // tm
    kind = lambda i: jnp.where(i >= p_tiles, 1, 0)
    x1, v = pl.pallas_call(
        functools.partial(_gmlp_kernel, p_tiles=p_tiles),
        grid=(n_tiles,),
        in_specs=[
            pl.BlockSpec((tm, D_MODEL), lambda i: (jnp.minimum(i, p_tiles - 1), 0)),
            pl.BlockSpec((tm, D_MODEL), lambda i: (jnp.maximum(i - p_tiles, 0) + b_tile0, 0)),
        ] + _gmlp_weight_specs(kind),
        out_specs=[
            pl.BlockSpec((tm, D_MODEL), lambda i: (i, 0)),
            pl.BlockSpec((tm, D_MODEL), lambda i: (jnp.maximum(i - (p_tiles - 1), 0), 0)),
        ],
        out_shape=[
            jax.ShapeDtypeStruct((n, D_MODEL), F32),
            jax.ShapeDtypeStruct(((s_tiles + 1) * tm, D_MODEL), F32),
        ],
        compiler_params=pltpu.CompilerParams(
            dimension_semantics=("arbitrary",), vmem_limit_bytes=VMEM_LIMIT),
        name="gmlp_mixer",
    )(xa, xb, *weights)
    return x1, v[tm:]


def _swa_prompt_kernel(sink_ref, x_ref, k_ref, v_ref, bias_ref, wq_ref, wo_ref, g1_ref, b1_ref,
                       x1_ref, *, tq):
    i = pl.program_id(1)
    x = x_ref[...]
    qb = (_dot(x.astype(BF16), wq_ref[...]) * (HEAD_DIM ** -0.5)).astype(BF16)
    blocks = []
    for j in range(tq // WINDOW):
        blk = i * (tq // WINDOW) + j
        kstart = pl.multiple_of(blk * WINDOW, WINDOW)
        kb = k_ref[pl.ds(kstart, 2 * WINDOW), :]
        vb = v_ref[pl.ds(kstart, 2 * WINDOW), :]
        sel = jnp.where(blk == 0, 0, 1)
        outs = []
        for h in range(N_HEADS):
            kh = h // Q_PER_KV
            qh = qb[j * WINDOW:(j + 1) * WINDOW, h * HEAD_DIM:(h + 1) * HEAD_DIM]
            s = _dot_nt(qh, kb[:, kh * HEAD_DIM:(kh + 1) * HEAD_DIM])
            l = s + bias_ref[sel, h]
            sink = sink_ref[h]
            m = jnp.maximum(jnp.max(l, axis=1, keepdims=True), sink)
            p = jnp.exp(l - m)
            den = jnp.sum(p, axis=1, keepdims=True) + jnp.exp(sink - m)
            o = _dot(p.astype(BF16), vb[:, kh * HEAD_DIM:(kh + 1) * HEAD_DIM])
            outs.append(o / den)
        blocks.append(jnp.concatenate(outs, axis=1))
    attn = jnp.concatenate(blocks, axis=0)
    m_out = _dot(attn.astype(BF16), wo_ref[...])
    x1_ref[...] = _ln(DEEPNORM_ALPHA * x + m_out, g1_ref[...], b1_ref[...])


def _swa_prompt_layer(x, bsz, seq, kpad, vpad, bias, sinks, w_q, w_o, g1, b1, tq=TM_TOKEN):
    nq = seq // tq
    return pl.pallas_call(
        functools.partial(_swa_prompt_kernel, tq=tq),
        grid=(bsz, nq),
        in_specs=[
            pl.BlockSpec(memory_space=pltpu.SMEM),
            pl.BlockSpec((tq, D_MODEL), lambda b, i: (b * nq + i, 0)),
            pl.BlockSpec((None, seq + WINDOW, KV_DIM), lambda b, i: (b, 0, 0)),
            pl.BlockSpec((None, seq + WINDOW, KV_DIM), lambda b, i: (b, 0, 0)),
            _full((2, N_HEADS, WINDOW, 2 * WINDOW)),
            _full((D_MODEL, D_MODEL)),
            _full((D_MODEL, D_MODEL)),
            _full((1, D_MODEL)),
            _full((1, D_MODEL)),
        ],
        out_specs=pl.BlockSpec((tq, D_MODEL), lambda b, i: (b * nq + i, 0)),
        out_shape=jax.ShapeDtypeStruct((bsz * seq, D_MODEL), F32),
        compiler_params=pltpu.CompilerParams(
            dimension_semantics=("arbitrary", "arbitrary"), vmem_limit_bytes=VMEM_LIMIT),
        name="swa_prompt",
    )(sinks, x, kpad, vpad, bias, w_q, w_o, g1, b1)


def _swa_sample_kernel(x_ref, k_ref, v_ref, bias_ref, sink_ref, mask_ref, wq_ref, wo_ref,
                       g1_ref, b1_ref, x1_ref, *, sb, t):
    x = x_ref[...]
    q = _dot(x.astype(BF16), wq_ref[...]) * (HEAD_DIM ** -0.5)
    mask = mask_ref[...]
    bias = bias_ref[...]
    sink = sink_ref[...]
    outs = []
    for s in range(sb):
        qs = q[s * t:(s + 1) * t, :]
        parts = []
        for g in range(Q_PER_KV):
            qg = qs[:, g * KV_DIM:(g + 1) * KV_DIM]
            parts.append(jnp.concatenate([qg] * N_KV_HEADS, axis=0) * mask)
        qexp = jnp.concatenate(parts, axis=0).astype(BF16)
        l = _dot_nt(qexp, k_ref[s]) + bias
        m = jnp.maximum(jnp.max(l, axis=1, keepdims=True), sink)
        p = jnp.exp(l - m)
        den = jnp.sum(p, axis=1, keepdims=True) + jnp.exp(sink - m)
        r = _dot(p.astype(BF16), v_ref[s]) / den
        og = []
        for g in range(Q_PER_KV):
            rg = r[g * N_KV_HEADS * t:(g + 1) * N_KV_HEADS * t, :] * mask
            acc = rg[0:t]
            for kh in range(1, N_KV_HEADS):
                acc = acc + rg[kh * t:(kh + 1) * t]
            og.append(acc)
        outs.append(jnp.concatenate(og, axis=1))
    attn = jnp.concatenate(outs, axis=0)
    m_out = _dot(attn.astype(BF16), wo_ref[...])
    x1_ref[...] = _ln(DEEPNORM_ALPHA * x + m_out, g1_ref[...], b1_ref[...])


def _swa_sample_layer(x, row0, n_seq, t, kbuf, vbuf, bias, sink_col, mask, w_q, w_o, g1, b1,
                      sb=SAMPLE_SEQ_BLOCK):
    rows = sb * t
    hr = N_HEADS * t
    blk0 = row0 // rows
    return pl.pallas_call(
        functools.partial(_swa_sample_kernel, sb=sb, t=t),
        grid=(n_seq // sb,),
        in_specs=[
            pl.BlockSpec((rows, D_MODEL), lambda i: (blk0 + i, 0)),
            pl.BlockSpec((sb, KEY_PAD, KV_DIM), lambda i: (i, 0, 0)),
            pl.BlockSpec((sb, KEY_PAD, KV_DIM), lambda i: (i, 0, 0)),
            _full((hr, KEY_PAD)),
            _full((hr, 1)),
            _full((N_KV_HEADS * t, KV_DIM)),
            _full((D_MODEL, D_MODEL)),
            _full((D_MODEL, D_MODEL)),
            _full((1, D_MODEL)),
            _full((1, D_MODEL)),
        ],
        out_specs=pl.BlockSpec((rows, D_MODEL), lambda i: (i, 0)),
        out_shape=jax.ShapeDtypeStruct((n_seq * t, D_MODEL), F32),
        compiler_params=pltpu.CompilerParams(
            dimension_semantics=("arbitrary",), vmem_limit_bytes=VMEM_LIMIT),
        name="swa_sample",
    )(x, kbuf, vbuf, bias, sink_col, mask, w_q, w_o, g1, b1)


def _router_kernel(x_ref, wt_ref, b_ref, tri_ref, idx_ref, gate_ref, rank_ref, cnt_ref, carry,
                   *, tm):
    i = pl.program_id(0)

    @pl.when(i == 0)
    def _():
        carry[...] = jnp.zeros_like(carry)

    logits = lax.dot_general(wt_ref[...], x_ref[...], (((1,), (1,)), ((), ())),
                             precision=lax.Precision.HIGHEST,
                             preferred_element_type=F32) + b_ref[...]
    eidx = lax.broadcasted_iota(jnp.int32, (N_EXPERTS, tm), 0)
    l = logits
    vals, idxs, sels = [], [], []
    for _ in range(TOP_K):
        m = jnp.max(l, axis=0, keepdims=True)
        idx = jnp.min(jnp.where(l == m, eidx, N_EXPERTS), axis=0, keepdims=True)
        sel = eidx == idx
        vals.append(m)
        idxs.append(idx)
        sels.append(sel)
        l = jnp.where(sel, -jnp.inf, l)
    exps = [jnp.exp(v - vals[0]) for v in vals]
    den = exps[0] + exps[1] + exps[2] + exps[3]
    onehot = jnp.zeros((N_EXPERTS, tm), F32)
    for sel in sels:
        onehot = onehot + sel.astype(F32)
    incl = _dot(onehot.astype(BF16), tri_ref[...])
    excl = incl - onehot + carry[...]
    for k in range(TOP_K):
        idx_ref[k:k + 1, :] = idxs[k]
        gate_ref[k:k + 1, :] = exps[k] / den
        rank = jnp.sum(jnp.where(sels[k], excl, 0.0), axis=0, keepdims=True)
        rank_ref[k:k + 1, :] = rank.astype(jnp.int32)
    carry[...] = carry[...] + jnp.sum(onehot, axis=1, keepdims=True)
    cnt_ref[...] = jnp.broadcast_to(carry[...], cnt_ref.shape)


def _route(x1, router_wt, router_b, tri, tm=TM_ROUTER):
    n = x1.shape[0]
    spec_kn = pl.BlockSpec((TOP_K, tm), lambda i: (0, i))
    idx, gate, rank, cnt = pl.pallas_call(
        functools.partial(_router_kernel, tm=tm),
        grid=(n // tm,),
        in_specs=[
            pl.BlockSpec((tm, D_MODEL), lambda i: (i, 0)),
            _full((N_EXPERTS, D_MODEL)),
            _full((N_EXPERTS, 1)),
            _full((tm, tm)),
        ],
        out_specs=[spec_kn, spec_kn, spec_kn, _full((N_EXPERTS, 128))],
        out_shape=[
            jax.ShapeDtypeStruct((TOP_K, n), jnp.int32),
            jax.ShapeDtypeStruct((TOP_K, n), F32),
            jax.ShapeDtypeStruct((TOP_K, n), jnp.int32),
            jax.ShapeDtypeStruct((N_EXPERTS, 128), F32),
        ],
        scratch_shapes=[pltpu.VMEM((N_EXPERTS, 1), F32)],
        compiler_params=pltpu.CompilerParams(
            dimension_semantics=("arbitrary",), vmem_limit_bytes=VMEM_LIMIT),
        name="moe_router",
    )(x1, router_wt, router_b, tri)
    return idx, gate, rank, cnt[:, 0].astype(jnp.int32)


ROW_TILE = 8
LANES = D_MODEL // ROW_TILE


def _to_row_tiles(ref, x):
    rows = x.shape[0]
    for c in range(ROW_TILE):
        ref[pl.ds(c, rows, stride=ROW_TILE), :] = x[:, c * LANES:(c + 1) * LANES]


def _from_row_tiles(ref, rows):
    return jnp.concatenate([ref[pl.ds(c, rows, stride=ROW_TILE), :] for c in range(ROW_TILE)], axis=1)


def _row_tile(ref, p):
    start = p * ROW_TILE
    if not isinstance(p, int):
        start = pl.multiple_of(start, ROW_TILE)
    return ref.at[pl.ds(start, ROW_TILE), :]


def _row_copy_out(x_ref, xs_hbm, sem, t, p):
    return pltpu.make_async_copy(_row_tile(x_ref, t), _row_tile(xs_hbm, p), sem)


def _dispatch_kernel(pos_ref, xin_ref, xs_in_hbm, xs_hbm, x_ref, sem, *, tm, chunk):
    del xs_in_hbm
    _to_row_tiles(x_ref, xin_ref[...])

    def issue(c, carry):
        t0 = c * chunk
        for t in range(chunk):
            for k in range(TOP_K):
                _row_copy_out(x_ref, xs_hbm, sem, t0 + t,
                              pos_ref[0, 0, k * tm + t0 + t]).start(priority=k % 2)
        return carry

    lax.fori_loop(0, tm // chunk, issue, 0)
    for _ in range(TOP_K):
        pltpu.make_async_copy(x_ref, xs_hbm.at[pl.ds(0, ROW_TILE * tm), :], sem).wait()


def _dispatch(x1, pos3, xs_buf, tm=TM_DISPATCH, chunk=DISPATCH_CHUNK):
    n = x1.shape[0]
    return pl.pallas_call(
        functools.partial(_dispatch_kernel, tm=tm, chunk=chunk),
        grid=(n // tm,),
        in_specs=[
            pl.BlockSpec((1, 1, TOP_K * tm), lambda i: (i, 0, 0), memory_space=pltpu.SMEM),
            pl.BlockSpec((tm, D_MODEL), lambda i: (i, 0)),
            pl.BlockSpec(memory_space=pl.ANY),
        ],
        out_specs=pl.BlockSpec(memory_space=pl.ANY),
        out_shape=jax.ShapeDtypeStruct(xs_buf.shape, xs_buf.dtype),
        scratch_shapes=[pltpu.VMEM((ROW_TILE * tm, LANES), F32), pltpu.SemaphoreType.DMA],
        input_output_aliases={2: 0},
        compiler_params=pltpu.CompilerParams(
            dimension_semantics=("arbitrary",), vmem_limit_bytes=VMEM_LIMIT,
            has_side_effects=True),
        name="moe_dispatch",
    )(pos3, x1, xs_buf)


def _expert_kernel(te_ref, nv_ref, xs_ref, wup_ref, bup_ref, wdn_ref, bdn_ref, y_ref):
    del te_ref
    i = pl.program_id(0)

    @pl.when(i < nv_ref[0])
    def _():
        x = _from_row_tiles(xs_ref, y_ref.shape[0] // ROW_TILE)
        h = _dot(x, wup_ref[...]) + bup_ref[...]
        glu = jnp.minimum(h[:, :D_EXPERT], SWIGLU_LIMIT)
        lin = jnp.clip(h[:, D_EXPERT:], -SWIGLU_LIMIT, SWIGLU_LIMIT)
        act = glu * jax.nn.sigmoid(SWIGLU_ALPHA * glu) * (lin + 1.0)
        _to_row_tiles(y_ref, _dot(act, wdn_ref[...]) + bdn_ref[...])

    @pl.when(i >= nv_ref[0])
    def _():
        y_ref[...] = jnp.zeros_like(y_ref)


def _experts(xs, tile_expert, n_valid, w_up, b_up, w_dn, b_dn, layer, tm=TM_EXPERT):
    rows = xs.shape[0] // ROW_TILE
    row_map = lambda i, te, nv: (jnp.minimum(i, nv[0] - 1), 0)
    wspec = lambda shape: pl.BlockSpec((None, None) + shape, lambda i, te, nv: (layer, te[i], 0, 0))
    grid_spec = pltpu.PrefetchScalarGridSpec(
        num_scalar_prefetch=2,
        grid=(rows // tm,),
        in_specs=[
            pl.BlockSpec((ROW_TILE * tm, LANES), row_map),
            wspec((D_MODEL, 2 * D_EXPERT)),
            wspec((1, 2 * D_EXPERT)),
            wspec((D_EXPERT, D_MODEL)),
            wspec((1, D_MODEL)),
        ],
        out_specs=pl.BlockSpec((ROW_TILE * tm, LANES), lambda i, te, nv: (i, 0)),
    )
    return pl.pallas_call(
        _expert_kernel,
        grid_spec=grid_spec,
        out_shape=jax.ShapeDtypeStruct(xs.shape, F32),
        compiler_params=pltpu.CompilerParams(
            dimension_semantics=("arbitrary",), vmem_limit_bytes=VMEM_LIMIT),
        name="moe_experts",
    )(tile_expert, n_valid, xs, w_up, b_up, w_dn, b_dn)


def _row_copy_in(ys_hbm, ybuf, sem, k, t, p):
    return pltpu.make_async_copy(_row_tile(ys_hbm, p), _row_tile(ybuf.at[k], t), sem)


def _combine_kernel(pos_ref, posn_ref, x1_ref, gate_ref, pp_ref, ps_ref, wg_ref, wp_ref, g2_ref,
                    b2_ref, *rest, tm, prompt_steps, emit_kv, fuse_gmlp):
    kv_ref = v_ref = None
    if emit_kv:
        kvg_ref, kvb_ref, wkv_ref, ys_hbm, x3_ref, kv_ref, ybuf_a, ybuf_b, sem = rest
    elif fuse_gmlp:
        *gmlp_refs, ys_hbm, x3_ref, v_ref, ybuf_a, ybuf_b, sem = rest
    else:
        ys_hbm, x3_ref, ybuf_a, ybuf_b, sem = rest
    s = pl.program_id(0)
    is_sample = s >= prompt_steps

    def issue(pref, half, ybuf, sm):
        for t in range(tm):
            for k in range(TOP_K):
                _row_copy_in(ys_hbm, ybuf, sm, k, t,
                             pref[0, 0, (half * TOP_K + k) * tm + t]).start(priority=k % 2)

    def wait_all(ybuf, sm):
        for k in range(TOP_K):
            pltpu.make_async_copy(ys_hbm.at[pl.ds(0, ROW_TILE * tm), :], ybuf.at[k], sm).wait()

    def half_math(half, ybuf):
        rows = slice(half * tm, (half + 1) * tm)
        gate = gate_ref[rows, :]
        f = gate[:, 0:1] * _from_row_tiles(ybuf.at[0], tm)
        for k in range(1, TOP_K):
            f = f + gate[:, k:k + 1] * _from_row_tiles(ybuf.at[k], tm)
        x2 = _ln(DEEPNORM_ALPHA * x1_ref[rows, :] + f, g2_ref[...], b2_ref[...])
        p = jnp.where(is_sample, ps_ref[rows, :], pp_ref[rows, :])
        pg = jax.nn.sigmoid(_dot(x2.astype(BF16), wg_ref[...]))
        pp = _dot(p.astype(BF16), wp_ref[...])
        x3 = x2 + pg * pp
        if fuse_gmlp:
            x3_ref[rows, :], v_ref[rows, :] = _gmlp_math(x3, *gmlp_refs)
        else:
            x3_ref[rows, :] = x3
        if emit_kv:
            kv_ref[rows, :] = _dot(_ln(x3, kvg_ref[...], kvb_ref[...]).astype(BF16), wkv_ref[...])

    @pl.when(s == 0)
    def _():
        def first(t, c):
            for k in range(TOP_K):
                _row_copy_in(ys_hbm, ybuf_a, sem.at[0], k, t, pos_ref[0, 0, k * tm + t]).start()
            return c

        lax.fori_loop(0, tm, first, 0)

    wait_all(ybuf_a, sem.at[0])
    issue(pos_ref, 1, ybuf_b, sem.at[1])
    half_math(0, ybuf_a)
    wait_all(ybuf_b, sem.at[1])
    issue(posn_ref, 0, ybuf_a, sem.at[0])
    half_math(1, ybuf_b)

    @pl.when(s == pl.num_programs(0) - 1)
    def _():
        wait_all(ybuf_a, sem.at[0])


def _combine(x1, pos3, gate_t, ys, p_prompt, p_sample, layer, n_prompt, w_gate, w_proj, g2, b2,
             kv_params=None, gmlp_weights=None, tm=TM_TOKEN):
    n = x1.shape[0]
    emit_kv = kv_params is not None
    fuse_gmlp = gmlp_weights is not None
    assert not (emit_kv and fuse_gmlp)
    tm2 = 2 * tm
    steps = n // tm2
    prompt_steps = n_prompt // tm2
    sample_steps = steps - prompt_steps
    row = lambda w: pl.BlockSpec((tm2, w), lambda i: (i, 0))
    smem_pos = lambda f: pl.BlockSpec((1, 1, 2 * TOP_K * tm), lambda i: (f(i), 0, 0),
                                      memory_space=pltpu.SMEM)
    in_specs = [
        smem_pos(lambda i: i),
        smem_pos(lambda i: jnp.minimum(i + 1, steps - 1)),
        row(D_MODEL),
        row(TOP_K),
        pl.BlockSpec((tm2, PLE_DIM),
                     lambda i: (layer * prompt_steps + jnp.minimum(i, prompt_steps - 1), 0)),
        pl.BlockSpec((tm2, PLE_DIM),
                     lambda i: (layer * sample_steps + jnp.maximum(i - prompt_steps, 0), 0)),
        _full((D_MODEL, D_MODEL)),
        _full((PLE_DIM, D_MODEL)),
        _full((1, D_MODEL)),
        _full((1, D_MODEL)),
    ]
    args = [pos3, pos3, x1, gate_t, p_prompt, p_sample, w_gate, w_proj, g2, b2]
    out_specs = [row(D_MODEL)]
    out_shape = [jax.ShapeDtypeStruct((n, D_MODEL), F32)]
    if emit_kv:
        in_specs += [_full((1, D_MODEL)), _full((1, D_MODEL)), _full((D_MODEL, 2 * KV_DIM))]
        args += list(kv_params)
        out_specs.append(row(2 * KV_DIM))
        out_shape.append(jax.ShapeDtypeStruct((n, 2 * KV_DIM), F32))
    if fuse_gmlp:
        in_specs += _gmlp_weight_specs(lambda i: jnp.where(i >= prompt_steps, 1, 0))
        args += list(gmlp_weights)
        out_specs.append(pl.BlockSpec((tm2, D_MODEL),
                                      lambda i: (jnp.maximum(i - (prompt_steps - 1), 0), 0)))
        out_shape.append(jax.ShapeDtypeStruct(((sample_steps + 1) * tm2, D_MODEL), F32))
    in_specs.append(pl.BlockSpec(memory_space=pl.ANY))
    args.append(ys)
    outs = pl.pallas_call(
        functools.partial(_combine_kernel, tm=tm, prompt_steps=prompt_steps, emit_kv=emit_kv,
                          fuse_gmlp=fuse_gmlp),
        grid=(steps,),
        in_specs=in_specs,
        out_specs=out_specs,
        out_shape=out_shape,
        scratch_shapes=[pltpu.VMEM((TOP_K, ROW_TILE * tm, LANES), F32),
                        pltpu.VMEM((TOP_K, ROW_TILE * tm, LANES), F32),
                        pltpu.SemaphoreType.DMA((2,))],
        compiler_params=pltpu.CompilerParams(
            dimension_semantics=("arbitrary",), vmem_limit_bytes=VMEM_LIMIT),
        name="moe_combine",
    )(*args)
    if fuse_gmlp:
        return outs[0], outs[1][tm2:]
    return outs if emit_kv else (outs[0], None)


def _tile_pos(pos, tm):
    n = pos.shape[1]
    return pos.reshape(TOP_K, n // tm, tm).transpose(1, 0, 2).reshape(n // tm, 1, TOP_K * tm)


def _pair_pos(pos, tm):
    n = pos.shape[1]
    p = pos.reshape(TOP_K, n // (2 * tm), 2, tm).transpose(1, 2, 0, 3)
    return p.reshape(n // (2 * tm), 1, 2 * TOP_K * tm)


def _moe_layer(x1, xs_buf, p_prompt, p_sample, n_prompt, layer, router_wt, router_b, tri,
               w_up, b_up, w_dn, b_dn, w_gate, w_proj, g2, b2, kv_params=None, gmlp_weights=None):
    idx, gate, rank, counts = _route(x1, router_wt, router_b, tri)
    padded = ((counts + TM_EXPERT - 1) // TM_EXPERT) * TM_EXPERT
    ends = jnp.cumsum(padded)
    starts = ends - padded
    onehot = idx[:, :, None] == jnp.arange(N_EXPERTS, dtype=jnp.int32)
    pos = jnp.sum(jnp.where(onehot, starts, 0), axis=-1) + rank
    n_tiles = xs_buf.shape[0] // (ROW_TILE * TM_EXPERT)
    n_valid = (ends[-1] // TM_EXPERT).astype(jnp.int32)
    tile_start = jnp.minimum(jnp.arange(n_tiles, dtype=jnp.int32), n_valid - 1) * TM_EXPERT
    tile_expert = jnp.sum(tile_start[:, None] >= ends[None, :], axis=1).astype(jnp.int32)
    xs_buf = _dispatch(x1, _tile_pos(pos, TM_DISPATCH), xs_buf)
    ys = _experts(xs_buf, tile_expert, n_valid.reshape(1), w_up, b_up, w_dn, b_dn, layer)
    x3, extra = _combine(x1, _pair_pos(pos, TM_TOKEN), gate.T, ys, p_prompt, p_sample, layer, n_prompt,
                         w_gate, w_proj, g2, b2, kv_params, gmlp_weights)
    return x3, extra, xs_buf


def _alibi_slopes():
    h = jnp.arange(1, N_HEADS + 1, dtype=F32)
    return 2.0 ** (-8.0 * h / N_HEADS)


def _prompt_bias():
    qi = jnp.arange(WINDOW)[:, None]
    kj = jnp.arange(2 * WINDOW)[None, :]
    dist = qi + WINDOW - kj
    valid = (dist >= 0) & (dist < WINDOW)
    slopes = _alibi_slopes()[:, None, None]
    tables = []
    for first in (True, False):
        ok = valid & (kj >= WINDOW) if first else valid
        tables.append(jnp.where(ok[None], -slopes * dist.astype(F32)[None], NEG_INF))
    return jnp.stack(tables).astype(F32)


def _sample_tables(t, sinks):
    g = jnp.arange(Q_PER_KV)[:, None, None]
    kh = jnp.arange(N_KV_HEADS)[None, :, None]
    tok = jnp.arange(t)[None, None, :]
    head = jnp.broadcast_to(kh * Q_PER_KV + g, (Q_PER_KV, N_KV_HEADS, t)).reshape(-1)
    tok = jnp.broadcast_to(tok, (Q_PER_KV, N_KV_HEADS, t)).reshape(-1)
    kj = jnp.arange(KEY_PAD)[None, :]
    dist = tok[:, None] + WINDOW - kj
    valid = (dist >= 0) & (dist < WINDOW) & (kj < WINDOW + t)
    slopes = _alibi_slopes()[head][:, None]
    bias = jnp.where(valid, -slopes * dist.astype(F32), NEG_INF).astype(F32)
    sink_col = sinks[head][:, None].astype(F32)
    row_kh = jnp.repeat(jnp.arange(N_KV_HEADS), t)[:, None]
    lane_kh = (jnp.arange(KV_DIM) // HEAD_DIM)[None, :]
    mask = (row_kh == lane_kh).astype(F32)
    return bias, sink_col, mask


def _gkd_cols(w):
    r = w.shape[0]
    w4 = w.reshape(r, N_KV_HEADS, Q_PER_KV, HEAD_DIM)
    return w4.transpose(0, 2, 1, 3).reshape(r, N_HEADS * HEAD_DIM)


def _spatial_tables(w_s, b_s, t):
    tril = jnp.tril(jnp.ones((CHUNK, CHUNK), dtype=bool))
    ws_p = jnp.where(tril[None], w_s, 0.0)
    bs_p = jnp.repeat(b_s.T, GROUP_DIM_A, axis=1)
    r = jnp.arange(CHUNK)
    same = (r[:, None] // t) == (r[None, :] // t)
    small = jnp.where(tril[None, :t, :t], w_s[:, :t, :t], 0.0)
    ws_s = jnp.where(same[None], jnp.tile(small, (1, CHUNK // t, CHUNK // t)), 0.0)
    bs_s = jnp.repeat(jnp.tile(b_s[:, :t], (1, CHUNK // t)).T, GROUP_DIM_A, axis=1)
    return jnp.stack([ws_p, ws_s]).astype(BF16), jnp.stack([bs_p, bs_s]).astype(F32)


def kernel(x_prompt, x_sample, cache_k, cache_v, p_prompt, p_sample, ln1_g, ln1_b, ln2_g, ln2_b,
           a_w_in, a_b_in, a_ln_g, a_ln_b, a_w_s, a_b_s, a_w_out, kv_ln_g, kv_ln_b, w_kv,
           b_w_q, b_sinks, b_w_o, router_w, router_b, exp_w_up, exp_b_up, exp_w_dn, exp_b_dn,
           ple_w_proj, ple_w_gate):
    bsz, seq, _ = x_prompt.shape
    n_seq, t, _ = x_sample.shape
    n_p = bsz * seq
    n_s = n_seq * t
    n = n_p + n_s
    depth = ln1_g.shape[0]
    n_a = a_w_in.shape[0]

    xp2 = x_prompt.reshape(n_p, D_MODEL)
    xs2 = x_sample.reshape(n_s, D_MODEL)
    pp2 = p_prompt.reshape(depth * n_p, PLE_DIM)
    ps2 = p_sample.reshape(depth * n_s, PLE_DIM)
    row = lambda v: v.reshape(1, -1).astype(F32)

    xs_buf = jnp.zeros((ROW_TILE * (n * TOP_K + N_EXPERTS * TM_EXPERT), LANES), F32)
    tri = (jnp.arange(TM_ROUTER)[:, None] <= jnp.arange(TM_ROUTER)[None, :]).astype(BF16)
    b_up4 = exp_b_up.reshape(depth, N_EXPERTS, 1, 2 * D_EXPERT)
    b_dn4 = exp_b_dn.reshape(depth, N_EXPERTS, 1, D_MODEL)
    prompt_bias = _prompt_bias()

    def gmlp_weights(i):
        ws2, bs2 = _spatial_tables(a_w_s[i], a_b_s[i], t)
        return (a_w_in[i].astype(BF16), row(a_b_in[i]), row(a_ln_g[i]), row(a_ln_b[i]), ws2, bs2,
                a_w_out[i].astype(BF16), row(ln1_g[i]), row(ln1_b[i]))

    chunk_v = []
    kv_params = (row(kv_ln_g), row(kv_ln_b), w_kv.astype(BF16))
    x = x1_next = None
    kp = vp = k_buf = v_buf = None
    kpad = vpad = kbuf_pad = vbuf_pad = None
    for i in range(depth):
        fuse_next = i + 1 < n_a
        if i < n_a:
            if i == 0:
                x1, v_rows = _gmlp_layer(xp2, xs2, 0, n_p, n_s, gmlp_weights(0))
                chunk_v.append(v_rows.reshape(n_seq, t, D_MODEL))
            else:
                x1 = x1_next
        else:
            j = i - n_a
            wq = b_w_q[j].astype(BF16)
            wo = b_w_o[j].astype(BF16)
            x1_p = _swa_prompt_layer(x, bsz, seq, kpad, vpad, prompt_bias,
                                     b_sinks[j].astype(F32), wq, wo, row(ln1_g[i]), row(ln1_b[i]))
            s_bias, s_sink, s_mask = _sample_tables(t, b_sinks[j])
            x1_s = _swa_sample_layer(x, n_p, n_seq, t, kbuf_pad, vbuf_pad, s_bias, s_sink, s_mask,
                                     _gkd_cols(wq), _gkd_cols(wo.T).T, row(ln1_g[i]), row(ln1_b[i]))
            x1 = jnp.concatenate([x1_p, x1_s], axis=0)
        out, extra, xs_buf = _moe_layer(
            x1, xs_buf, pp2, ps2, n_p, i, router_w[i].T, router_b[i].reshape(N_EXPERTS, 1), tri,
            exp_w_up, b_up4, exp_w_dn, b_dn4, ple_w_gate[i].astype(BF16),
            ple_w_proj[i].astype(BF16), row(ln2_g[i]), row(ln2_b[i]),
            kv_params if i == n_a - 1 else None,
            gmlp_weights(i + 1) if fuse_next else None)
        if fuse_next:
            x1_next = out
            chunk_v.append(extra.reshape(n_seq, t, D_MODEL))
        else:
            x, kv = out, extra
        if i == n_a - 1:
            k_all, v_all = kv[:, :KV_DIM], kv[:, KV_DIM:]
            kp = k_all[:n_p].reshape(bsz, seq, KV_DIM)
            vp = v_all[:n_p].reshape(bsz, seq, KV_DIM)
            k_buf = jnp.concatenate([cache_k.reshape(n_seq, WINDOW, KV_DIM),
                                     k_all[n_p:].reshape(n_seq, t, KV_DIM)], axis=1)
            v_buf = jnp.concatenate([cache_v.reshape(n_seq, WINDOW, KV_DIM),
                                     v_all[n_p:].reshape(n_seq, t, KV_DIM)], axis=1)
            front = ((0, 0), (WINDOW, 0), (0, 0))
            kpad = jnp.pad(kp, front).astype(BF16)
            vpad = jnp.pad(vp, front).astype(BF16)
            tail = ((0, 0), (0, KEY_PAD - WINDOW - t), (0, 0))
            kbuf_pad = jnp.pad(k_buf, tail).astype(BF16)
            vbuf_pad = jnp.pad(v_buf, tail).astype(BF16)

    heads = (N_KV_HEADS, HEAD_DIM)
    return (x[:n_p].reshape(bsz, seq, D_MODEL),
            x[n_p:].reshape(n_seq, t, D_MODEL),
            jnp.stack(chunk_v),
            kp[:, -WINDOW:].reshape(bsz, WINDOW, *heads),
            vp[:, -WINDOW:].reshape(bsz, WINDOW, *heads),
            k_buf[:, -WINDOW:].reshape(n_seq, WINDOW, *heads),
            v_buf[:, -WINDOW:].reshape(n_seq, WINDOW, *heads))
```

```python
import functools

import jax
import jax.numpy as jnp
from jax import lax
from jax.experimental import pallas as pl
from jax.experimental.pallas import tpu as pltpu

F32 = jnp.float32
BF16 = jnp.bfloat16

D_MODEL = 1024
DEPTH = 4
N_A_LAYERS = 2
CHUNK = 128
N_GROUPS_A = 8
GROUP_DIM_A = D_MODEL // N_GROUPS_A
HEAD_DIM = 64
N_HEADS = 16
N_KV_HEADS = 4
Q_PER_KV = 4
KV_DIM = N_KV_HEADS * HEAD_DIM
WINDOW = 128
N_EXPERTS = 32
TOP_K = 4
D_EXPERT = 1024
SWIGLU_LIMIT = 7.0
SWIGLU_ALPHA = 1.702
PLE_DIM = 256
DEEPNORM_ALPHA = (2 * DEPTH) ** 0.25
LN_EPS = 1e-5
NEG_INF = -1e30

TM_TOKEN = 256
TM_DISPATCH = 512
DISPATCH_CHUNK = 128
TM_ROUTER = 512
TM_EXPERT = 512
SAMPLE_SEQ_BLOCK = 8
KEY_PAD = 256
VMEM_LIMIT = 56 * 1024 * 1024


def _ln(x, g, b):
    mu = jnp.mean(x, axis=-1, keepdims=True)
    xc = x - mu
    var = jnp.mean(xc * xc, axis=-1, keepdims=True)
    return xc * lax.rsqrt(var + LN_EPS) * g + b


def _dot(a, b):
    return jnp.dot(a, b, preferred_element_type=F32)


def _dot_nt(a, b):
    return lax.dot_general(a, b, (((1,), (1,)), ((), ())), preferred_element_type=F32)


def _full(shape):
    n = len(shape)
    return pl.BlockSpec(shape, lambda *_: (0,) * n)


def _gmlp_math(x, win_ref, bin_ref, lng_ref, lnb_ref, ws_ref, bs_ref, wout_ref, g1_ref, b1_ref):
    h = _dot(x.astype(BF16), win_ref[...]) + bin_ref[...]
    h = 0.5 * h * (1.0 + lax.erf(h * (2.0 ** -0.5)))
    u = h[:, :D_MODEL]
    v = _ln(h[:, D_MODEL:], lng_ref[...], lnb_ref[...])
    vb = v.astype(BF16)
    rows = []
    for c in range(x.shape[0] // CHUNK):
        cols = []
        for g in range(N_GROUPS_A):
            blk = vb[c * CHUNK:(c + 1) * CHUNK, g * GROUP_DIM_A:(g + 1) * GROUP_DIM_A]
            cols.append(_dot(ws_ref[g], blk))
        rows.append(jnp.concatenate(cols, axis=1) + bs_ref[...])
    s = jnp.concatenate(rows, axis=0)
    gated = (u * s).astype(BF16)
    m = _dot(gated, wout_ref[...])
    return _ln(DEEPNORM_ALPHA * x + m, g1_ref[...], b1_ref[...]), v


def _gmlp_kernel(xa_ref, xb_ref, *refs, p_tiles):
    *w_refs, x1_ref, v_ref = refs
    x = jnp.where(pl.program_id(0) >= p_tiles, xb_ref[...], xa_ref[...])
    x1_ref[...], v_ref[...] = _gmlp_math(x, *w_refs)


def _gmlp_weight_specs(kind):
    return [
        _full((D_MODEL, 2 * D_MODEL)),
        _full((1, 2 * D_MODEL)),
        _full((1, D_MODEL)),
        _full((1, D_MODEL)),
        pl.BlockSpec((None, N_GROUPS_A, CHUNK, CHUNK), lambda i: (kind(i), 0, 0, 0)),
        pl.BlockSpec((None, CHUNK, D_MODEL), lambda i: (kind(i), 0, 0)),
        _full((D_MODEL, D_MODEL)),
        _full((1, D_MODEL)),
        _full((1, D_MODEL)),
    ]


def _gmlp_layer(xa, xb, xb_row0, n_prompt, n_sample, weights, tm=TM_TOKEN):
    n = n_prompt + n_sample
    p_tiles = n_prompt // tm
    n_tiles = n // tm
    s_tiles = n_tiles - p_tiles
    b_tile0 = xb_row0 // tm
    kind = lambda i: jnp.where(i >= p_tiles, 1, 0)
    x1, v = pl.pallas_call(
        functools.partial(_gmlp_kernel, p_tiles=p_tiles),
        grid=(n_tiles,),
        in_specs=[
            pl.BlockSpec((tm, D_MODEL), lambda i: (jnp.minimum(i, p_tiles - 1), 0)),
            pl.BlockSpec((tm, D_MODEL), lambda i: (jnp.maximum(i - p_tiles, 0) + b_tile0, 0)),
        ] + _gmlp_weight_specs(kind),
        out_specs=[
            pl.BlockSpec((tm, D_MODEL), lambda i: (i, 0)),
            pl.BlockSpec((tm, D_MODEL), lambda i: (jnp.maximum(i - (p_tiles - 1), 0), 0)),
        ],
        out_shape=[
            jax.ShapeDtypeStruct((n, D_MODEL), F32),
            jax.ShapeDtypeStruct(((s_tiles + 1) * tm, D_MODEL), F32),
        ],
        compiler_params=pltpu.CompilerParams(
            dimension_semantics=("arbitrary",), vmem_limit_bytes=VMEM_LIMIT),
        name="gmlp_mixer",
    )(xa, xb, *weights)
    return x1, v[tm:]


def _swa_prompt_kernel(sink_ref, x_ref, k_ref, v_ref, bias_ref, wq_ref, wo_ref, g1_ref, b1_ref,
                       x1_ref, *, tq):
    i = pl.program_id(1)
    x = x_ref[...]
    qb = (_dot(x.astype(BF16), wq_ref[...]) * (HEAD_DIM ** -0.5)).astype(BF16)
    blocks = []
    for j in range(tq // WINDOW):
        blk = i * (tq // WINDOW) + j
        kstart = pl.multiple_of(blk * WINDOW, WINDOW)
        kb = k_ref[pl.ds(kstart, 2 * WINDOW), :]
        vb = v_ref[pl.ds(kstart, 2 * WINDOW), :]
        sel = jnp.where(blk == 0, 0, 1)
        outs = []
        for h in range(N_HEADS):
            kh = h // Q_PER_KV
            qh = qb[j * WINDOW:(j + 1) * WINDOW, h * HEAD_DIM:(h + 1) * HEAD_DIM]
            s = _dot_nt(qh, kb[:, kh * HEAD_DIM:(kh + 1) * HEAD_DIM])
            l = s + bias_ref[sel, h]
            sink = sink_ref[h]
            m = jnp.maximum(jnp.max(l, axis=1, keepdims=True), sink)
            p = jnp.exp(l - m)
            den = jnp.sum(p, axis=1, keepdims=True) + jnp.exp(sink - m)
            o = _dot(p.astype(BF16), vb[:, kh * HEAD_DIM:(kh + 1) * HEAD_DIM])
            outs.append(o / den)
        blocks.append(jnp.concatenate(outs, axis=1))
    attn = jnp.concatenate(blocks, axis=0)
    m_out = _dot(attn.astype(BF16), wo_ref[...])
    x1_ref[...] = _ln(DEEPNORM_ALPHA * x + m_out, g1_ref[...], b1_ref[...])


def _swa_prompt_layer(x, bsz, seq, kpad, vpad, bias, sinks, w_q, w_o, g1, b1, tq=TM_TOKEN):
    nq = seq // tq
    return pl.pallas_call(
        functools.partial(_swa_prompt_kernel, tq=tq),
        grid=(bsz, nq),
        in_specs=[
            pl.BlockSpec(memory_space=pltpu.SMEM),
            pl.BlockSpec((tq, D_MODEL), lambda b, i: (b * nq + i, 0)),
            pl.BlockSpec((None, seq + WINDOW, KV_DIM), lambda b, i: (b, 0, 0)),
            pl.BlockSpec((None, seq + WINDOW, KV_DIM), lambda b, i: (b, 0, 0)),
            _full((2, N_HEADS, WINDOW, 2 * WINDOW)),
            _full((D_MODEL, D_MODEL)),
            _full((D_MODEL, D_MODEL)),
            _full((1, D_MODEL)),
            _full((1, D_MODEL)),
        ],
        out_specs=pl.BlockSpec((tq, D_MODEL), lambda b, i: (b * nq + i, 0)),
        out_shape=jax.ShapeDtypeStruct((bsz * seq, D_MODEL), F32),
        compiler_params=pltpu.CompilerParams(
            dimension_semantics=("arbitrary", "arbitrary"), vmem_limit_bytes=VMEM_LIMIT),
        name="swa_prompt",
    )(sinks, x, kpad, vpad, bias, w_q, w_o, g1, b1)


def _swa_sample_kernel(x_ref, k_ref, v_ref, bias_ref, sink_ref, mask_ref, wq_ref, wo_ref,
                       g1_ref, b1_ref, x1_ref, *, sb, t):
    x = x_ref[...]
    q = _dot(x.astype(BF16), wq_ref[...]) * (HEAD_DIM ** -0.5)
    mask = mask_ref[...]
    bias = bias_ref[...]
    sink = sink_ref[...]
    outs = []
    for s in range(sb):
        qs = q[s * t:(s + 1) * t, :]
        parts = []
        for g in range(Q_PER_KV):
            qg = qs[:, g * KV_DIM:(g + 1) * KV_DIM]
            parts.append(jnp.concatenate([qg] * N_KV_HEADS, axis=0) * mask)
        qexp = jnp.concatenate(parts, axis=0).astype(BF16)
        l = _dot_nt(qexp, k_ref[s]) + bias
        m = jnp.maximum(jnp.max(l, axis=1, keepdims=True), sink)
        p = jnp.exp(l - m)
        den = jnp.sum(p, axis=1, keepdims=True) + jnp.exp(sink - m)
        r = _dot(p.astype(BF16), v_ref[s]) / den
        og = []
        for g in range(Q_PER_KV):
            rg = r[g * N_KV_HEADS * t:(g + 1) * N_KV_HEADS * t, :] * mask
            acc = rg[0:t]
            for kh in range(1, N_KV_HEADS):
                acc = acc + rg[kh * t:(kh + 1) * t]
            og.append(acc)
        outs.append(jnp.concatenate(og, axis=1))
    attn = jnp.concatenate(outs, axis=0)
    m_out = _dot(attn.astype(BF16), wo_ref[...])
    x1_ref[...] = _ln(DEEPNORM_ALPHA * x + m_out, g1_ref[...], b1_ref[...])


def _swa_sample_layer(x, row0, n_seq, t, kbuf, vbuf, bias, sink_col, mask, w_q, w_o, g1, b1,
                      sb=SAMPLE_SEQ_BLOCK):
    rows = sb * t
    hr = N_HEADS * t
    blk0 = row0 // rows
    return pl.pallas_call(
        functools.partial(_swa_sample_kernel, sb=sb, t=t),
        grid=(n_seq // sb,),
        in_specs=[
            pl.BlockSpec((rows, D_MODEL), lambda i: (blk0 + i, 0)),
            pl.BlockSpec((sb, KEY_PAD, KV_DIM), lambda i: (i, 0, 0)),
            pl.BlockSpec((sb, KEY_PAD, KV_DIM), lambda i: (i, 0, 0)),
            _full((hr, KEY_PAD)),
            _full((hr, 1)),
            _full((N_KV_HEADS * t, KV_DIM)),
            _full((D_MODEL, D_MODEL)),
            _full((D_MODEL, D_MODEL)),
            _full((1, D_MODEL)),
            _full((1, D_MODEL)),
        ],
        out_specs=pl.BlockSpec((rows, D_MODEL), lambda i: (i, 0)),
        out_shape=jax.ShapeDtypeStruct((n_seq * t, D_MODEL), F32),
        compiler_params=pltpu.CompilerParams(
            dimension_semantics=("arbitrary",), vmem_limit_bytes=VMEM_LIMIT),
        name="swa_sample",
    )(x, kbuf, vbuf, bias, sink_col, mask, w_q, w_o, g1, b1)


def _token_specs(tm, n_prompt, b_row0):
    p_steps = n_prompt // tm
    b0 = b_row0 // tm
    return [pl.BlockSpec((tm, D_MODEL), lambda i: (jnp.minimum(i, p_steps - 1), 0)),
            pl.BlockSpec((tm, D_MODEL), lambda i: (jnp.maximum(i - p_steps, 0) + b0, 0))]


def _router_kernel(xa_ref, xb_ref, wt_ref, b_ref, tri_ref, idx_ref, gate_ref, rank_ref, cnt_ref,
                   carry, *, tm, p_steps):
    i = pl.program_id(0)

    @pl.when(i == 0)
    def _():
        carry[...] = jnp.zeros_like(carry)

    x = jnp.where(i >= p_steps, xb_ref[...], xa_ref[...])
    logits = lax.dot_general(wt_ref[...], x, (((1,), (1,)), ((), ())),
                             precision=lax.Precision.HIGHEST,
                             preferred_element_type=F32) + b_ref[...]
    eidx = lax.broadcasted_iota(jnp.int32, (N_EXPERTS, tm), 0)
    l = logits
    vals, idxs, sels = [], [], []
    for _ in range(TOP_K):
        m = jnp.max(l, axis=0, keepdims=True)
        idx = jnp.min(jnp.where(l == m, eidx, N_EXPERTS), axis=0, keepdims=True)
        sel = eidx == idx
        vals.append(m)
        idxs.append(idx)
        sels.append(sel)
        l = jnp.where(sel, -jnp.inf, l)
    exps = [jnp.exp(v - vals[0]) for v in vals]
    den = exps[0] + exps[1] + exps[2] + exps[3]
    onehot = jnp.zeros((N_EXPERTS, tm), F32)
    for sel in sels:
        onehot = onehot + sel.astype(F32)
    incl = _dot(onehot.astype(BF16), tri_ref[...])
    excl = incl - onehot + carry[...]
    for k in range(TOP_K):
        idx_ref[k:k + 1, :] = idxs[k]
        gate_ref[k:k + 1, :] = exps[k] / den
        rank = jnp.sum(jnp.where(sels[k], excl, 0.0), axis=0, keepdims=True)
        rank_ref[k:k + 1, :] = rank.astype(jnp.int32)
    carry[...] = carry[...] + jnp.sum(onehot, axis=1, keepdims=True)
    cnt_ref[...] = jnp.broadcast_to(carry[...], cnt_ref.shape)


def _route(src, n, n_prompt, router_wt, router_b, tri, tm=TM_ROUTER):
    xa, xb, b_row0 = src
    spec_kn = pl.BlockSpec((TOP_K, tm), lambda i: (0, i))
    idx, gate, rank, cnt = pl.pallas_call(
        functools.partial(_router_kernel, tm=tm, p_steps=n_prompt // tm),
        grid=(n // tm,),
        in_specs=_token_specs(tm, n_prompt, b_row0) + [
            _full((N_EXPERTS, D_MODEL)),
            _full((N_EXPERTS, 1)),
            _full((tm, tm)),
        ],
        out_specs=[spec_kn, spec_kn, spec_kn, _full((N_EXPERTS, 128))],
        out_shape=[
            jax.ShapeDtypeStruct((TOP_K, n), jnp.int32),
            jax.ShapeDtypeStruct((TOP_K, n), F32),
            jax.ShapeDtypeStruct((TOP_K, n), jnp.int32),
            jax.ShapeDtypeStruct((N_EXPERTS, 128), F32),
        ],
        scratch_shapes=[pltpu.VMEM((N_EXPERTS, 1), F32)],
        compiler_params=pltpu.CompilerParams(
            dimension_semantics=("arbitrary",), vmem_limit_bytes=VMEM_LIMIT),
        name="moe_router",
    )(xa, xb, router_wt, router_b, tri)
    return idx, gate, rank, cnt[:, 0].astype(jnp.int32)


ROW_TILE = 8
LANES = D_MODEL // ROW_TILE


def _to_row_tiles(ref, x):
    rows = x.shape[0]
    for c in range(ROW_TILE):
        ref[pl.ds(c, rows, stride=ROW_TILE), :] = x[:, c * LANES:(c + 1) * LANES]


def _from_row_tiles(ref, rows):
    return jnp.concatenate([ref[pl.ds(c, rows, stride=ROW_TILE), :] for c in range(ROW_TILE)], axis=1)


def _row_tile(ref, p):
    start = p * ROW_TILE
    if not isinstance(p, int):
        start = pl.multiple_of(start, ROW_TILE)
    return ref.at[pl.ds(start, ROW_TILE), :]


def _row_copy_out(x_ref, xs_hbm, sem, t, p):
    return pltpu.make_async_copy(_row_tile(x_ref, t), _row_tile(xs_hbm, p), sem)


def _dispatch_kernel(pos_ref, xa_ref, xb_ref, xs_in_hbm, xs_hbm, x_ref, sem, *, tm, chunk, p_steps):
    del xs_in_hbm
    _to_row_tiles(x_ref, jnp.where(pl.program_id(0) >= p_steps, xb_ref[...], xa_ref[...]))

    def issue(c, carry):
        t0 = c * chunk
        for t in range(chunk):
            for k in range(TOP_K):
                _row_copy_out(x_ref, xs_hbm, sem, t0 + t,
                              pos_ref[0, 0, k * tm + t0 + t]).start(priority=k % 2)
        return carry

    lax.fori_loop(0, tm // chunk, issue, 0)
    for _ in range(TOP_K):
        pltpu.make_async_copy(x_ref, xs_hbm.at[pl.ds(0, ROW_TILE * tm), :], sem).wait()


def _dispatch(src, n, n_prompt, pos3, xs_buf, tm=TM_DISPATCH, chunk=DISPATCH_CHUNK):
    xa, xb, b_row0 = src
    return pl.pallas_call(
        functools.partial(_dispatch_kernel, tm=tm, chunk=chunk, p_steps=n_prompt // tm),
        grid=(n // tm,),
        in_specs=[
            pl.BlockSpec((1, 1, TOP_K * tm), lambda i: (i, 0, 0), memory_space=pltpu.SMEM),
        ] + _token_specs(tm, n_prompt, b_row0) + [
            pl.BlockSpec(memory_space=pl.ANY),
        ],
        out_specs=pl.BlockSpec(memory_space=pl.ANY),
        out_shape=jax.ShapeDtypeStruct(xs_buf.shape, xs_buf.dtype),
        scratch_shapes=[pltpu.VMEM((ROW_TILE * tm, LANES), F32), pltpu.SemaphoreType.DMA],
        input_output_aliases={3: 0},
        compiler_params=pltpu.CompilerParams(
            dimension_semantics=("arbitrary",), vmem_limit_bytes=VMEM_LIMIT),
        name="moe_dispatch",
    )(pos3, xa, xb, xs_buf)


def _expert_kernel(te_ref, nv_ref, xs_ref, wup_ref, bup_ref, wdn_ref, bdn_ref, y_ref):
    del te_ref
    i = pl.program_id(0)

    @pl.when(i < nv_ref[0])
    def _():
        x = _from_row_tiles(xs_ref, y_ref.shape[0] // ROW_TILE)
        h = _dot(x, wup_ref[...]) + bup_ref[...]
        glu = jnp.minimum(h[:, :D_EXPERT], SWIGLU_LIMIT)
        lin = jnp.clip(h[:, D_EXPERT:], -SWIGLU_LIMIT, SWIGLU_LIMIT)
        act = glu * jax.nn.sigmoid(SWIGLU_ALPHA * glu) * (lin + 1.0)
        _to_row_tiles(y_ref, _dot(act, wdn_ref[...]) + bdn_ref[...])

    @pl.when(i >= nv_ref[0])
    def _():
        y_ref[...] = jnp.zeros_like(y_ref)


def _experts(xs, tile_expert, n_valid, w_up, b_up, w_dn, b_dn, layer, tm=TM_EXPERT):
    rows = xs.shape[0] // ROW_TILE
    row_map = lambda i, te, nv: (jnp.minimum(i, nv[0] - 1), 0)
    wspec = lambda shape: pl.BlockSpec((None, None) + shape, lambda i, te, nv: (layer, te[i], 0, 0))
    grid_spec = pltpu.PrefetchScalarGridSpec(
        num_scalar_prefetch=2,
        grid=(rows // tm,),
        in_specs=[
            pl.BlockSpec((ROW_TILE * tm, LANES), row_map),
            wspec((D_MODEL, 2 * D_EXPERT)),
            wspec((1, 2 * D_EXPERT)),
            wspec((D_EXPERT, D_MODEL)),
            wspec((1, D_MODEL)),
        ],
        out_specs=pl.BlockSpec((ROW_TILE * tm, LANES), lambda i, te, nv: (i, 0)),
    )
    return pl.pallas_call(
        _expert_kernel,
        grid_spec=grid_spec,
        out_shape=jax.ShapeDtypeStruct(xs.shape, F32),
        compiler_params=pltpu.CompilerParams(
            dimension_semantics=("arbitrary",), vmem_limit_bytes=VMEM_LIMIT),
        name="moe_experts",
    )(tile_expert, n_valid, xs, w_up, b_up, w_dn, b_dn)


def _row_copy_in(ys_hbm, ybuf, sem, k, t, p):
    return pltpu.make_async_copy(_row_tile(ys_hbm, p), _row_tile(ybuf.at[k], t), sem)


def _combine_kernel(pos_ref, posn_ref, x1a_ref, x1b_ref, gate_ref, pp_ref, ps_ref, wg_ref, wp_ref,
                    g2_ref, b2_ref, *rest, tm, prompt_steps, emit_kv, fuse_gmlp, split_out):
    kv_ref = v_ref = x3s_ref = None
    if emit_kv:
        kvg_ref, kvb_ref, wkv_ref, ys_hbm, x3_ref, kv_ref, ybuf_a, ybuf_b, sem = rest
    elif fuse_gmlp:
        *gmlp_refs, ys_hbm, x3_ref, v_ref, ybuf_a, ybuf_b, sem = rest
    elif split_out:
        ys_hbm, x3_ref, x3s_ref, ybuf_a, ybuf_b, sem = rest
    else:
        ys_hbm, x3_ref, ybuf_a, ybuf_b, sem = rest
    s = pl.program_id(0)
    is_sample = s >= prompt_steps

    def issue(pref, half, ybuf, sm):
        for t in range(tm):
            for k in range(TOP_K):
                _row_copy_in(ys_hbm, ybuf, sm, k, t,
                             pref[0, 0, (half * TOP_K + k) * tm + t]).start(priority=k % 2)

    def wait_all(ybuf, sm):
        for k in range(TOP_K):
            pltpu.make_async_copy(ys_hbm.at[pl.ds(0, ROW_TILE * tm), :], ybuf.at[k], sm).wait()

    def half_math(half, ybuf):
        rows = slice(half * tm, (half + 1) * tm)
        gate = gate_ref[rows, :]
        f = gate[:, 0:1] * _from_row_tiles(ybuf.at[0], tm)
        for k in range(1, TOP_K):
            f = f + gate[:, k:k + 1] * _from_row_tiles(ybuf.at[k], tm)
        x1 = jnp.where(is_sample, x1b_ref[rows, :], x1a_ref[rows, :])
        x2 = _ln(DEEPNORM_ALPHA * x1 + f, g2_ref[...], b2_ref[...])
        p = jnp.where(is_sample, ps_ref[rows, :], pp_ref[rows, :])
        pg = jax.nn.sigmoid(_dot(x2.astype(BF16), wg_ref[...]))
        pp = _dot(p.astype(BF16), wp_ref[...])
        x3 = x2 + pg * pp
        if fuse_gmlp:
            x3_ref[rows, :], v_ref[rows, :] = _gmlp_math(x3, *gmlp_refs)
        elif split_out:
            @pl.when(is_sample)
            def _():
                x3s_ref[rows, :] = x3

            @pl.when(jnp.logical_not(is_sample))
            def _():
                x3_ref[rows, :] = x3
        else:
            x3_ref[rows, :] = x3
        if emit_kv:
            kv_ref[rows, :] = _dot(_ln(x3, kvg_ref[...], kvb_ref[...]).astype(BF16), wkv_ref[...])

    @pl.when(s == 0)
    def _():
        def first(t, c):
            for k in range(TOP_K):
                _row_copy_in(ys_hbm, ybuf_a, sem.at[0], k, t, pos_ref[0, 0, k * tm + t]).start()
            return c

        lax.fori_loop(0, tm, first, 0)

    wait_all(ybuf_a, sem.at[0])
    issue(pos_ref, 1, ybuf_b, sem.at[1])
    half_math(0, ybuf_a)
    wait_all(ybuf_b, sem.at[1])
    issue(posn_ref, 0, ybuf_a, sem.at[0])
    half_math(1, ybuf_b)

    @pl.when(s == pl.num_programs(0) - 1)
    def _():
        wait_all(ybuf_a, sem.at[0])


def _combine(src, n, pos3, gate_t, ys, p_prompt, p_sample, layer, n_prompt, w_gate, w_proj, g2, b2,
             kv_params=None, gmlp_weights=None, split_out=False, tm=TM_TOKEN):
    xa, xb, b_row0 = src
    emit_kv = kv_params is not None
    fuse_gmlp = gmlp_weights is not None
    assert emit_kv + fuse_gmlp + split_out <= 1
    tm2 = 2 * tm
    steps = n // tm2
    prompt_steps = n_prompt // tm2
    sample_steps = steps - prompt_steps
    row = lambda w: pl.BlockSpec((tm2, w), lambda i: (i, 0))
    smem_pos = lambda f: pl.BlockSpec((1, 1, 2 * TOP_K * tm), lambda i: (f(i), 0, 0),
                                      memory_space=pltpu.SMEM)
    in_specs = [
        smem_pos(lambda i: i),
        smem_pos(lambda i: jnp.minimum(i + 1, steps - 1)),
    ] + _token_specs(tm2, n_prompt, b_row0) + [
        row(TOP_K),
        pl.BlockSpec((tm2, PLE_DIM),
                     lambda i: (layer * prompt_steps + jnp.minimum(i, prompt_steps - 1), 0)),
        pl.BlockSpec((tm2, PLE_DIM),
                     lambda i: (layer * sample_steps + jnp.maximum(i - prompt_steps, 0), 0)),
        _full((D_MODEL, D_MODEL)),
        _full((PLE_DIM, D_MODEL)),
        _full((1, D_MODEL)),
        _full((1, D_MODEL)),
    ]
    args = [pos3, pos3, xa, xb, gate_t, p_prompt, p_sample, w_gate, w_proj, g2, b2]
    if split_out:
        out_specs = [
            pl.BlockSpec((tm2, D_MODEL), lambda i: (jnp.minimum(i, prompt_steps - 1), 0)),
            pl.BlockSpec((tm2, D_MODEL), lambda i: (jnp.maximum(i - prompt_steps, 0), 0)),
        ]
        out_shape = [jax.ShapeDtypeStruct((n_prompt, D_MODEL), F32),
                     jax.ShapeDtypeStruct((n - n_prompt, D_MODEL), F32)]
    else:
        out_specs = [row(D_MODEL)]
        out_shape = [jax.ShapeDtypeStruct((n, D_MODEL), F32)]
    if emit_kv:
        in_specs += [_full((1, D_MODEL)), _full((1, D_MODEL)), _full((D_MODEL, 2 * KV_DIM))]
        args += list(kv_params)
        out_specs.append(row(2 * KV_DIM))
        out_shape.append(jax.ShapeDtypeStruct((n, 2 * KV_DIM), F32))
    if fuse_gmlp:
        in_specs += _gmlp_weight_specs(lambda i: jnp.where(i >= prompt_steps, 1, 0))
        args += list(gmlp_weights)
        out_specs.append(pl.BlockSpec((tm2, D_MODEL),
                                      lambda i: (jnp.maximum(i - (prompt_steps - 1), 0), 0)))
        out_shape.append(jax.ShapeDtypeStruct(((sample_steps + 1) * tm2, D_MODEL), F32))
    in_specs.append(pl.BlockSpec(memory_space=pl.ANY))
    args.append(ys)
    outs = pl.pallas_call(
        functools.partial(_combine_kernel, tm=tm, prompt_steps=prompt_steps, emit_kv=emit_kv,
                          fuse_gmlp=fuse_gmlp, split_out=split_out),
        grid=(steps,),
        in_specs=in_specs,
        out_specs=out_specs,
        out_shape=out_shape,
        scratch_shapes=[pltpu.VMEM((TOP_K, ROW_TILE * tm, LANES), F32),
                        pltpu.VMEM((TOP_K, ROW_TILE * tm, LANES), F32),
                        pltpu.SemaphoreType.DMA((2,))],
        compiler_params=pltpu.CompilerParams(
            dimension_semantics=("arbitrary",), vmem_limit_bytes=VMEM_LIMIT),
        name="moe_combine",
    )(*args)
    if fuse_gmlp:
        return outs[0], outs[1][tm2:]
    return outs if (emit_kv or split_out) else (outs[0], None)


def _tile_pos(pos, tm):
    n = pos.shape[1]
    return pos.reshape(TOP_K, n // tm, tm).transpose(1, 0, 2).reshape(n // tm, 1, TOP_K * tm)


def _pair_pos(pos, tm):
    n = pos.shape[1]
    p = pos.reshape(TOP_K, n // (2 * tm), 2, tm).transpose(1, 2, 0, 3)
    return p.reshape(n // (2 * tm), 1, 2 * TOP_K * tm)


def _moe_layer(src, n, xs_buf, p_prompt, p_sample, n_prompt, layer, router_wt, router_b, tri,
               w_up, b_up, w_dn, b_dn, w_gate, w_proj, g2, b2, kv_params=None, gmlp_weights=None,
               split_out=False):
    idx, gate, rank, counts = _route(src, n, n_prompt, router_wt, router_b, tri)
    padded = ((counts + TM_EXPERT - 1) // TM_EXPERT) * TM_EXPERT
    ends = jnp.cumsum(padded)
    starts = ends - padded
    onehot = idx[:, :, None] == jnp.arange(N_EXPERTS, dtype=jnp.int32)
    pos = jnp.sum(jnp.where(onehot, starts, 0), axis=-1) + rank
    n_tiles = xs_buf.shape[0] // (ROW_TILE * TM_EXPERT)
    n_valid = (ends[-1] // TM_EXPERT).astype(jnp.int32)
    tile_start = jnp.minimum(jnp.arange(n_tiles, dtype=jnp.int32), n_valid - 1) * TM_EXPERT
    tile_expert = jnp.sum(tile_start[:, None] >= ends[None, :], axis=1).astype(jnp.int32)
    xs_buf = _dispatch(src, n, n_prompt, _tile_pos(pos, TM_DISPATCH), xs_buf)
    ys = _experts(xs_buf, tile_expert, n_valid.reshape(1), w_up, b_up, w_dn, b_dn, layer)
    x3, extra = _combine(src, n, _pair_pos(pos, TM_TOKEN), gate.T, ys, p_prompt, p_sample, layer,
                         n_prompt, w_gate, w_proj, g2, b2, kv_params, gmlp_weights, split_out)
    return x3, extra, xs_buf


def _alibi_slopes():
    h = jnp.arange(1, N_HEADS + 1, dtype=F32)
    return 2.0 ** (-8.0 * h / N_HEADS)


def _prompt_bias():
    qi = jnp.arange(WINDOW)[:, None]
    kj = jnp.arange(2 * WINDOW)[None, :]
    dist = qi + WINDOW - kj
    valid = (dist >= 0) & (dist < WINDOW)
    slopes = _alibi_slopes()[:, None, None]
    tables = []
    for first in (True, False):
        ok = valid & (kj >= WINDOW) if first else valid
        tables.append(jnp.where(ok[None], -slopes * dist.astype(F32)[None], NEG_INF))
    return jnp.stack(tables).astype(F32)


def _sample_tables(t, sinks):
    g = jnp.arange(Q_PER_KV)[:, None, None]
    kh = jnp.arange(N_KV_HEADS)[None, :, None]
    tok = jnp.arange(t)[None, None, :]
    head = jnp.broadcast_to(kh * Q_PER_KV + g, (Q_PER_KV, N_KV_HEADS, t)).reshape(-1)
    tok = jnp.broadcast_to(tok, (Q_PER_KV, N_KV_HEADS, t)).reshape(-1)
    kj = jnp.arange(KEY_PAD)[None, :]
    dist = tok[:, None] + WINDOW - kj
    valid = (dist >= 0) & (dist < WINDOW) & (kj < WINDOW + t)
    slopes = _alibi_slopes()[head][:, None]
    bias = jnp.where(valid, -slopes * dist.astype(F32), NEG_INF).astype(F32)
    sink_col = sinks[head][:, None].astype(F32)
    row_kh = jnp.repeat(jnp.arange(N_KV_HEADS), t)[:, None]
    lane_kh = (jnp.arange(KV_DIM) // HEAD_DIM)[None, :]
    mask = (row_kh == lane_kh).astype(F32)
    return bias, sink_col, mask


def _gkd_cols(w):
    r = w.shape[0]
    w4 = w.reshape(r, N_KV_HEADS, Q_PER_KV, HEAD_DIM)
    return w4.transpose(0, 2, 1, 3).reshape(r, N_HEADS * HEAD_DIM)


def _spatial_tables(w_s, b_s, t):
    tril = jnp.tril(jnp.ones((CHUNK, CHUNK), dtype=bool))
    ws_p = jnp.where(tril[None], w_s, 0.0)
    bs_p = jnp.repeat(b_s.T, GROUP_DIM_A, axis=1)
    r = jnp.arange(CHUNK)
    same = (r[:, None] // t) == (r[None, :] // t)
    small = jnp.where(tril[None, :t, :t], w_s[:, :t, :t], 0.0)
    ws_s = jnp.where(same[None], jnp.tile(small, (1, CHUNK // t, CHUNK // t)), 0.0)
    bs_s = jnp.repeat(jnp.tile(b_s[:, :t], (1, CHUNK // t)).T, GROUP_DIM_A, axis=1)
    return jnp.stack([ws_p, ws_s]).astype(BF16), jnp.stack([bs_p, bs_s]).astype(F32)


def kernel(x_prompt, x_sample, cache_k, cache_v, p_prompt, p_sample, ln1_g, ln1_b, ln2_g, ln2_b,
           a_w_in, a_b_in, a_ln_g, a_ln_b, a_w_s, a_b_s, a_w_out, kv_ln_g, kv_ln_b, w_kv,
           b_w_q, b_sinks, b_w_o, router_w, router_b, exp_w_up, exp_b_up, exp_w_dn, exp_b_dn,
           ple_w_proj, ple_w_gate):
    bsz, seq, _ = x_prompt.shape
    n_seq, t, _ = x_sample.shape
    n_p = bsz * seq
    n_s = n_seq * t
    n = n_p + n_s
    depth = ln1_g.shape[0]
    n_a = a_w_in.shape[0]

    xp2 = x_prompt.reshape(n_p, D_MODEL)
    xs2 = x_sample.reshape(n_s, D_MODEL)
    pp2 = p_prompt.reshape(depth * n_p, PLE_DIM)
    ps2 = p_sample.reshape(depth * n_s, PLE_DIM)
    row = lambda v: v.reshape(1, -1).astype(F32)

    xs_buf = jnp.zeros((ROW_TILE * (n * TOP_K + N_EXPERTS * TM_EXPERT), LANES), F32)
    tri = (jnp.arange(TM_ROUTER)[:, None] <= jnp.arange(TM_ROUTER)[None, :]).astype(BF16)
    b_up4 = exp_b_up.reshape(depth, N_EXPERTS, 1, 2 * D_EXPERT)
    b_dn4 = exp_b_dn.reshape(depth, N_EXPERTS, 1, D_MODEL)
    prompt_bias = _prompt_bias()

    def gmlp_weights(i):
        ws2, bs2 = _spatial_tables(a_w_s[i], a_b_s[i], t)
        return (a_w_in[i].astype(BF16), row(a_b_in[i]), row(a_ln_g[i]), row(a_ln_b[i]), ws2, bs2,
                a_w_out[i].astype(BF16), row(ln1_g[i]), row(ln1_b[i]))

    chunk_v = []
    kv_params = (row(kv_ln_g), row(kv_ln_b), w_kv.astype(BF16))
    x = x1_next = x_out_p = x_out_s = None
    kp = vp = k_buf = v_buf = None
    kpad = vpad = kbuf_pad = vbuf_pad = None
    for i in range(depth):
        fuse_next = i + 1 < n_a
        last = i == depth - 1
        if i < n_a:
            if i == 0:
                x1, v_rows = _gmlp_layer(xp2, xs2, 0, n_p, n_s, gmlp_weights(0))
                chunk_v.append(v_rows.reshape(n_seq, t, D_MODEL))
            else:
                x1 = x1_next
            src = (x1, x1, n_p)
        else:
            j = i - n_a
            wq = b_w_q[j].astype(BF16)
            wo = b_w_o[j].astype(BF16)
            x1_p = _swa_prompt_layer(x, bsz, seq, kpad, vpad, prompt_bias,
                                     b_sinks[j].astype(F32), wq, wo, row(ln1_g[i]), row(ln1_b[i]))
            s_bias, s_sink, s_mask = _sample_tables(t, b_sinks[j])
            x1_s = _swa_sample_layer(x, n_p, n_seq, t, kbuf_pad, vbuf_pad, s_bias, s_sink, s_mask,
                                     _gkd_cols(wq), _gkd_cols(wo.T).T, row(ln1_g[i]), row(ln1_b[i]))
            src = (x1_p, x1_s, 0)
        out, extra, xs_buf = _moe_layer(
            src, n, xs_buf, pp2, ps2, n_p, i, router_w[i].T, router_b[i].reshape(N_EXPERTS, 1), tri,
            exp_w_up, b_up4, exp_w_dn, b_dn4, ple_w_gate[i].astype(BF16),
            ple_w_proj[i].astype(BF16), row(ln2_g[i]), row(ln2_b[i]),
            kv_params if i == n_a - 1 else None,
            gmlp_weights(i + 1) if fuse_next else None,
            split_out=last and i != n_a - 1 and not fuse_next)
        if fuse_next:
            x1_next = out
            chunk_v.append(extra.reshape(n_seq, t, D_MODEL))
        elif last and i != n_a - 1:
            x_out_p, x_out_s = out, extra
        else:
            x, kv = out, extra
        if last and i == n_a - 1:
            x_out_p, x_out_s = x[:n_p], x[n_p:]
        if i == n_a - 1:
            k_all, v_all = kv[:, :KV_DIM], kv[:, KV_DIM:]
            kp = k_all[:n_p].reshape(bsz, seq, KV_DIM)
            vp = v_all[:n_p].reshape(bsz, seq, KV_DIM)
            k_buf = jnp.concatenate([cache_k.reshape(n_seq, WINDOW, KV_DIM),
                                     k_all[n_p:].reshape(n_seq, t, KV_DIM)], axis=1)
            v_buf = jnp.concatenate([cache_v.reshape(n_seq, WINDOW, KV_DIM),
                                     v_all[n_p:].reshape(n_seq, t, KV_DIM)], axis=1)
            front = ((0, 0), (WINDOW, 0), (0, 0))
            kpad = jnp.pad(kp, front).astype(BF16)
            vpad = jnp.pad(vp, front).astype(BF16)
            tail = ((0, 0), (0, KEY_PAD - WINDOW - t), (0, 0))
            kbuf_pad = jnp.pad(k_buf, tail).astype(BF16)
            vbuf_pad = jnp.pad(v_buf, tail).astype(BF16)

    heads = (N_KV_HEADS, HEAD_DIM)
    return (x_out_p.reshape(bsz, seq, D_MODEL),
            x_out_s.reshape(n_seq, t, D_MODEL),
            jnp.stack(chunk_v),
            kp[:, -WINDOW:].reshape(bsz, WINDOW, *heads),
            vp[:, -WINDOW:].reshape(bsz, WINDOW, *heads),
            k_buf[:, -WINDOW:].reshape(n_seq, WINDOW, *heads),
            v_buf[:, -WINDOW:].reshape(n_seq, WINDOW, *heads))
```

```python
import functools

import jax
import jax.numpy as jnp
from jax import lax
from jax.experimental import pallas as pl
from jax.experimental.pallas import tpu as pltpu

F32 = jnp.float32
BF16 = jnp.bfloat16

D_MODEL = 1024
DEPTH = 4
N_A_LAYERS = 2
CHUNK = 128
N_GROUPS_A = 8
GROUP_DIM_A = D_MODEL // N_GROUPS_A
HEAD_DIM = 64
N_HEADS = 16
N_KV_HEADS = 4
Q_PER_KV = 4
KV_DIM = N_KV_HEADS * HEAD_DIM
WINDOW = 128
N_EXPERTS = 32
TOP_K = 4
D_EXPERT = 1024
SWIGLU_LIMIT = 7.0
SWIGLU_ALPHA = 1.702
PLE_DIM = 256
DEEPNORM_ALPHA = (2 * DEPTH) ** 0.25
LN_EPS = 1e-5
NEG_INF = -1e30

TM_TOKEN = 256
TM_MIXER = 512
TM_DISPATCH = 512
DISPATCH_CHUNK = 128
TM_ROUTER = 512
TM_EXPERT = 512
SAMPLE_SEQ_BLOCK = 16
KEY_PAD = 256
VMEM_LIMIT = 56 * 1024 * 1024


def _ln(x, g, b):
    mu = jnp.mean(x, axis=-1, keepdims=True)
    xc = x - mu
    var = jnp.mean(xc * xc, axis=-1, keepdims=True)
    return xc * lax.rsqrt(var + LN_EPS) * g + b


def _dot(a, b):
    return jnp.dot(a, b, preferred_element_type=F32)


def _dot_nt(a, b):
    return lax.dot_general(a, b, (((1,), (1,)), ((), ())), preferred_element_type=F32)


def _full(shape):
    n = len(shape)
    return pl.BlockSpec(shape, lambda *_: (0,) * n)


def _gmlp_math(x, win_ref, bin_ref, lng_ref, lnb_ref, ws_ref, bs_ref, wout_ref, g1_ref, b1_ref):
    h = _dot(x.astype(BF16), win_ref[...]) + bin_ref[...]
    h = 0.5 * h * (1.0 + lax.erf(h * (2.0 ** -0.5)))
    u = h[:, :D_MODEL]
    v = _ln(h[:, D_MODEL:], lng_ref[...], lnb_ref[...])
    vb = v.astype(BF16)
    rows = []
    for c in range(x.shape[0] // CHUNK):
        cols = []
        for g in range(N_GROUPS_A):
            blk = vb[c * CHUNK:(c + 1) * CHUNK, g * GROUP_DIM_A:(g + 1) * GROUP_DIM_A]
            cols.append(_dot(ws_ref[g], blk))
        rows.append(jnp.concatenate(cols, axis=1) + bs_ref[...])
    s = jnp.concatenate(rows, axis=0)
    gated = (u * s).astype(BF16)
    m = _dot(gated, wout_ref[...])
    return _ln(DEEPNORM_ALPHA * x + m, g1_ref[...], b1_ref[...]), v


def _gmlp_kernel(xa_ref, xb_ref, *refs, p_tiles):
    *w_refs, x1_ref, v_ref = refs
    x = jnp.where(pl.program_id(0) >= p_tiles, xb_ref[...], xa_ref[...])
    x1_ref[...], v_ref[...] = _gmlp_math(x, *w_refs)


def _gmlp_weight_specs(kind):
    return [
        _full((D_MODEL, 2 * D_MODEL)),
        _full((1, 2 * D_MODEL)),
        _full((1, D_MODEL)),
        _full((1, D_MODEL)),
        pl.BlockSpec((None, N_GROUPS_A, CHUNK, CHUNK), lambda i: (kind(i), 0, 0, 0)),
        pl.BlockSpec((None, CHUNK, D_MODEL), lambda i: (kind(i), 0, 0)),
        _full((D_MODEL, D_MODEL)),
        _full((1, D_MODEL)),
        _full((1, D_MODEL)),
    ]


def _gmlp_layer(xa, xb, xb_row0, n_prompt, n_sample, weights, tm=TM_MIXER):
    n = n_prompt + n_sample
    p_tiles = n_prompt // tm
    n_tiles = n // tm
    s_tiles = n_tiles - p_tiles
    b_tile0 = xb_row0 // tm
    kind = lambda i: jnp.where(i >= p_tiles, 1, 0)
    x1, v = pl.pallas_call(
        functools.partial(_gmlp_kernel, p_tiles=p_tiles),
        grid=(n_tiles,),
        in_specs=[
            pl.BlockSpec((tm, D_MODEL), lambda i: (jnp.minimum(i, p_tiles - 1), 0)),
            pl.BlockSpec((tm, D_MODEL), lambda i: (jnp.maximum(i - p_tiles, 0) + b_tile0, 0)),
        ] + _gmlp_weight_specs(kind),
        out_specs=[
            pl.BlockSpec((tm, D_MODEL), lambda i: (i, 0)),
            pl.BlockSpec((tm, D_MODEL), lambda i: (jnp.maximum(i - (p_tiles - 1), 0), 0)),
        ],
        out_shape=[
            jax.ShapeDtypeStruct((n, D_MODEL), F32),
            jax.ShapeDtypeStruct(((s_tiles + 1) * tm, D_MODEL), F32),
        ],
        compiler_params=pltpu.CompilerParams(
            dimension_semantics=("arbitrary",), vmem_limit_bytes=VMEM_LIMIT),
        name="gmlp_mixer",
    )(xa, xb, *weights)
    return x1, v[tm:]


def _swa_prompt_kernel(sink_ref, x_ref, k_ref, v_ref, bias_ref, wq_ref, wo_ref, g1_ref, b1_ref,
                       x1_ref, *, tq):
    i = pl.program_id(1)
    x = x_ref[...]
    qb = (_dot(x.astype(BF16), wq_ref[...]) * (HEAD_DIM ** -0.5)).astype(BF16)
    blocks = []
    for j in range(tq // WINDOW):
        blk = i * (tq // WINDOW) + j
        kstart = pl.multiple_of(blk * WINDOW, WINDOW)
        kb = k_ref[pl.ds(kstart, 2 * WINDOW), :]
        vb = v_ref[pl.ds(kstart, 2 * WINDOW), :]
        sel = jnp.where(blk == 0, 0, 1)
        outs = []
        for h in range(N_HEADS):
            kh = h // Q_PER_KV
            qh = qb[j * WINDOW:(j + 1) * WINDOW, h * HEAD_DIM:(h + 1) * HEAD_DIM]
            s = _dot_nt(qh, kb[:, kh * HEAD_DIM:(kh + 1) * HEAD_DIM])
            l = s + bias_ref[sel, h]
            sink = sink_ref[h]
            m = jnp.maximum(jnp.max(l, axis=1, keepdims=True), sink)
            p = jnp.exp(l - m)
            den = jnp.sum(p, axis=1, keepdims=True) + jnp.exp(sink - m)
            o = _dot(p.astype(BF16), vb[:, kh * HEAD_DIM:(kh + 1) * HEAD_DIM])
            outs.append(o / den)
        blocks.append(jnp.concatenate(outs, axis=1))
    attn = jnp.concatenate(blocks, axis=0)
    m_out = _dot(attn.astype(BF16), wo_ref[...])
    x1_ref[...] = _ln(DEEPNORM_ALPHA * x + m_out, g1_ref[...], b1_ref[...])


def _swa_prompt_layer(x, bsz, seq, kpad, vpad, bias, sinks, w_q, w_o, g1, b1, tq=TM_MIXER):
    nq = seq // tq
    return pl.pallas_call(
        functools.partial(_swa_prompt_kernel, tq=tq),
        grid=(bsz, nq),
        in_specs=[
            pl.BlockSpec(memory_space=pltpu.SMEM),
            pl.BlockSpec((tq, D_MODEL), lambda b, i: (b * nq + i, 0)),
            pl.BlockSpec((None, seq + WINDOW, KV_DIM), lambda b, i: (b, 0, 0)),
            pl.BlockSpec((None, seq + WINDOW, KV_DIM), lambda b, i: (b, 0, 0)),
            _full((2, N_HEADS, WINDOW, 2 * WINDOW)),
            _full((D_MODEL, D_MODEL)),
            _full((D_MODEL, D_MODEL)),
            _full((1, D_MODEL)),
            _full((1, D_MODEL)),
        ],
        out_specs=pl.BlockSpec((tq, D_MODEL), lambda b, i: (b * nq + i, 0)),
        out_shape=jax.ShapeDtypeStruct((bsz * seq, D_MODEL), F32),
        compiler_params=pltpu.CompilerParams(
            dimension_semantics=("arbitrary", "arbitrary"), vmem_limit_bytes=VMEM_LIMIT),
        name="swa_prompt",
    )(sinks, x, kpad, vpad, bias, w_q, w_o, g1, b1)


def _swa_sample_kernel(x_ref, k_ref, v_ref, bias_ref, sink_ref, mask_ref, wq_ref, wo_ref,
                       g1_ref, b1_ref, x1_ref, *, sb, t):
    x = x_ref[...]
    q = _dot(x.astype(BF16), wq_ref[...]) * (HEAD_DIM ** -0.5)
    mask = mask_ref[...]
    bias = bias_ref[...]
    sink = sink_ref[...]
    outs = []
    for s in range(sb):
        qs = q[s * t:(s + 1) * t, :]
        parts = []
        for g in range(Q_PER_KV):
            qg = qs[:, g * KV_DIM:(g + 1) * KV_DIM]
            parts.append(jnp.concatenate([qg] * N_KV_HEADS, axis=0) * mask)
        qexp = jnp.concatenate(parts, axis=0).astype(BF16)
        l = _dot_nt(qexp, k_ref[s]) + bias
        m = jnp.maximum(jnp.max(l, axis=1, keepdims=True), sink)
        p = jnp.exp(l - m)
        den = jnp.sum(p, axis=1, keepdims=True) + jnp.exp(sink - m)
        r = _dot(p.astype(BF16), v_ref[s]) / den
        og = []
        for g in range(Q_PER_KV):
            rg = r[g * N_KV_HEADS * t:(g + 1) * N_KV_HEADS * t, :] * mask
            acc = rg[0:t]
            for kh in range(1, N_KV_HEADS):
                acc = acc + rg[kh * t:(kh + 1) * t]
            og.append(acc)
        outs.append(jnp.concatenate(og, axis=1))
    attn = jnp.concatenate(outs, axis=0)
    m_out = _dot(attn.astype(BF16), wo_ref[...])
    x1_ref[...] = _ln(DEEPNORM_ALPHA * x + m_out, g1_ref[...], b1_ref[...])


def _swa_sample_layer(x, row0, n_seq, t, kbuf, vbuf, bias, sink_col, mask, w_q, w_o, g1, b1,
                      sb=SAMPLE_SEQ_BLOCK):
    rows = sb * t
    hr = N_HEADS * t
    blk0 = row0 // rows
    return pl.pallas_call(
        functools.partial(_swa_sample_kernel, sb=sb, t=t),
        grid=(n_seq // sb,),
        in_specs=[
            pl.BlockSpec((rows, D_MODEL), lambda i: (blk0 + i, 0)),
            pl.BlockSpec((sb, KEY_PAD, KV_DIM), lambda i: (i, 0, 0)),
            pl.BlockSpec((sb, KEY_PAD, KV_DIM), lambda i: (i, 0, 0)),
            _full((hr, KEY_PAD)),
            _full((hr, 1)),
            _full((N_KV_HEADS * t, KV_DIM)),
            _full((D_MODEL, D_MODEL)),
            _full((D_MODEL, D_MODEL)),
            _full((1, D_MODEL)),
            _full((1, D_MODEL)),
        ],
        out_specs=pl.BlockSpec((rows, D_MODEL), lambda i: (i, 0)),
        out_shape=jax.ShapeDtypeStruct((n_seq * t, D_MODEL), F32),
        compiler_params=pltpu.CompilerParams(
            dimension_semantics=("arbitrary",), vmem_limit_bytes=VMEM_LIMIT),
        name="swa_sample",
    )(x, kbuf, vbuf, bias, sink_col, mask, w_q, w_o, g1, b1)


def _token_specs(tm, n_prompt, b_row0):
    p_steps = n_prompt // tm
    b0 = b_row0 // tm
    return [pl.BlockSpec((tm, D_MODEL), lambda i: (jnp.minimum(i, p_steps - 1), 0)),
            pl.BlockSpec((tm, D_MODEL), lambda i: (jnp.maximum(i - p_steps, 0) + b0, 0))]


def _router_kernel(xa_ref, xb_ref, wt_ref, b_ref, tri_ref, idx_ref, gate_ref, rank_ref, cnt_ref,
                   carry, *, tm, p_steps):
    i = pl.program_id(0)

    @pl.when(i == 0)
    def _():
        carry[...] = jnp.zeros_like(carry)

    x = jnp.where(i >= p_steps, xb_ref[...], xa_ref[...])
    logits = lax.dot_general(wt_ref[...], x, (((1,), (1,)), ((), ())),
                             precision=lax.Precision.HIGHEST,
                             preferred_element_type=F32) + b_ref[...]
    eidx = lax.broadcasted_iota(jnp.int32, (N_EXPERTS, tm), 0)
    l = logits
    vals, idxs, sels = [], [], []
    for _ in range(TOP_K):
        m = jnp.max(l, axis=0, keepdims=True)
        idx = jnp.min(jnp.where(l == m, eidx, N_EXPERTS), axis=0, keepdims=True)
        sel = eidx == idx
        vals.append(m)
        idxs.append(idx)
        sels.append(sel)
        l = jnp.where(sel, -jnp.inf, l)
    exps = [jnp.exp(v - vals[0]) for v in vals]
    den = exps[0] + exps[1] + exps[2] + exps[3]
    onehot = jnp.zeros((N_EXPERTS, tm), F32)
    for sel in sels:
        onehot = onehot + sel.astype(F32)
    incl = _dot(onehot.astype(BF16), tri_ref[...])
    excl = incl - onehot + carry[...]
    for k in range(TOP_K):
        idx_ref[k:k + 1, :] = idxs[k]
        gate_ref[k:k + 1, :] = exps[k] / den
        rank = jnp.sum(jnp.where(sels[k], excl, 0.0), axis=0, keepdims=True)
        rank_ref[k:k + 1, :] = rank.astype(jnp.int32)
    carry[...] = carry[...] + jnp.sum(onehot, axis=1, keepdims=True)
    cnt_ref[...] = jnp.broadcast_to(carry[...], cnt_ref.shape)


def _route(src, n, n_prompt, router_wt, router_b, tri, tm=TM_ROUTER):
    xa, xb, b_row0 = src
    spec_kn = pl.BlockSpec((TOP_K, tm), lambda i: (0, i))
    idx, gate, rank, cnt = pl.pallas_call(
        functools.partial(_router_kernel, tm=tm, p_steps=n_prompt // tm),
        grid=(n // tm,),
        in_specs=_token_specs(tm, n_prompt, b_row0) + [
            _full((N_EXPERTS, D_MODEL)),
            _full((N_EXPERTS, 1)),
            _full((tm, tm)),
        ],
        out_specs=[spec_kn, spec_kn, spec_kn, _full((N_EXPERTS, 128))],
        out_shape=[
            jax.ShapeDtypeStruct((TOP_K, n), jnp.int32),
            jax.ShapeDtypeStruct((TOP_K, n), F32),
            jax.ShapeDtypeStruct((TOP_K, n), jnp.int32),
            jax.ShapeDtypeStruct((N_EXPERTS, 128), F32),
        ],
        scratch_shapes=[pltpu.VMEM((N_EXPERTS, 1), F32)],
        compiler_params=pltpu.CompilerParams(
            dimension_semantics=("arbitrary",), vmem_limit_bytes=VMEM_LIMIT),
        name="moe_router",
    )(xa, xb, router_wt, router_b, tri)
    return idx, gate, rank, cnt[:, 0].astype(jnp.int32)


ROW_TILE = 8
LANES = D_MODEL // ROW_TILE


def _to_row_tiles(ref, x):
    rows = x.shape[0]
    for c in range(ROW_TILE):
        ref[pl.ds(c, rows, stride=ROW_TILE), :] = x[:, c * LANES:(c + 1) * LANES]


def _from_row_tiles(ref, rows):
    return jnp.concatenate([ref[pl.ds(c, rows, stride=ROW_TILE), :] for c in range(ROW_TILE)], axis=1)


def _row_tile(ref, p):
    start = p * ROW_TILE
    if not isinstance(p, int):
        start = pl.multiple_of(start, ROW_TILE)
    return ref.at[pl.ds(start, ROW_TILE), :]


def _row_copy_out(x_ref, xs_hbm, sem, t, p):
    return pltpu.make_async_copy(_row_tile(x_ref, t), _row_tile(xs_hbm, p), sem)


def _dispatch_kernel(pos_ref, xa_ref, xb_ref, xs_in_hbm, xs_hbm, x_ref, sem, *, tm, chunk, p_steps):
    del xs_in_hbm
    _to_row_tiles(x_ref, jnp.where(pl.program_id(0) >= p_steps, xb_ref[...], xa_ref[...]))

    def issue(c, carry):
        t0 = c * chunk
        for t in range(chunk):
            for k in range(TOP_K):
                _row_copy_out(x_ref, xs_hbm, sem, t0 + t,
                              pos_ref[0, 0, k * tm + t0 + t]).start(priority=k % 2)
        return carry

    lax.fori_loop(0, tm // chunk, issue, 0)
    for _ in range(TOP_K):
        pltpu.make_async_copy(x_ref, xs_hbm.at[pl.ds(0, ROW_TILE * tm), :], sem).wait()


def _dispatch(src, n, n_prompt, pos3, xs_buf, tm=TM_DISPATCH, chunk=DISPATCH_CHUNK):
    xa, xb, b_row0 = src
    return pl.pallas_call(
        functools.partial(_dispatch_kernel, tm=tm, chunk=chunk, p_steps=n_prompt // tm),
        grid=(n // tm,),
        in_specs=[
            pl.BlockSpec((1, 1, TOP_K * tm), lambda i: (i, 0, 0), memory_space=pltpu.SMEM),
        ] + _token_specs(tm, n_prompt, b_row0) + [
            pl.BlockSpec(memory_space=pl.ANY),
        ],
        out_specs=pl.BlockSpec(memory_space=pl.ANY),
        out_shape=jax.ShapeDtypeStruct(xs_buf.shape, xs_buf.dtype),
        scratch_shapes=[pltpu.VMEM((ROW_TILE * tm, LANES), F32), pltpu.SemaphoreType.DMA],
        input_output_aliases={3: 0},
        compiler_params=pltpu.CompilerParams(
            dimension_semantics=("arbitrary",), vmem_limit_bytes=VMEM_LIMIT),
        name="moe_dispatch",
    )(pos3, xa, xb, xs_buf)


def _expert_kernel(te_ref, nv_ref, xs_ref, wup_ref, bup_ref, wdn_ref, bdn_ref, y_ref):
    del te_ref
    i = pl.program_id(0)

    @pl.when(i < nv_ref[0])
    def _():
        x = _from_row_tiles(xs_ref, y_ref.shape[0] // ROW_TILE)
        h = _dot(x, wup_ref[...]) + bup_ref[...]
        glu = jnp.minimum(h[:, :D_EXPERT], SWIGLU_LIMIT)
        lin = jnp.clip(h[:, D_EXPERT:], -SWIGLU_LIMIT, SWIGLU_LIMIT)
        act = glu * jax.nn.sigmoid(SWIGLU_ALPHA * glu) * (lin + 1.0)
        _to_row_tiles(y_ref, _dot(act, wdn_ref[...]) + bdn_ref[...])

    @pl.when(i >= nv_ref[0])
    def _():
        y_ref[...] = jnp.zeros_like(y_ref)


def _experts(xs, tile_expert, n_valid, w_up, b_up, w_dn, b_dn, layer, tm=TM_EXPERT):
    rows = xs.shape[0] // ROW_TILE
    row_map = lambda i, te, nv: (jnp.minimum(i, nv[0] - 1), 0)
    wspec = lambda shape: pl.BlockSpec((None, None) + shape, lambda i, te, nv: (layer, te[i], 0, 0))
    grid_spec = pltpu.PrefetchScalarGridSpec(
        num_scalar_prefetch=2,
        grid=(rows // tm,),
        in_specs=[
            pl.BlockSpec((ROW_TILE * tm, LANES), row_map),
            wspec((D_MODEL, 2 * D_EXPERT)),
            wspec((1, 2 * D_EXPERT)),
            wspec((D_EXPERT, D_MODEL)),
            wspec((1, D_MODEL)),
        ],
        out_specs=pl.BlockSpec((ROW_TILE * tm, LANES), lambda i, te, nv: (i, 0)),
    )
    return pl.pallas_call(
        _expert_kernel,
        grid_spec=grid_spec,
        out_shape=jax.ShapeDtypeStruct(xs.shape, F32),
        compiler_params=pltpu.CompilerParams(
            dimension_semantics=("arbitrary",), vmem_limit_bytes=VMEM_LIMIT),
        name="moe_experts",
    )(tile_expert, n_valid, xs, w_up, b_up, w_dn, b_dn)


def _row_copy_in(ys_hbm, ybuf, sem, k, t, p):
    return pltpu.make_async_copy(_row_tile(ys_hbm, p), _row_tile(ybuf.at[k], t), sem)


def _combine_kernel(pos_ref, posn_ref, x1a_ref, x1b_ref, gate_ref, pp_ref, ps_ref, wg_ref, wp_ref,
                    g2_ref, b2_ref, *rest, tm, prompt_steps, emit_kv, fuse_gmlp, split_out):
    kv_ref = v_ref = x3s_ref = None
    if emit_kv:
        kvg_ref, kvb_ref, wkv_ref, ys_hbm, x3_ref, kv_ref, ybuf_a, ybuf_b, sem = rest
    elif fuse_gmlp:
        *gmlp_refs, ys_hbm, x3_ref, v_ref, ybuf_a, ybuf_b, sem = rest
    elif split_out:
        ys_hbm, x3_ref, x3s_ref, ybuf_a, ybuf_b, sem = rest
    else:
        ys_hbm, x3_ref, ybuf_a, ybuf_b, sem = rest
    s = pl.program_id(0)
    is_sample = s >= prompt_steps

    def issue(pref, half, ybuf, sm):
        for t in range(tm):
            for k in range(TOP_K):
                _row_copy_in(ys_hbm, ybuf, sm, k, t,
                             pref[0, 0, (half * TOP_K + k) * tm + t]).start(priority=k % 2)

    def wait_all(ybuf, sm):
        for k in range(TOP_K):
            pltpu.make_async_copy(ys_hbm.at[pl.ds(0, ROW_TILE * tm), :], ybuf.at[k], sm).wait()

    def half_math(half, ybuf):
        rows = slice(half * tm, (half + 1) * tm)
        gate = gate_ref[rows, :]
        f = gate[:, 0:1] * _from_row_tiles(ybuf.at[0], tm)
        for k in range(1, TOP_K):
            f = f + gate[:, k:k + 1] * _from_row_tiles(ybuf.at[k], tm)
        x1 = jnp.where(is_sample, x1b_ref[rows, :], x1a_ref[rows, :])
        x2 = _ln(DEEPNORM_ALPHA * x1 + f, g2_ref[...], b2_ref[...])
        p = jnp.where(is_sample, ps_ref[rows, :], pp_ref[rows, :])
        pg = jax.nn.sigmoid(_dot(x2.astype(BF16), wg_ref[...]))
        pp = _dot(p.astype(BF16), wp_ref[...])
        x3 = x2 + pg * pp
        if fuse_gmlp:
            x3_ref[rows, :], v_ref[rows, :] = _gmlp_math(x3, *gmlp_refs)
        elif split_out:
            @pl.when(is_sample)
            def _():
                x3s_ref[rows, :] = x3

            @pl.when(jnp.logical_not(is_sample))
            def _():
                x3_ref[rows, :] = x3
        else:
            x3_ref[rows, :] = x3
        if emit_kv:
            kv_ref[rows, :] = _dot(_ln(x3, kvg_ref[...], kvb_ref[...]).astype(BF16), wkv_ref[...])

    @pl.when(s == 0)
    def _():
        def first(t, c):
            for k in range(TOP_K):
                _row_copy_in(ys_hbm, ybuf_a, sem.at[0], k, t, pos_ref[0, 0, k * tm + t]).start()
            return c

        lax.fori_loop(0, tm, first, 0)

    wait_all(ybuf_a, sem.at[0])
    issue(pos_ref, 1, ybuf_b, sem.at[1])
    half_math(0, ybuf_a)
    wait_all(ybuf_b, sem.at[1])
    issue(posn_ref, 0, ybuf_a, sem.at[0])
    half_math(1, ybuf_b)

    @pl.when(s == pl.num_programs(0) - 1)
    def _():
        wait_all(ybuf_a, sem.at[0])


def _combine(src, n, pos3, gate_t, ys, p_prompt, p_sample, layer, n_prompt, w_gate, w_proj, g2, b2,
             kv_params=None, gmlp_weights=None, split_out=False, tm=TM_TOKEN):
    xa, xb, b_row0 = src
    emit_kv = kv_params is not None
    fuse_gmlp = gmlp_weights is not None
    assert emit_kv + fuse_gmlp + split_out <= 1
    tm2 = 2 * tm
    steps = n // tm2
    prompt_steps = n_prompt // tm2
    sample_steps = steps - prompt_steps
    row = lambda w: pl.BlockSpec((tm2, w), lambda i: (i, 0))
    smem_pos = lambda f: pl.BlockSpec((1, 1, 2 * TOP_K * tm), lambda i: (f(i), 0, 0),
                                      memory_space=pltpu.SMEM)
    in_specs = [
        smem_pos(lambda i: i),
        smem_pos(lambda i: jnp.minimum(i + 1, steps - 1)),
    ] + _token_specs(tm2, n_prompt, b_row0) + [
        row(TOP_K),
        pl.BlockSpec((tm2, PLE_DIM),
                     lambda i: (layer * prompt_steps + jnp.minimum(i, prompt_steps - 1), 0)),
        pl.BlockSpec((tm2, PLE_DIM),
                     lambda i: (layer * sample_steps + jnp.maximum(i - prompt_steps, 0), 0)),
        _full((D_MODEL, D_MODEL)),
        _full((PLE_DIM, D_MODEL)),
        _full((1, D_MODEL)),
        _full((1, D_MODEL)),
    ]
    args = [pos3, pos3, xa, xb, gate_t, p_prompt, p_sample, w_gate, w_proj, g2, b2]
    if split_out:
        out_specs = [
            pl.BlockSpec((tm2, D_MODEL), lambda i: (jnp.minimum(i, prompt_steps - 1), 0)),
            pl.BlockSpec((tm2, D_MODEL), lambda i: (jnp.maximum(i - prompt_steps, 0), 0)),
        ]
        out_shape = [jax.ShapeDtypeStruct((n_prompt, D_MODEL), F32),
                     jax.ShapeDtypeStruct((n - n_prompt, D_MODEL), F32)]
    else:
        out_specs = [row(D_MODEL)]
        out_shape = [jax.ShapeDtypeStruct((n, D_MODEL), F32)]
    if emit_kv:
        in_specs += [_full((1, D_MODEL)), _full((1, D_MODEL)), _full((D_MODEL, 2 * KV_DIM))]
        args += list(kv_params)
        out_specs.append(row(2 * KV_DIM))
        out_shape.append(jax.ShapeDtypeStruct((n, 2 * KV_DIM), F32))
    if fuse_gmlp:
        in_specs += _gmlp_weight_specs(lambda i: jnp.where(i >= prompt_steps, 1, 0))
        args += list(gmlp_weights)
        out_specs.append(pl.BlockSpec((tm2, D_MODEL),
                                      lambda i: (jnp.maximum(i - (prompt_steps - 1), 0), 0)))
        out_shape.append(jax.ShapeDtypeStruct(((sample_steps + 1) * tm2, D_MODEL), F32))
    in_specs.append(pl.BlockSpec(memory_space=pl.ANY))
    args.append(ys)
    outs = pl.pallas_call(
        functools.partial(_combine_kernel, tm=tm, prompt_steps=prompt_steps, emit_kv=emit_kv,
                          fuse_gmlp=fuse_gmlp, split_out=split_out),
        grid=(steps,),
        in_specs=in_specs,
        out_specs=out_specs,
        out_shape=out_shape,
        scratch_shapes=[pltpu.VMEM((TOP_K, ROW_TILE * tm, LANES), F32),
                        pltpu.VMEM((TOP_K, ROW_TILE * tm, LANES), F32),
                        pltpu.SemaphoreType.DMA((2,))],
        compiler_params=pltpu.CompilerParams(
            dimension_semantics=("arbitrary",), vmem_limit_bytes=VMEM_LIMIT),
        name="moe_combine",
    )(*args)
    if fuse_gmlp:
        return outs[0], outs[1][tm2:]
    return outs if (emit_kv or split_out) else (outs[0], None)


def _tile_pos(pos, tm):
    n = pos.shape[1]
    return pos.reshape(TOP_K, n // tm, tm).transpose(1, 0, 2).reshape(n // tm, 1, TOP_K * tm)


def _pair_pos(pos, tm):
    n = pos.shape[1]
    p = pos.reshape(TOP_K, n // (2 * tm), 2, tm).transpose(1, 2, 0, 3)
    return p.reshape(n // (2 * tm), 1, 2 * TOP_K * tm)


def _moe_layer(src, n, xs_buf, p_prompt, p_sample, n_prompt, layer, router_wt, router_b, tri,
               w_up, b_up, w_dn, b_dn, w_gate, w_proj, g2, b2, kv_params=None, gmlp_weights=None,
               split_out=False):
    idx, gate, rank, counts = _route(src, n, n_prompt, router_wt, router_b, tri)
    padded = ((counts + TM_EXPERT - 1) // TM_EXPERT) * TM_EXPERT
    ends = jnp.cumsum(padded)
    starts = ends - padded
    onehot = idx[:, :, None] == jnp.arange(N_EXPERTS, dtype=jnp.int32)
    pos = jnp.sum(jnp.where(onehot, starts, 0), axis=-1) + rank
    n_tiles = xs_buf.shape[0] // (ROW_TILE * TM_EXPERT)
    n_valid = (ends[-1] // TM_EXPERT).astype(jnp.int32)
    tile_start = jnp.minimum(jnp.arange(n_tiles, dtype=jnp.int32), n_valid - 1) * TM_EXPERT
    tile_expert = jnp.sum(tile_start[:, None] >= ends[None, :], axis=1).astype(jnp.int32)
    xs_buf = _dispatch(src, n, n_prompt, _tile_pos(pos, TM_DISPATCH), xs_buf)
    ys = _experts(xs_buf, tile_expert, n_valid.reshape(1), w_up, b_up, w_dn, b_dn, layer)
    x3, extra = _combine(src, n, _pair_pos(pos, TM_TOKEN), gate.T, ys, p_prompt, p_sample, layer,
                         n_prompt, w_gate, w_proj, g2, b2, kv_params, gmlp_weights, split_out)
    return x3, extra, xs_buf


def _alibi_slopes():
    h = jnp.arange(1, N_HEADS + 1, dtype=F32)
    return 2.0 ** (-8.0 * h / N_HEADS)


def _prompt_bias():
    qi = jnp.arange(WINDOW)[:, None]
    kj = jnp.arange(2 * WINDOW)[None, :]
    dist = qi + WINDOW - kj
    valid = (dist >= 0) & (dist < WINDOW)
    slopes = _alibi_slopes()[:, None, None]
    tables = []
    for first in (True, False):
        ok = valid & (kj >= WINDOW) if first else valid
        tables.append(jnp.where(ok[None], -slopes * dist.astype(F32)[None], NEG_INF))
    return jnp.stack(tables).astype(F32)


def _sample_tables(t, sinks):
    g = jnp.arange(Q_PER_KV)[:, None, None]
    kh = jnp.arange(N_KV_HEADS)[None, :, None]
    tok = jnp.arange(t)[None, None, :]
    head = jnp.broadcast_to(kh * Q_PER_KV + g, (Q_PER_KV, N_KV_HEADS, t)).reshape(-1)
    tok = jnp.broadcast_to(tok, (Q_PER_KV, N_KV_HEADS, t)).reshape(-1)
    kj = jnp.arange(KEY_PAD)[None, :]
    dist = tok[:, None] + WINDOW - kj
    valid = (dist >= 0) & (dist < WINDOW) & (kj < WINDOW + t)
    slopes = _alibi_slopes()[head][:, None]
    bias = jnp.where(valid, -slopes * dist.astype(F32), NEG_INF).astype(F32)
    sink_col = sinks[head][:, None].astype(F32)
    row_kh = jnp.repeat(jnp.arange(N_KV_HEADS), t)[:, None]
    lane_kh = (jnp.arange(KV_DIM) // HEAD_DIM)[None, :]
    mask = (row_kh == lane_kh).astype(F32)
    return bias, sink_col, mask


def _gkd_cols(w):
    r = w.shape[0]
    w4 = w.reshape(r, N_KV_HEADS, Q_PER_KV, HEAD_DIM)
    return w4.transpose(0, 2, 1, 3).reshape(r, N_HEADS * HEAD_DIM)


def _spatial_tables(w_s, b_s, t):
    tril = jnp.tril(jnp.ones((CHUNK, CHUNK), dtype=bool))
    ws_p = jnp.where(tril[None], w_s, 0.0)
    bs_p = jnp.repeat(b_s.T, GROUP_DIM_A, axis=1)
    r = jnp.arange(CHUNK)
    same = (r[:, None] // t) == (r[None, :] // t)
    small = jnp.where(tril[None, :t, :t], w_s[:, :t, :t], 0.0)
    ws_s = jnp.where(same[None], jnp.tile(small, (1, CHUNK // t, CHUNK // t)), 0.0)
    bs_s = jnp.repeat(jnp.tile(b_s[:, :t], (1, CHUNK // t)).T, GROUP_DIM_A, axis=1)
    return jnp.stack([ws_p, ws_s]).astype(BF16), jnp.stack([bs_p, bs_s]).astype(F32)


def kernel(x_prompt, x_sample, cache_k, cache_v, p_prompt, p_sample, ln1_g, ln1_b, ln2_g, ln2_b,
           a_w_in, a_b_in, a_ln_g, a_ln_b, a_w_s, a_b_s, a_w_out, kv_ln_g, kv_ln_b, w_kv,
           b_w_q, b_sinks, b_w_o, router_w, router_b, exp_w_up, exp_b_up, exp_w_dn, exp_b_dn,
           ple_w_proj, ple_w_gate):
    bsz, seq, _ = x_prompt.shape
    n_seq, t, _ = x_sample.shape
    n_p = bsz * seq
    n_s = n_seq * t
    n = n_p + n_s
    depth = ln1_g.shape[0]
    n_a = a_w_in.shape[0]

    xp2 = x_prompt.reshape(n_p, D_MODEL)
    xs2 = x_sample.reshape(n_s, D_MODEL)
    pp2 = p_prompt.reshape(depth * n_p, PLE_DIM)
    ps2 = p_sample.reshape(depth * n_s, PLE_DIM)
    row = lambda v: v.reshape(1, -1).astype(F32)

    xs_buf = jnp.zeros((ROW_TILE * (n * TOP_K + N_EXPERTS * TM_EXPERT), LANES), F32)
    tri = (jnp.arange(TM_ROUTER)[:, None] <= jnp.arange(TM_ROUTER)[None, :]).astype(BF16)
    b_up4 = exp_b_up.reshape(depth, N_EXPERTS, 1, 2 * D_EXPERT)
    b_dn4 = exp_b_dn.reshape(depth, N_EXPERTS, 1, D_MODEL)
    prompt_bias = _prompt_bias()

    def gmlp_weights(i):
        ws2, bs2 = _spatial_tables(a_w_s[i], a_b_s[i], t)
        return (a_w_in[i].astype(BF16), row(a_b_in[i]), row(a_ln_g[i]), row(a_ln_b[i]), ws2, bs2,
                a_w_out[i].astype(BF16), row(ln1_g[i]), row(ln1_b[i]))

    chunk_v = []
    kv_params = (row(kv_ln_g), row(kv_ln_b), w_kv.astype(BF16))
    x = x1_next = x_out_p = x_out_s = None
    kp = vp = k_buf = v_buf = None
    kpad = vpad = kbuf_pad = vbuf_pad = None
    for i in range(depth):
        fuse_next = i + 1 < n_a
        last = i == depth - 1
        if i < n_a:
            if i == 0:
                x1, v_rows = _gmlp_layer(xp2, xs2, 0, n_p, n_s, gmlp_weights(0))
                chunk_v.append(v_rows.reshape(n_seq, t, D_MODEL))
            else:
                x1 = x1_next
            src = (x1, x1, n_p)
        else:
            j = i - n_a
            wq = b_w_q[j].astype(BF16)
            wo = b_w_o[j].astype(BF16)
            x1_p = _swa_prompt_layer(x, bsz, seq, kpad, vpad, prompt_bias,
                                     b_sinks[j].astype(F32), wq, wo, row(ln1_g[i]), row(ln1_b[i]))
            s_bias, s_sink, s_mask = _sample_tables(t, b_sinks[j])
            x1_s = _swa_sample_layer(x, n_p, n_seq, t, kbuf_pad, vbuf_pad, s_bias, s_sink, s_mask,
                                     _gkd_cols(wq), _gkd_cols(wo.T).T, row(ln1_g[i]), row(ln1_b[i]))
            src = (x1_p, x1_s, 0)
        out, extra, xs_buf = _moe_layer(
            src, n, xs_buf, pp2, ps2, n_p, i, router_w[i].T, router_b[i].reshape(N_EXPERTS, 1), tri,
            exp_w_up, b_up4, exp_w_dn, b_dn4, ple_w_gate[i].astype(BF16),
            ple_w_proj[i].astype(BF16), row(ln2_g[i]), row(ln2_b[i]),
            kv_params if i == n_a - 1 else None,
            gmlp_weights(i + 1) if fuse_next else None,
            split_out=last and i != n_a - 1 and not fuse_next)
        if fuse_next:
            x1_next = out
            chunk_v.append(extra.reshape(n_seq, t, D_MODEL))
        elif last and i != n_a - 1:
            x_out_p, x_out_s = out, extra
        else:
            x, kv = out, extra
        if last and i == n_a - 1:
            x_out_p, x_out_s = x[:n_p], x[n_p:]
        if i == n_a - 1:
            k_all, v_all = kv[:, :KV_DIM], kv[:, KV_DIM:]
            kp = k_all[:n_p].reshape(bsz, seq, KV_DIM)
            vp = v_all[:n_p].reshape(bsz, seq, KV_DIM)
            k_buf = jnp.concatenate([cache_k.reshape(n_seq, WINDOW, KV_DIM),
                                     k_all[n_p:].reshape(n_seq, t, KV_DIM)], axis=1)
            v_buf = jnp.concatenate([cache_v.reshape(n_seq, WINDOW, KV_DIM),
                                     v_all[n_p:].reshape(n_seq, t, KV_DIM)], axis=1)
            front = ((0, 0), (WINDOW, 0), (0, 0))
            kpad = jnp.pad(kp, front).astype(BF16)
            vpad = jnp.pad(vp, front).astype(BF16)
            tail = ((0, 0), (0, KEY_PAD - WINDOW - t), (0, 0))
            kbuf_pad = jnp.pad(k_buf, tail).astype(BF16)
            vbuf_pad = jnp.pad(v_buf, tail).astype(BF16)

    heads = (N_KV_HEADS, HEAD_DIM)
    return (x_out_p.reshape(bsz, seq, D_MODEL),
            x_out_s.reshape(n_seq, t, D_MODEL),
            jnp.stack(chunk_v),
            kp[:, -WINDOW:].reshape(bsz, WINDOW, *heads),
            vp[:, -WINDOW:].reshape(bsz, WINDOW, *heads),
            k_buf[:, -WINDOW:].reshape(n_seq, WINDOW, *heads),
            v_buf[:, -WINDOW:].reshape(n_seq, WINDOW, *heads))
```

```python
import functools

import jax
import jax.numpy as jnp
from jax import lax
from jax.experimental import pallas as pl
from jax.experimental.pallas import tpu as pltpu

F32 = jnp.float32
BF16 = jnp.bfloat16

D_MODEL = 1024
DEPTH = 4
N_A_LAYERS = 2
CHUNK = 128
N_GROUPS_A = 8
GROUP_DIM_A = D_MODEL // N_GROUPS_A
HEAD_DIM = 64
N_HEADS = 16
N_KV_HEADS = 4
Q_PER_KV = 4
KV_DIM = N_KV_HEADS * HEAD_DIM
WINDOW = 128
N_EXPERTS = 32
TOP_K = 4
D_EXPERT = 1024
SWIGLU_LIMIT = 7.0
SWIGLU_ALPHA = 1.702
PLE_DIM = 256
DEEPNORM_ALPHA = (2 * DEPTH) ** 0.25
LN_EPS = 1e-5
NEG_INF = -1e30

TM_TOKEN = 256
TM_MIXER = 512
TM_DISPATCH = 512
DISPATCH_CHUNK = 128
TM_ROUTER = 512
TM_EXPERT = 512
SAMPLE_SEQ_BLOCK = 16
KEY_PAD = 256
VMEM_LIMIT = 56 * 1024 * 1024


def _ln(x, g, b):
    mu = jnp.mean(x, axis=-1, keepdims=True)
    xc = x - mu
    var = jnp.mean(xc * xc, axis=-1, keepdims=True)
    return xc * lax.rsqrt(var + LN_EPS) * g + b


def _dot(a, b):
    return jnp.dot(a, b, preferred_element_type=F32)


def _dot_nt(a, b):
    return lax.dot_general(a, b, (((1,), (1,)), ((), ())), preferred_element_type=F32)


def _full(shape):
    n = len(shape)
    return pl.BlockSpec(shape, lambda *_: (0,) * n)


def _gmlp_math(x, win_ref, bin_ref, lng_ref, lnb_ref, ws_ref, bs_ref, wout_ref, g1_ref, b1_ref):
    h = _dot(x.astype(BF16), win_ref[...]) + bin_ref[...]
    h = 0.5 * h * (1.0 + lax.erf(h * (2.0 ** -0.5)))
    u = h[:, :D_MODEL]
    v = _ln(h[:, D_MODEL:], lng_ref[...], lnb_ref[...])
    vb = v.astype(BF16)
    rows = []
    for c in range(x.shape[0] // CHUNK):
        cols = []
        for g in range(N_GROUPS_A):
            blk = vb[c * CHUNK:(c + 1) * CHUNK, g * GROUP_DIM_A:(g + 1) * GROUP_DIM_A]
            cols.append(_dot(ws_ref[g], blk))
        rows.append(jnp.concatenate(cols, axis=1) + bs_ref[...])
    s = jnp.concatenate(rows, axis=0)
    gated = (u * s).astype(BF16)
    m = _dot(gated, wout_ref[...])
    return _ln(DEEPNORM_ALPHA * x + m, g1_ref[...], b1_ref[...]), v


ZERO_ROWS = 4096


def _gmlp_kernel(xa_ref, xb_ref, *refs, p_tiles, zero_chunks, chunks_per_step):
    *w_refs, x1_ref, v_ref, zs_hbm, zbuf, sem = refs
    i = pl.program_id(0)

    @pl.when(i == 0)
    def _():
        zbuf[...] = jnp.zeros_like(zbuf)

    def zero_copy(c):
        start = pl.multiple_of(c * ZERO_ROWS, ZERO_ROWS)
        return pltpu.make_async_copy(zbuf, zs_hbm.at[pl.ds(start, ZERO_ROWS), :], sem)

    def for_my_chunks(fn):
        for j in range(chunks_per_step):
            c = i * chunks_per_step + j

            @pl.when(c < zero_chunks)
            def _():
                fn(zero_copy(c))

    for_my_chunks(lambda cp: cp.start())
    x = jnp.where(i >= p_tiles, xb_ref[...], xa_ref[...])
    x1_ref[...], v_ref[...] = _gmlp_math(x, *w_refs)
    for_my_chunks(lambda cp: cp.wait())


def _gmlp_weight_specs(kind):
    return [
        _full((D_MODEL, 2 * D_MODEL)),
        _full((1, 2 * D_MODEL)),
        _full((1, D_MODEL)),
        _full((1, D_MODEL)),
        pl.BlockSpec((None, N_GROUPS_A, CHUNK, CHUNK), lambda i: (kind(i), 0, 0, 0)),
        pl.BlockSpec((None, CHUNK, D_MODEL), lambda i: (kind(i), 0, 0)),
        _full((D_MODEL, D_MODEL)),
        _full((1, D_MODEL)),
        _full((1, D_MODEL)),
    ]


def _gmlp_layer(xa, xb, xb_row0, n_prompt, n_sample, weights, zero_rows, tm=TM_MIXER):
    n = n_prompt + n_sample
    p_tiles = n_prompt // tm
    n_tiles = n // tm
    s_tiles = n_tiles - p_tiles
    b_tile0 = xb_row0 // tm
    kind = lambda i: jnp.where(i >= p_tiles, 1, 0)
    assert zero_rows % ZERO_ROWS == 0
    zero_chunks = zero_rows // ZERO_ROWS
    x1, v, zs = pl.pallas_call(
        functools.partial(_gmlp_kernel, p_tiles=p_tiles, zero_chunks=zero_chunks,
                          chunks_per_step=-(-zero_chunks // n_tiles)),
        grid=(n_tiles,),
        in_specs=[
            pl.BlockSpec((tm, D_MODEL), lambda i: (jnp.minimum(i, p_tiles - 1), 0)),
            pl.BlockSpec((tm, D_MODEL), lambda i: (jnp.maximum(i - p_tiles, 0) + b_tile0, 0)),
        ] + _gmlp_weight_specs(kind),
        out_specs=[
            pl.BlockSpec((tm, D_MODEL), lambda i: (i, 0)),
            pl.BlockSpec((tm, D_MODEL), lambda i: (jnp.maximum(i - (p_tiles - 1), 0), 0)),
            pl.BlockSpec(memory_space=pl.ANY),
        ],
        out_shape=[
            jax.ShapeDtypeStruct((n, D_MODEL), F32),
            jax.ShapeDtypeStruct(((s_tiles + 1) * tm, D_MODEL), F32),
            jax.ShapeDtypeStruct((zero_rows, LANES), F32),
        ],
        scratch_shapes=[pltpu.VMEM((ZERO_ROWS, LANES), F32), pltpu.SemaphoreType.DMA],
        compiler_params=pltpu.CompilerParams(
            dimension_semantics=("arbitrary",), vmem_limit_bytes=VMEM_LIMIT),
        name="gmlp_mixer",
    )(xa, xb, *weights)
    return x1, v[tm:], zs


def _swa_prompt_kernel(sink_ref, x_ref, k_ref, v_ref, bias_ref, wq_ref, wo_ref, g1_ref, b1_ref,
                       x1_ref, *, tq):
    i = pl.program_id(1)
    x = x_ref[...]
    qb = (_dot(x.astype(BF16), wq_ref[...]) * (HEAD_DIM ** -0.5)).astype(BF16)
    blocks = []
    for j in range(tq // WINDOW):
        blk = i * (tq // WINDOW) + j
        kstart = pl.multiple_of(blk * WINDOW, WINDOW)
        kb = k_ref[pl.ds(kstart, 2 * WINDOW), :]
        vb = v_ref[pl.ds(kstart, 2 * WINDOW), :]
        sel = jnp.where(blk == 0, 0, 1)
        outs = []
        for h in range(N_HEADS):
            kh = h // Q_PER_KV
            qh = qb[j * WINDOW:(j + 1) * WINDOW, h * HEAD_DIM:(h + 1) * HEAD_DIM]
            s = _dot_nt(qh, kb[:, kh * HEAD_DIM:(kh + 1) * HEAD_DIM])
            l = s + bias_ref[sel, h]
            sink = sink_ref[h]
            m = jnp.maximum(jnp.max(l, axis=1, keepdims=True), sink)
            p = jnp.exp(l - m)
            den = jnp.sum(p, axis=1, keepdims=True) + jnp.exp(sink - m)
            o = _dot(p.astype(BF16), vb[:, kh * HEAD_DIM:(kh + 1) * HEAD_DIM])
            outs.append(o / den)
        blocks.append(jnp.concatenate(outs, axis=1))
    attn = jnp.concatenate(blocks, axis=0)
    m_out = _dot(attn.astype(BF16), wo_ref[...])
    x1_ref[...] = _ln(DEEPNORM_ALPHA * x + m_out, g1_ref[...], b1_ref[...])


def _swa_prompt_layer(x, bsz, seq, kpad, vpad, bias, sinks, w_q, w_o, g1, b1, tq=TM_MIXER):
    nq = seq // tq
    return pl.pallas_call(
        functools.partial(_swa_prompt_kernel, tq=tq),
        grid=(bsz, nq),
        in_specs=[
            pl.BlockSpec(memory_space=pltpu.SMEM),
            pl.BlockSpec((tq, D_MODEL), lambda b, i: (b * nq + i, 0)),
            pl.BlockSpec((None, seq + WINDOW, KV_DIM), lambda b, i: (b, 0, 0)),
            pl.BlockSpec((None, seq + WINDOW, KV_DIM), lambda b, i: (b, 0, 0)),
            _full((2, N_HEADS, WINDOW, 2 * WINDOW)),
            _full((D_MODEL, D_MODEL)),
            _full((D_MODEL, D_MODEL)),
            _full((1, D_MODEL)),
            _full((1, D_MODEL)),
        ],
        out_specs=pl.BlockSpec((tq, D_MODEL), lambda b, i: (b * nq + i, 0)),
        out_shape=jax.ShapeDtypeStruct((bsz * seq, D_MODEL), F32),
        compiler_params=pltpu.CompilerParams(
            dimension_semantics=("arbitrary", "arbitrary"), vmem_limit_bytes=VMEM_LIMIT),
        name="swa_prompt",
    )(sinks, x, kpad, vpad, bias, w_q, w_o, g1, b1)


def _swa_sample_kernel(x_ref, k_ref, v_ref, bias_ref, sink_ref, mask_ref, wq_ref, wo_ref,
                       g1_ref, b1_ref, x1_ref, *, sb, t):
    x = x_ref[...]
    q = _dot(x.astype(BF16), wq_ref[...]) * (HEAD_DIM ** -0.5)
    mask = mask_ref[...]
    bias = bias_ref[...]
    sink = sink_ref[...]
    outs = []
    for s in range(sb):
        qs = q[s * t:(s + 1) * t, :]
        parts = []
        for g in range(Q_PER_KV):
            qg = qs[:, g * KV_DIM:(g + 1) * KV_DIM]
            parts.append(jnp.concatenate([qg] * N_KV_HEADS, axis=0) * mask)
        qexp = jnp.concatenate(parts, axis=0).astype(BF16)
        l = _dot_nt(qexp, k_ref[s]) + bias
        m = jnp.maximum(jnp.max(l, axis=1, keepdims=True), sink)
        p = jnp.exp(l - m)
        den = jnp.sum(p, axis=1, keepdims=True) + jnp.exp(sink - m)
        r = _dot(p.astype(BF16), v_ref[s]) / den
        og = []
        for g in range(Q_PER_KV):
            rg = r[g * N_KV_HEADS * t:(g + 1) * N_KV_HEADS * t, :] * mask
            acc = rg[0:t]
            for kh in range(1, N_KV_HEADS):
                acc = acc + rg[kh * t:(kh + 1) * t]
            og.append(acc)
        outs.append(jnp.concatenate(og, axis=1))
    attn = jnp.concatenate(outs, axis=0)
    m_out = _dot(attn.astype(BF16), wo_ref[...])
    x1_ref[...] = _ln(DEEPNORM_ALPHA * x + m_out, g1_ref[...], b1_ref[...])


def _swa_sample_layer(x, row0, n_seq, t, kbuf, vbuf, bias, sink_col, mask, w_q, w_o, g1, b1,
                      sb=SAMPLE_SEQ_BLOCK):
    rows = sb * t
    hr = N_HEADS * t
    blk0 = row0 // rows
    return pl.pallas_call(
        functools.partial(_swa_sample_kernel, sb=sb, t=t),
        grid=(n_seq // sb,),
        in_specs=[
            pl.BlockSpec((rows, D_MODEL), lambda i: (blk0 + i, 0)),
            pl.BlockSpec((sb, KEY_PAD, KV_DIM), lambda i: (i, 0, 0)),
            pl.BlockSpec((sb, KEY_PAD, KV_DIM), lambda i: (i, 0, 0)),
            _full((hr, KEY_PAD)),
            _full((hr, 1)),
            _full((N_KV_HEADS * t, KV_DIM)),
            _full((D_MODEL, D_MODEL)),
            _full((D_MODEL, D_MODEL)),
            _full((1, D_MODEL)),
            _full((1, D_MODEL)),
        ],
        out_specs=pl.BlockSpec((rows, D_MODEL), lambda i: (i, 0)),
        out_shape=jax.ShapeDtypeStruct((n_seq * t, D_MODEL), F32),
        compiler_params=pltpu.CompilerParams(
            dimension_semantics=("arbitrary",), vmem_limit_bytes=VMEM_LIMIT),
        name="swa_sample",
    )(x, kbuf, vbuf, bias, sink_col, mask, w_q, w_o, g1, b1)


def _token_specs(tm, n_prompt, b_row0):
    p_steps = n_prompt // tm
    b0 = b_row0 // tm
    return [pl.BlockSpec((tm, D_MODEL), lambda i: (jnp.minimum(i, p_steps - 1), 0)),
            pl.BlockSpec((tm, D_MODEL), lambda i: (jnp.maximum(i - p_steps, 0) + b0, 0))]


def _router_kernel(xa_ref, xb_ref, wt_ref, b_ref, tri_ref, idx_ref, gate_ref, rank_ref, cnt_ref,
                   carry, *, tm, p_steps):
    i = pl.program_id(0)

    @pl.when(i == 0)
    def _():
        carry[...] = jnp.zeros_like(carry)

    x = jnp.where(i >= p_steps, xb_ref[...], xa_ref[...])
    logits = lax.dot_general(wt_ref[...], x, (((1,), (1,)), ((), ())),
                             precision=lax.Precision.HIGHEST,
                             preferred_element_type=F32) + b_ref[...]
    eidx = lax.broadcasted_iota(jnp.int32, (N_EXPERTS, tm), 0)
    l = logits
    vals, idxs, sels = [], [], []
    for _ in range(TOP_K):
        m = jnp.max(l, axis=0, keepdims=True)
        idx = jnp.min(jnp.where(l == m, eidx, N_EXPERTS), axis=0, keepdims=True)
        sel = eidx == idx
        vals.append(m)
        idxs.append(idx)
        sels.append(sel)
        l = jnp.where(sel, -jnp.inf, l)
    exps = [jnp.exp(v - vals[0]) for v in vals]
    den = exps[0] + exps[1] + exps[2] + exps[3]
    onehot = jnp.zeros((N_EXPERTS, tm), F32)
    for sel in sels:
        onehot = onehot + sel.astype(F32)
    incl = _dot(onehot.astype(BF16), tri_ref[...])
    excl = incl - onehot + carry[...]
    for k in range(TOP_K):
        idx_ref[k:k + 1, :] = idxs[k]
        gate_ref[k:k + 1, :] = exps[k] / den
        rank = jnp.sum(jnp.where(sels[k], excl, 0.0), axis=0, keepdims=True)
        rank_ref[k:k + 1, :] = rank.astype(jnp.int32)
    carry[...] = carry[...] + jnp.sum(onehot, axis=1, keepdims=True)
    cnt_ref[...] = jnp.broadcast_to(carry[...], cnt_ref.shape)


def _route(src, n, n_prompt, router_wt, router_b, tri, tm=TM_ROUTER):
    xa, xb, b_row0 = src
    spec_kn = pl.BlockSpec((TOP_K, tm), lambda i: (0, i))
    idx, gate, rank, cnt = pl.pallas_call(
        functools.partial(_router_kernel, tm=tm, p_steps=n_prompt // tm),
        grid=(n // tm,),
        in_specs=_token_specs(tm, n_prompt, b_row0) + [
            _full((N_EXPERTS, D_MODEL)),
            _full((N_EXPERTS, 1)),
            _full((tm, tm)),
        ],
        out_specs=[spec_kn, spec_kn, spec_kn, _full((N_EXPERTS, 128))],
        out_shape=[
            jax.ShapeDtypeStruct((TOP_K, n), jnp.int32),
            jax.ShapeDtypeStruct((TOP_K, n), F32),
            jax.ShapeDtypeStruct((TOP_K, n), jnp.int32),
            jax.ShapeDtypeStruct((N_EXPERTS, 128), F32),
        ],
        scratch_shapes=[pltpu.VMEM((N_EXPERTS, 1), F32)],
        compiler_params=pltpu.CompilerParams(
            dimension_semantics=("arbitrary",), vmem_limit_bytes=VMEM_LIMIT),
        name="moe_router",
    )(xa, xb, router_wt, router_b, tri)
    return idx, gate, rank, cnt[:, 0].astype(jnp.int32)


ROW_TILE = 8
LANES = D_MODEL // ROW_TILE


def _to_row_tiles(ref, x):
    rows = x.shape[0]
    for c in range(ROW_TILE):
        ref[pl.ds(c, rows, stride=ROW_TILE), :] = x[:, c * LANES:(c + 1) * LANES]


def _from_row_tiles(ref, rows):
    return jnp.concatenate([ref[pl.ds(c, rows, stride=ROW_TILE), :] for c in range(ROW_TILE)], axis=1)


def _row_tile(ref, p):
    start = p * ROW_TILE
    if not isinstance(p, int):
        start = pl.multiple_of(start, ROW_TILE)
    return ref.at[pl.ds(start, ROW_TILE), :]


def _row_copy_out(x_ref, xs_hbm, sem, t, p):
    return pltpu.make_async_copy(_row_tile(x_ref, t), _row_tile(xs_hbm, p), sem)


def _dispatch_kernel(pos_ref, xa_ref, xb_ref, xs_in_hbm, xs_hbm, x_ref, sem, *, tm, chunk, p_steps):
    del xs_in_hbm
    _to_row_tiles(x_ref, jnp.where(pl.program_id(0) >= p_steps, xb_ref[...], xa_ref[...]))

    def issue(c, carry):
        t0 = c * chunk
        for t in range(chunk):
            for k in range(TOP_K):
                _row_copy_out(x_ref, xs_hbm, sem, t0 + t,
                              pos_ref[0, 0, k * tm + t0 + t]).start(priority=k % 2)
        return carry

    lax.fori_loop(0, tm // chunk, issue, 0)
    for _ in range(TOP_K):
        pltpu.make_async_copy(x_ref, xs_hbm.at[pl.ds(0, ROW_TILE * tm), :], sem).wait()


def _dispatch(src, n, n_prompt, pos3, xs_buf, tm=TM_DISPATCH, chunk=DISPATCH_CHUNK):
    xa, xb, b_row0 = src
    return pl.pallas_call(
        functools.partial(_dispatch_kernel, tm=tm, chunk=chunk, p_steps=n_prompt // tm),
        grid=(n // tm,),
        in_specs=[
            pl.BlockSpec((1, 1, TOP_K * tm), lambda i: (i, 0, 0), memory_space=pltpu.SMEM),
        ] + _token_specs(tm, n_prompt, b_row0) + [
            pl.BlockSpec(memory_space=pl.ANY),
        ],
        out_specs=pl.BlockSpec(memory_space=pl.ANY),
        out_shape=jax.ShapeDtypeStruct(xs_buf.shape, xs_buf.dtype),
        scratch_shapes=[pltpu.VMEM((ROW_TILE * tm, LANES), F32), pltpu.SemaphoreType.DMA],
        input_output_aliases={3: 0},
        compiler_params=pltpu.CompilerParams(
            dimension_semantics=("arbitrary",), vmem_limit_bytes=VMEM_LIMIT),
        name="moe_dispatch",
    )(pos3, xa, xb, xs_buf)


def _expert_kernel(te_ref, nv_ref, xs_ref, wup_ref, bup_ref, wdn_ref, bdn_ref, y_ref):
    del te_ref
    i = pl.program_id(0)

    @pl.when(i < nv_ref[0])
    def _():
        x = _from_row_tiles(xs_ref, y_ref.shape[0] // ROW_TILE)
        h = _dot(x, wup_ref[...]) + bup_ref[...]
        glu = jnp.minimum(h[:, :D_EXPERT], SWIGLU_LIMIT)
        lin = jnp.clip(h[:, D_EXPERT:], -SWIGLU_LIMIT, SWIGLU_LIMIT)
        act = glu * jax.nn.sigmoid(SWIGLU_ALPHA * glu) * (lin + 1.0)
        _to_row_tiles(y_ref, _dot(act, wdn_ref[...]) + bdn_ref[...])

    @pl.when(i >= nv_ref[0])
    def _():
        y_ref[...] = jnp.zeros_like(y_ref)


def _experts(xs, tile_expert, n_valid, w_up, b_up, w_dn, b_dn, layer, tm=TM_EXPERT):
    rows = xs.shape[0] // ROW_TILE
    row_map = lambda i, te, nv: (jnp.minimum(i, nv[0] - 1), 0)
    wspec = lambda shape: pl.BlockSpec((None, None) + shape, lambda i, te, nv: (layer, te[i], 0, 0))
    grid_spec = pltpu.PrefetchScalarGridSpec(
        num_scalar_prefetch=2,
        grid=(rows // tm,),
        in_specs=[
            pl.BlockSpec((ROW_TILE * tm, LANES), row_map),
            wspec((D_MODEL, 2 * D_EXPERT)),
            wspec((1, 2 * D_EXPERT)),
            wspec((D_EXPERT, D_MODEL)),
            wspec((1, D_MODEL)),
        ],
        out_specs=pl.BlockSpec((ROW_TILE * tm, LANES), lambda i, te, nv: (i, 0)),
    )
    return pl.pallas_call(
        _expert_kernel,
        grid_spec=grid_spec,
        out_shape=jax.ShapeDtypeStruct(xs.shape, F32),
        compiler_params=pltpu.CompilerParams(
            dimension_semantics=("arbitrary",), vmem_limit_bytes=VMEM_LIMIT),
        name="moe_experts",
    )(tile_expert, n_valid, xs, w_up, b_up, w_dn, b_dn)


def _row_copy_in(ys_hbm, ybuf, sem, k, t, p):
    return pltpu.make_async_copy(_row_tile(ys_hbm, p), _row_tile(ybuf.at[k], t), sem)


def _combine_kernel(pos_ref, posn_ref, x1a_ref, x1b_ref, gate_ref, pp_ref, ps_ref, wg_ref, wp_ref,
                    g2_ref, b2_ref, *rest, tm, prompt_steps, emit_kv, fuse_gmlp, split_out):
    kv_ref = v_ref = x3s_ref = None
    if emit_kv:
        kvg_ref, kvb_ref, wkv_ref, ys_hbm, x3_ref, kv_ref, ybuf_a, ybuf_b, sem = rest
    elif fuse_gmlp:
        *gmlp_refs, ys_hbm, x3_ref, v_ref, ybuf_a, ybuf_b, sem = rest
    elif split_out:
        ys_hbm, x3_ref, x3s_ref, ybuf_a, ybuf_b, sem = rest
    else:
        ys_hbm, x3_ref, ybuf_a, ybuf_b, sem = rest
    s = pl.program_id(0)
    is_sample = s >= prompt_steps

    def issue(pref, half, ybuf, sm):
        for t in range(tm):
            for k in range(TOP_K):
                _row_copy_in(ys_hbm, ybuf, sm, k, t,
                             pref[0, 0, (half * TOP_K + k) * tm + t]).start(priority=k % 2)

    def wait_all(ybuf, sm):
        for k in range(TOP_K):
            pltpu.make_async_copy(ys_hbm.at[pl.ds(0, ROW_TILE * tm), :], ybuf.at[k], sm).wait()

    def half_math(half, ybuf):
        rows = slice(half * tm, (half + 1) * tm)
        gate = gate_ref[rows, :]
        f = gate[:, 0:1] * _from_row_tiles(ybuf.at[0], tm)
        for k in range(1, TOP_K):
            f = f + gate[:, k:k + 1] * _from_row_tiles(ybuf.at[k], tm)
        x1 = jnp.where(is_sample, x1b_ref[rows, :], x1a_ref[rows, :])
        x2 = _ln(DEEPNORM_ALPHA * x1 + f, g2_ref[...], b2_ref[...])
        p = jnp.where(is_sample, ps_ref[rows, :], pp_ref[rows, :])
        pg = jax.nn.sigmoid(_dot(x2.astype(BF16), wg_ref[...]))
        pp = _dot(p.astype(BF16), wp_ref[...])
        x3 = x2 + pg * pp
        if fuse_gmlp:
            x3_ref[rows, :], v_ref[rows, :] = _gmlp_math(x3, *gmlp_refs)
        elif split_out:
            @pl.when(is_sample)
            def _():
                x3s_ref[rows, :] = x3

            @pl.when(jnp.logical_not(is_sample))
            def _():
                x3_ref[rows, :] = x3
        else:
            x3_ref[rows, :] = x3
        if emit_kv:
            kv_ref[rows, :] = _dot(_ln(x3, kvg_ref[...], kvb_ref[...]).astype(BF16), wkv_ref[...])

    @pl.when(s == 0)
    def _():
        def first(t, c):
            for k in range(TOP_K):
                _row_copy_in(ys_hbm, ybuf_a, sem.at[0], k, t, pos_ref[0, 0, k * tm + t]).start()
            return c

        lax.fori_loop(0, tm, first, 0)

    wait_all(ybuf_a, sem.at[0])
    issue(pos_ref, 1, ybuf_b, sem.at[1])
    half_math(0, ybuf_a)
    wait_all(ybuf_b, sem.at[1])
    issue(posn_ref, 0, ybuf_a, sem.at[0])
    half_math(1, ybuf_b)

    @pl.when(s == pl.num_programs(0) - 1)
    def _():
        wait_all(ybuf_a, sem.at[0])


def _combine(src, n, pos3, gate_t, ys, p_prompt, p_sample, layer, n_prompt, w_gate, w_proj, g2, b2,
             kv_params=None, gmlp_weights=None, split_out=False, tm=TM_TOKEN):
    xa, xb, b_row0 = src
    emit_kv = kv_params is not None
    fuse_gmlp = gmlp_weights is not None
    assert emit_kv + fuse_gmlp + split_out <= 1
    tm2 = 2 * tm
    steps = n // tm2
    prompt_steps = n_prompt // tm2
    sample_steps = steps - prompt_steps
    row = lambda w: pl.BlockSpec((tm2, w), lambda i: (i, 0))
    smem_pos = lambda f: pl.BlockSpec((1, 1, 2 * TOP_K * tm), lambda i: (f(i), 0, 0),
                                      memory_space=pltpu.SMEM)
    in_specs = [
        smem_pos(lambda i: i),
        smem_pos(lambda i: jnp.minimum(i + 1, steps - 1)),
    ] + _token_specs(tm2, n_prompt, b_row0) + [
        row(TOP_K),
        pl.BlockSpec((tm2, PLE_DIM),
                     lambda i: (layer * prompt_steps + jnp.minimum(i, prompt_steps - 1), 0)),
        pl.BlockSpec((tm2, PLE_DIM),
                     lambda i: (layer * sample_steps + jnp.maximum(i - prompt_steps, 0), 0)),
        _full((D_MODEL, D_MODEL)),
        _full((PLE_DIM, D_MODEL)),
        _full((1, D_MODEL)),
        _full((1, D_MODEL)),
    ]
    args = [pos3, pos3, xa, xb, gate_t, p_prompt, p_sample, w_gate, w_proj, g2, b2]
    if split_out:
        out_specs = [
            pl.BlockSpec((tm2, D_MODEL), lambda i: (jnp.minimum(i, prompt_steps - 1), 0)),
            pl.BlockSpec((tm2, D_MODEL), lambda i: (jnp.maximum(i - prompt_steps, 0), 0)),
        ]
        out_shape = [jax.ShapeDtypeStruct((n_prompt, D_MODEL), F32),
                     jax.ShapeDtypeStruct((n - n_prompt, D_MODEL), F32)]
    else:
        out_specs = [row(D_MODEL)]
        out_shape = [jax.ShapeDtypeStruct((n, D_MODEL), F32)]
    if emit_kv:
        in_specs += [_full((1, D_MODEL)), _full((1, D_MODEL)), _full((D_MODEL, 2 * KV_DIM))]
        args += list(kv_params)
        out_specs.append(row(2 * KV_DIM))
        out_shape.append(jax.ShapeDtypeStruct((n, 2 * KV_DIM), F32))
    if fuse_gmlp:
        in_specs += _gmlp_weight_specs(lambda i: jnp.where(i >= prompt_steps, 1, 0))
        args += list(gmlp_weights)
        out_specs.append(pl.BlockSpec((tm2, D_MODEL),
                                      lambda i: (jnp.maximum(i - (prompt_steps - 1), 0), 0)))
        out_shape.append(jax.ShapeDtypeStruct(((sample_steps + 1) * tm2, D_MODEL), F32))
    in_specs.append(pl.BlockSpec(memory_space=pl.ANY))
    args.append(ys)
    outs = pl.pallas_call(
        functools.partial(_combine_kernel, tm=tm, prompt_steps=prompt_steps, emit_kv=emit_kv,
                          fuse_gmlp=fuse_gmlp, split_out=split_out),
        grid=(steps,),
        in_specs=in_specs,
        out_specs=out_specs,
        out_shape=out_shape,
        scratch_shapes=[pltpu.VMEM((TOP_K, ROW_TILE * tm, LANES), F32),
                        pltpu.VMEM((TOP_K, ROW_TILE * tm, LANES), F32),
                        pltpu.SemaphoreType.DMA((2,))],
        compiler_params=pltpu.CompilerParams(
            dimension_semantics=("arbitrary",), vmem_limit_bytes=VMEM_LIMIT),
        name="moe_combine",
    )(*args)
    if fuse_gmlp:
        return outs[0], outs[1][tm2:]
    return outs if (emit_kv or split_out) else (outs[0], None)


def _tile_pos(pos, tm):
    n = pos.shape[1]
    return pos.reshape(TOP_K, n // tm, tm).transpose(1, 0, 2).reshape(n // tm, 1, TOP_K * tm)


def _pair_pos(pos, tm):
    n = pos.shape[1]
    p = pos.reshape(TOP_K, n // (2 * tm), 2, tm).transpose(1, 2, 0, 3)
    return p.reshape(n // (2 * tm), 1, 2 * TOP_K * tm)


def _moe_layer(src, n, xs_buf, p_prompt, p_sample, n_prompt, layer, router_wt, router_b, tri,
               w_up, b_up, w_dn, b_dn, w_gate, w_proj, g2, b2, kv_params=None, gmlp_weights=None,
               split_out=False):
    idx, gate, rank, counts = _route(src, n, n_prompt, router_wt, router_b, tri)
    padded = ((counts + TM_EXPERT - 1) // TM_EXPERT) * TM_EXPERT
    ends = jnp.cumsum(padded)
    starts = ends - padded
    onehot = idx[:, :, None] == jnp.arange(N_EXPERTS, dtype=jnp.int32)
    pos = jnp.sum(jnp.where(onehot, starts, 0), axis=-1) + rank
    n_tiles = xs_buf.shape[0] // (ROW_TILE * TM_EXPERT)
    n_valid = (ends[-1] // TM_EXPERT).astype(jnp.int32)
    tile_start = jnp.minimum(jnp.arange(n_tiles, dtype=jnp.int32), n_valid - 1) * TM_EXPERT
    tile_expert = jnp.sum(tile_start[:, None] >= ends[None, :], axis=1).astype(jnp.int32)
    xs_buf = _dispatch(src, n, n_prompt, _tile_pos(pos, TM_DISPATCH), xs_buf)
    ys = _experts(xs_buf, tile_expert, n_valid.reshape(1), w_up, b_up, w_dn, b_dn, layer)
    x3, extra = _combine(src, n, _pair_pos(pos, TM_TOKEN), gate.T, ys, p_prompt, p_sample, layer,
                         n_prompt, w_gate, w_proj, g2, b2, kv_params, gmlp_weights, split_out)
    return x3, extra, xs_buf


def _alibi_slopes():
    h = jnp.arange(1, N_HEADS + 1, dtype=F32)
    return 2.0 ** (-8.0 * h / N_HEADS)


def _prompt_bias():
    qi = jnp.arange(WINDOW)[:, None]
    kj = jnp.arange(2 * WINDOW)[None, :]
    dist = qi + WINDOW - kj
    valid = (dist >= 0) & (dist < WINDOW)
    slopes = _alibi_slopes()[:, None, None]
    tables = []
    for first in (True, False):
        ok = valid & (kj >= WINDOW) if first else valid
        tables.append(jnp.where(ok[None], -slopes * dist.astype(F32)[None], NEG_INF))
    return jnp.stack(tables).astype(F32)


def _sample_tables(t, sinks):
    g = jnp.arange(Q_PER_KV)[:, None, None]
    kh = jnp.arange(N_KV_HEADS)[None, :, None]
    tok = jnp.arange(t)[None, None, :]
    head = jnp.broadcast_to(kh * Q_PER_KV + g, (Q_PER_KV, N_KV_HEADS, t)).reshape(-1)
    tok = jnp.broadcast_to(tok, (Q_PER_KV, N_KV_HEADS, t)).reshape(-1)
    kj = jnp.arange(KEY_PAD)[None, :]
    dist = tok[:, None] + WINDOW - kj
    valid = (dist >= 0) & (dist < WINDOW) & (kj < WINDOW + t)
    slopes = _alibi_slopes()[head][:, None]
    bias = jnp.where(valid, -slopes * dist.astype(F32), NEG_INF).astype(F32)
    sink_col = sinks[head][:, None].astype(F32)
    row_kh = jnp.repeat(jnp.arange(N_KV_HEADS), t)[:, None]
    lane_kh = (jnp.arange(KV_DIM) // HEAD_DIM)[None, :]
    mask = (row_kh == lane_kh).astype(F32)
    return bias, sink_col, mask


def _gkd_cols(w):
    r = w.shape[0]
    w4 = w.reshape(r, N_KV_HEADS, Q_PER_KV, HEAD_DIM)
    return w4.transpose(0, 2, 1, 3).reshape(r, N_HEADS * HEAD_DIM)


def _spatial_tables(w_s, b_s, t):
    tril = jnp.tril(jnp.ones((CHUNK, CHUNK), dtype=bool))
    ws_p = jnp.where(tril[None], w_s, 0.0)
    bs_p = jnp.repeat(b_s.T, GROUP_DIM_A, axis=1)
    r = jnp.arange(CHUNK)
    same = (r[:, None] // t) == (r[None, :] // t)
    small = jnp.where(tril[None, :t, :t], w_s[:, :t, :t], 0.0)
    ws_s = jnp.where(same[None], jnp.tile(small, (1, CHUNK // t, CHUNK // t)), 0.0)
    bs_s = jnp.repeat(jnp.tile(b_s[:, :t], (1, CHUNK // t)).T, GROUP_DIM_A, axis=1)
    return jnp.stack([ws_p, ws_s]).astype(BF16), jnp.stack([bs_p, bs_s]).astype(F32)


def kernel(x_prompt, x_sample, cache_k, cache_v, p_prompt, p_sample, ln1_g, ln1_b, ln2_g, ln2_b,
           a_w_in, a_b_in, a_ln_g, a_ln_b, a_w_s, a_b_s, a_w_out, kv_ln_g, kv_ln_b, w_kv,
           b_w_q, b_sinks, b_w_o, router_w, router_b, exp_w_up, exp_b_up, exp_w_dn, exp_b_dn,
           ple_w_proj, ple_w_gate):
    bsz, seq, _ = x_prompt.shape
    n_seq, t, _ = x_sample.shape
    n_p = bsz * seq
    n_s = n_seq * t
    n = n_p + n_s
    depth = ln1_g.shape[0]
    n_a = a_w_in.shape[0]

    xp2 = x_prompt.reshape(n_p, D_MODEL)
    xs2 = x_sample.reshape(n_s, D_MODEL)
    pp2 = p_prompt.reshape(depth * n_p, PLE_DIM)
    ps2 = p_sample.reshape(depth * n_s, PLE_DIM)
    row = lambda v: v.reshape(1, -1).astype(F32)

    xs_rows = ROW_TILE * (n * TOP_K + N_EXPERTS * TM_EXPERT)
    xs_buf = None
    tri = (jnp.arange(TM_ROUTER)[:, None] <= jnp.arange(TM_ROUTER)[None, :]).astype(BF16)
    b_up4 = exp_b_up.reshape(depth, N_EXPERTS, 1, 2 * D_EXPERT)
    b_dn4 = exp_b_dn.reshape(depth, N_EXPERTS, 1, D_MODEL)
    prompt_bias = _prompt_bias()

    def gmlp_weights(i):
        ws2, bs2 = _spatial_tables(a_w_s[i], a_b_s[i], t)
        return (a_w_in[i].astype(BF16), row(a_b_in[i]), row(a_ln_g[i]), row(a_ln_b[i]), ws2, bs2,
                a_w_out[i].astype(BF16), row(ln1_g[i]), row(ln1_b[i]))

    chunk_v = []
    kv_params = (row(kv_ln_g), row(kv_ln_b), w_kv.astype(BF16))
    x = x1_next = x_out_p = x_out_s = None
    kp = vp = k_buf = v_buf = None
    kpad = vpad = kbuf_pad = vbuf_pad = None
    for i in range(depth):
        fuse_next = i + 1 < n_a
        last = i == depth - 1
        if i < n_a:
            if i == 0:
                x1, v_rows, xs_buf = _gmlp_layer(xp2, xs2, 0, n_p, n_s, gmlp_weights(0), xs_rows)
                chunk_v.append(v_rows.reshape(n_seq, t, D_MODEL))
            else:
                x1 = x1_next
            src = (x1, x1, n_p)
        else:
            j = i - n_a
            wq = b_w_q[j].astype(BF16)
            wo = b_w_o[j].astype(BF16)
            x1_p = _swa_prompt_layer(x, bsz, seq, kpad, vpad, prompt_bias,
                                     b_sinks[j].astype(F32), wq, wo, row(ln1_g[i]), row(ln1_b[i]))
            s_bias, s_sink, s_mask = _sample_tables(t, b_sinks[j])
            x1_s = _swa_sample_layer(x, n_p, n_seq, t, kbuf_pad, vbuf_pad, s_bias, s_sink, s_mask,
                                     _gkd_cols(wq), _gkd_cols(wo.T).T, row(ln1_g[i]), row(ln1_b[i]))
            src = (x1_p, x1_s, 0)
        out, extra, xs_buf = _moe_layer(
            src, n, xs_buf, pp2, ps2, n_p, i, router_w[i].T, router_b[i].reshape(N_EXPERTS, 1), tri,
            exp_w_up, b_up4, exp_w_dn, b_dn4, ple_w_gate[i].astype(BF16),
            ple_w_proj[i].astype(BF16), row(ln2_g[i]), row(ln2_b[i]),
            kv_params if i == n_a - 1 else None,
            gmlp_weights(i + 1) if fuse_next else None,
            split_out=last and i != n_a - 1 and not fuse_next)
        if fuse_next:
            x1_next = out
            chunk_v.append(extra.reshape(n_seq, t, D_MODEL))
        elif last and i != n_a - 1:
            x_out_p, x_out_s = out, extra
        else:
            x, kv = out, extra
        if last and i == n_a - 1:
            x_out_p, x_out_s = x[:n_p], x[n_p:]
        if i == n_a - 1:
            k_all, v_all = kv[:, :KV_DIM], kv[:, KV_DIM:]
            kp = k_all[:n_p].reshape(bsz, seq, KV_DIM)
            vp = v_all[:n_p].reshape(bsz, seq, KV_DIM)
            k_buf = jnp.concatenate([cache_k.reshape(n_seq, WINDOW, KV_DIM),
                                     k_all[n_p:].reshape(n_seq, t, KV_DIM)], axis=1)
            v_buf = jnp.concatenate([cache_v.reshape(n_seq, WINDOW, KV_DIM),
                                     v_all[n_p:].reshape(n_seq, t, KV_DIM)], axis=1)
            front = ((0, 0), (WINDOW, 0), (0, 0))
            kpad = jnp.pad(kp, front).astype(BF16)
            vpad = jnp.pad(vp, front).astype(BF16)
            tail = ((0, 0), (0, KEY_PAD - WINDOW - t), (0, 0))
            kbuf_pad = jnp.pad(k_buf, tail).astype(BF16)
            vbuf_pad = jnp.pad(v_buf, tail).astype(BF16)

    heads = (N_KV_HEADS, HEAD_DIM)
    return (x_out_p.reshape(bsz, seq, D_MODEL),
            x_out_s.reshape(n_seq, t, D_MODEL),
            jnp.stack(chunk_v),
            kp[:, -WINDOW:].reshape(bsz, WINDOW, *heads),
            vp[:, -WINDOW:].reshape(bsz, WINDOW, *heads),
            k_buf[:, -WINDOW:].reshape(n_seq, WINDOW, *heads),
            v_buf[:, -WINDOW:].reshape(n_seq, WINDOW, *heads))
```

```python
import functools

import jax
import jax.numpy as jnp
from jax import lax
from jax.experimental import pallas as pl
from jax.experimental.pallas import tpu as pltpu

F32 = jnp.float32
BF16 = jnp.bfloat16

D_MODEL = 1024
DEPTH = 4
N_A_LAYERS = 2
CHUNK = 128
N_GROUPS_A = 8
GROUP_DIM_A = D_MODEL // N_GROUPS_A
HEAD_DIM = 64
N_HEADS = 16
N_KV_HEADS = 4
Q_PER_KV = 4
KV_DIM = N_KV_HEADS * HEAD_DIM
WINDOW = 128
N_EXPERTS = 32
TOP_K = 4
D_EXPERT = 1024
SWIGLU_LIMIT = 7.0
SWIGLU_ALPHA = 1.702
PLE_DIM = 256
DEEPNORM_ALPHA = (2 * DEPTH) ** 0.25
LN_EPS = 1e-5
NEG_INF = -1e30

TM_TOKEN = 256
TM_MIXER = 512
TM_DISPATCH = 512
DISPATCH_CHUNK = 128
TM_ROUTER = 512
TM_EXPERT = 512
SAMPLE_SEQ_BLOCK = 16
KEY_PAD = 256
VMEM_LIMIT = 56 * 1024 * 1024


def _ln(x, g, b):
    mu = jnp.mean(x, axis=-1, keepdims=True)
    xc = x - mu
    var = jnp.mean(xc * xc, axis=-1, keepdims=True)
    return xc * lax.rsqrt(var + LN_EPS) * g + b


def _dot(a, b):
    return jnp.dot(a, b, preferred_element_type=F32)


def _dot_nt(a, b):
    return lax.dot_general(a, b, (((1,), (1,)), ((), ())), preferred_element_type=F32)


def _full(shape):
    n = len(shape)
    return pl.BlockSpec(shape, lambda *_: (0,) * n)


def _gmlp_math(x, win_ref, bin_ref, lng_ref, lnb_ref, ws_ref, bs_ref, wout_ref, g1_ref, b1_ref):
    h = _dot(x.astype(BF16), win_ref[...]) + bin_ref[...]
    h = 0.5 * h * (1.0 + lax.erf(h * (2.0 ** -0.5)))
    u = h[:, :D_MODEL]
    v = _ln(h[:, D_MODEL:], lng_ref[...], lnb_ref[...])
    vb = v.astype(BF16)
    rows = []
    for c in range(x.shape[0] // CHUNK):
        cols = []
        for g in range(N_GROUPS_A):
            blk = vb[c * CHUNK:(c + 1) * CHUNK, g * GROUP_DIM_A:(g + 1) * GROUP_DIM_A]
            cols.append(_dot(ws_ref[g], blk))
        rows.append(jnp.concatenate(cols, axis=1) + bs_ref[...])
    s = jnp.concatenate(rows, axis=0)
    gated = (u * s).astype(BF16)
    m = _dot(gated, wout_ref[...])
    return _ln(DEEPNORM_ALPHA * x + m, g1_ref[...], b1_ref[...]), v


ZERO_ROWS = 4096


def _gmlp_kernel(xa_ref, xb_ref, *refs, p_tiles, zero_chunks, chunks_per_step):
    *w_refs, x1_ref, v_ref, zs_hbm, zbuf, sem = refs
    i = pl.program_id(0)

    @pl.when(i == 0)
    def _():
        zbuf[...] = jnp.zeros_like(zbuf)

    def zero_copy(c):
        start = pl.multiple_of(c * ZERO_ROWS, ZERO_ROWS)
        return pltpu.make_async_copy(zbuf, zs_hbm.at[pl.ds(start, ZERO_ROWS), :], sem)

    def for_my_chunks(fn):
        for j in range(chunks_per_step):
            c = i * chunks_per_step + j

            @pl.when(c < zero_chunks)
            def _():
                fn(zero_copy(c))

    for_my_chunks(lambda cp: cp.start())
    x = jnp.where(i >= p_tiles, xb_ref[...], xa_ref[...])
    x1_ref[...], v_ref[...] = _gmlp_math(x, *w_refs)
    for_my_chunks(lambda cp: cp.wait())


def _gmlp_weight_specs(kind):
    return [
        _full((D_MODEL, 2 * D_MODEL)),
        _full((1, 2 * D_MODEL)),
        _full((1, D_MODEL)),
        _full((1, D_MODEL)),
        pl.BlockSpec((None, N_GROUPS_A, CHUNK, CHUNK), lambda i: (kind(i), 0, 0, 0)),
        pl.BlockSpec((None, CHUNK, D_MODEL), lambda i: (kind(i), 0, 0)),
        _full((D_MODEL, D_MODEL)),
        _full((1, D_MODEL)),
        _full((1, D_MODEL)),
    ]


def _gmlp_layer(xa, xb, xb_row0, n_prompt, n_sample, weights, zero_rows, tm=TM_MIXER):
    n = n_prompt + n_sample
    p_tiles = n_prompt // tm
    n_tiles = n // tm
    s_tiles = n_tiles - p_tiles
    b_tile0 = xb_row0 // tm
    kind = lambda i: jnp.where(i >= p_tiles, 1, 0)
    assert zero_rows % ZERO_ROWS == 0
    zero_chunks = zero_rows // ZERO_ROWS
    x1, v, zs = pl.pallas_call(
        functools.partial(_gmlp_kernel, p_tiles=p_tiles, zero_chunks=zero_chunks,
                          chunks_per_step=-(-zero_chunks // n_tiles)),
        grid=(n_tiles,),
        in_specs=[
            pl.BlockSpec((tm, D_MODEL), lambda i: (jnp.minimum(i, p_tiles - 1), 0)),
            pl.BlockSpec((tm, D_MODEL), lambda i: (jnp.maximum(i - p_tiles, 0) + b_tile0, 0)),
        ] + _gmlp_weight_specs(kind),
        out_specs=[
            pl.BlockSpec((tm, D_MODEL), lambda i: (i, 0)),
            pl.BlockSpec((tm, D_MODEL), lambda i: (jnp.maximum(i - (p_tiles - 1), 0), 0)),
            pl.BlockSpec(memory_space=pl.ANY),
        ],
        out_shape=[
            jax.ShapeDtypeStruct((n, D_MODEL), F32),
            jax.ShapeDtypeStruct(((s_tiles + 1) * tm, D_MODEL), F32),
            jax.ShapeDtypeStruct((zero_rows, LANES), F32),
        ],
        scratch_shapes=[pltpu.VMEM((ZERO_ROWS, LANES), F32), pltpu.SemaphoreType.DMA],
        compiler_params=pltpu.CompilerParams(
            dimension_semantics=("arbitrary",), vmem_limit_bytes=VMEM_LIMIT),
        name="gmlp_mixer",
    )(xa, xb, *weights)
    return x1, v[tm:], zs


def _swa_prompt_kernel(sink_ref, x_ref, k_ref, v_ref, bias_ref, wq_ref, wo_ref, g1_ref, b1_ref,
                       x1_ref, *, tq):
    i = pl.program_id(1)
    x = x_ref[...]
    qb = (_dot(x.astype(BF16), wq_ref[...]) * (HEAD_DIM ** -0.5)).astype(BF16)
    blocks = []
    for j in range(tq // WINDOW):
        blk = i * (tq // WINDOW) + j
        kstart = pl.multiple_of(blk * WINDOW, WINDOW)
        kb = k_ref[pl.ds(kstart, 2 * WINDOW), :]
        vb = v_ref[pl.ds(kstart, 2 * WINDOW), :]
        sel = jnp.where(blk == 0, 0, 1)
        outs = []
        for h in range(N_HEADS):
            kh = h // Q_PER_KV
            qh = qb[j * WINDOW:(j + 1) * WINDOW, h * HEAD_DIM:(h + 1) * HEAD_DIM]
            s = _dot_nt(qh, kb[:, kh * HEAD_DIM:(kh + 1) * HEAD_DIM])
            l = s + bias_ref[sel, h]
            sink = sink_ref[h]
            m = jnp.maximum(jnp.max(l, axis=1, keepdims=True), sink)
            p = jnp.exp(l - m)
            den = jnp.sum(p, axis=1, keepdims=True) + jnp.exp(sink - m)
            o = _dot(p.astype(BF16), vb[:, kh * HEAD_DIM:(kh + 1) * HEAD_DIM])
            outs.append(o / den)
        blocks.append(jnp.concatenate(outs, axis=1))
    attn = jnp.concatenate(blocks, axis=0)
    m_out = _dot(attn.astype(BF16), wo_ref[...])
    x1_ref[...] = _ln(DEEPNORM_ALPHA * x + m_out, g1_ref[...], b1_ref[...])


def _swa_prompt_layer(x, bsz, seq, kpad, vpad, bias, sinks, w_q, w_o, g1, b1, tq=TM_MIXER):
    nq = seq // tq
    return pl.pallas_call(
        functools.partial(_swa_prompt_kernel, tq=tq),
        grid=(bsz, nq),
        in_specs=[
            pl.BlockSpec(memory_space=pltpu.SMEM),
            pl.BlockSpec((tq, D_MODEL), lambda b, i: (b * nq + i, 0)),
            pl.BlockSpec((None, seq + WINDOW, KV_DIM), lambda b, i: (b, 0, 0)),
            pl.BlockSpec((None, seq + WINDOW, KV_DIM), lambda b, i: (b, 0, 0)),
            _full((2, N_HEADS, WINDOW, 2 * WINDOW)),
            _full((D_MODEL, D_MODEL)),
            _full((D_MODEL, D_MODEL)),
            _full((1, D_MODEL)),
            _full((1, D_MODEL)),
        ],
        out_specs=pl.BlockSpec((tq, D_MODEL), lambda b, i: (b * nq + i, 0)),
        out_shape=jax.ShapeDtypeStruct((bsz * seq, D_MODEL), F32),
        compiler_params=pltpu.CompilerParams(
            dimension_semantics=("arbitrary", "arbitrary"), vmem_limit_bytes=VMEM_LIMIT),
        name="swa_prompt",
    )(sinks, x, kpad, vpad, bias, w_q, w_o, g1, b1)


def _swa_sample_kernel(x_ref, k_ref, v_ref, bias_ref, sink_ref, mask_ref, wq_ref, wo_ref,
                       g1_ref, b1_ref, x1_ref, *, sb, t):
    x = x_ref[...]
    q = _dot(x.astype(BF16), wq_ref[...]) * (HEAD_DIM ** -0.5)
    mask = mask_ref[...]
    bias = bias_ref[...]
    sink = sink_ref[...]
    outs = []
    for s in range(sb):
        qs = q[s * t:(s + 1) * t, :]
        parts = []
        for g in range(Q_PER_KV):
            qg = qs[:, g * KV_DIM:(g + 1) * KV_DIM]
            parts.append(jnp.concatenate([qg] * N_KV_HEADS, axis=0) * mask)
        qexp = jnp.concatenate(parts, axis=0).astype(BF16)
        l = _dot_nt(qexp, k_ref[s]) + bias
        m = jnp.maximum(jnp.max(l, axis=1, keepdims=True), sink)
        p = jnp.exp(l - m)
        den = jnp.sum(p, axis=1, keepdims=True) + jnp.exp(sink - m)
        r = _dot(p.astype(BF16), v_ref[s]) / den
        og = []
        for g in range(Q_PER_KV):
            rg = r[g * N_KV_HEADS * t:(g + 1) * N_KV_HEADS * t, :] * mask
            acc = rg[0:t]
            for kh in range(1, N_KV_HEADS):
                acc = acc + rg[kh * t:(kh + 1) * t]
            og.append(acc)
        outs.append(jnp.concatenate(og, axis=1))
    attn = jnp.concatenate(outs, axis=0)
    m_out = _dot(attn.astype(BF16), wo_ref[...])
    x1_ref[...] = _ln(DEEPNORM_ALPHA * x + m_out, g1_ref[...], b1_ref[...])


def _swa_sample_layer(x, row0, n_seq, t, kbuf, vbuf, bias, sink_col, mask, w_q, w_o, g1, b1,
                      sb=SAMPLE_SEQ_BLOCK):
    rows = sb * t
    hr = N_HEADS * t
    blk0 = row0 // rows
    return pl.pallas_call(
        functools.partial(_swa_sample_kernel, sb=sb, t=t),
        grid=(n_seq // sb,),
        in_specs=[
            pl.BlockSpec((rows, D_MODEL), lambda i: (blk0 + i, 0)),
            pl.BlockSpec((sb, KEY_PAD, KV_DIM), lambda i: (i, 0, 0)),
            pl.BlockSpec((sb, KEY_PAD, KV_DIM), lambda i: (i, 0, 0)),
            _full((hr, KEY_PAD)),
            _full((hr, 1)),
            _full((N_KV_HEADS * t, KV_DIM)),
            _full((D_MODEL, D_MODEL)),
            _full((D_MODEL, D_MODEL)),
            _full((1, D_MODEL)),
            _full((1, D_MODEL)),
        ],
        out_specs=pl.BlockSpec((rows, D_MODEL), lambda i: (i, 0)),
        out_shape=jax.ShapeDtypeStruct((n_seq * t, D_MODEL), F32),
        compiler_params=pltpu.CompilerParams(
            dimension_semantics=("arbitrary",), vmem_limit_bytes=VMEM_LIMIT),
        name="swa_sample",
    )(x, kbuf, vbuf, bias, sink_col, mask, w_q, w_o, g1, b1)


def _token_specs(tm, n_prompt, b_row0):
    p_steps = n_prompt // tm
    b0 = b_row0 // tm
    return [pl.BlockSpec((tm, D_MODEL), lambda i: (jnp.minimum(i, p_steps - 1), 0)),
            pl.BlockSpec((tm, D_MODEL), lambda i: (jnp.maximum(i - p_steps, 0) + b0, 0))]


def _router_kernel(xa_ref, xb_ref, wt_ref, b_ref, tri_ref, idx_ref, gate_ref, rank_ref, cnt_ref,
                   carry, *, tm, p_steps):
    i = pl.program_id(0)

    @pl.when(i == 0)
    def _():
        carry[...] = jnp.zeros_like(carry)

    x = jnp.where(i >= p_steps, xb_ref[...], xa_ref[...])
    x_hi = x.astype(BF16)
    x_lo = (x - x_hi.astype(F32)).astype(BF16)
    w = wt_ref[...]
    w_hi = w.astype(BF16)
    w_lo = (w - w_hi.astype(F32)).astype(BF16)
    logits = (_dot_nt(w_hi, x_hi) + (_dot_nt(w_hi, x_lo) + _dot_nt(w_lo, x_hi))) + b_ref[...]
    eidx = lax.broadcasted_iota(jnp.int32, (N_EXPERTS, tm), 0)
    l = logits
    vals, idxs, sels = [], [], []
    for _ in range(TOP_K):
        m = jnp.max(l, axis=0, keepdims=True)
        idx = jnp.min(jnp.where(l == m, eidx, N_EXPERTS), axis=0, keepdims=True)
        sel = eidx == idx
        vals.append(m)
        idxs.append(idx)
        sels.append(sel)
        l = jnp.where(sel, -jnp.inf, l)
    exps = [jnp.exp(v - vals[0]) for v in vals]
    den = exps[0] + exps[1] + exps[2] + exps[3]
    onehot = jnp.zeros((N_EXPERTS, tm), F32)
    for sel in sels:
        onehot = onehot + sel.astype(F32)
    incl = _dot(onehot.astype(BF16), tri_ref[...])
    excl = incl - onehot + carry[...]
    for k in range(TOP_K):
        idx_ref[k:k + 1, :] = idxs[k]
        gate_ref[k:k + 1, :] = exps[k] / den
        rank = jnp.sum(jnp.where(sels[k], excl, 0.0), axis=0, keepdims=True)
        rank_ref[k:k + 1, :] = rank.astype(jnp.int32)
    carry[...] = carry[...] + jnp.sum(onehot, axis=1, keepdims=True)
    cnt_ref[...] = jnp.broadcast_to(carry[...], cnt_ref.shape)


def _route(src, n, n_prompt, router_wt, router_b, tri, tm=TM_ROUTER):
    xa, xb, b_row0 = src
    spec_kn = pl.BlockSpec((TOP_K, tm), lambda i: (0, i))
    idx, gate, rank, cnt = pl.pallas_call(
        functools.partial(_router_kernel, tm=tm, p_steps=n_prompt // tm),
        grid=(n // tm,),
        in_specs=_token_specs(tm, n_prompt, b_row0) + [
            _full((N_EXPERTS, D_MODEL)),
            _full((N_EXPERTS, 1)),
            _full((tm, tm)),
        ],
        out_specs=[spec_kn, spec_kn, spec_kn, _full((N_EXPERTS, 128))],
        out_shape=[
            jax.ShapeDtypeStruct((TOP_K, n), jnp.int32),
            jax.ShapeDtypeStruct((TOP_K, n), F32),
            jax.ShapeDtypeStruct((TOP_K, n), jnp.int32),
            jax.ShapeDtypeStruct((N_EXPERTS, 128), F32),
        ],
        scratch_shapes=[pltpu.VMEM((N_EXPERTS, 1), F32)],
        compiler_params=pltpu.CompilerParams(
            dimension_semantics=("arbitrary",), vmem_limit_bytes=VMEM_LIMIT),
        name="moe_router",
    )(xa, xb, router_wt, router_b, tri)
    return idx, gate, rank, cnt[:, 0].astype(jnp.int32)


ROW_TILE = 8
LANES = D_MODEL // ROW_TILE


def _to_row_tiles(ref, x):
    rows = x.shape[0]
    for c in range(ROW_TILE):
        ref[pl.ds(c, rows, stride=ROW_TILE), :] = x[:, c * LANES:(c + 1) * LANES]


def _from_row_tiles(ref, rows):
    return jnp.concatenate([ref[pl.ds(c, rows, stride=ROW_TILE), :] for c in range(ROW_TILE)], axis=1)


def _row_tile(ref, p):
    start = p * ROW_TILE
    if not isinstance(p, int):
        start = pl.multiple_of(start, ROW_TILE)
    return ref.at[pl.ds(start, ROW_TILE), :]


def _row_copy_out(x_ref, xs_hbm, sem, t, p):
    return pltpu.make_async_copy(_row_tile(x_ref, t), _row_tile(xs_hbm, p), sem)


def _dispatch_kernel(pos_ref, xa_ref, xb_ref, xs_in_hbm, xs_hbm, x_ref, sem, *, tm, chunk, p_steps):
    del xs_in_hbm
    _to_row_tiles(x_ref, jnp.where(pl.program_id(0) >= p_steps, xb_ref[...], xa_ref[...]))

    def issue(c, carry):
        t0 = c * chunk
        for t in range(chunk):
            for k in range(TOP_K):
                _row_copy_out(x_ref, xs_hbm, sem, t0 + t,
                              pos_ref[0, 0, k * tm + t0 + t]).start(priority=k % 2)
        return carry

    lax.fori_loop(0, tm // chunk, issue, 0)
    for _ in range(TOP_K):
        pltpu.make_async_copy(x_ref, xs_hbm.at[pl.ds(0, ROW_TILE * tm), :], sem).wait()


def _dispatch(src, n, n_prompt, pos3, xs_buf, tm=TM_DISPATCH, chunk=DISPATCH_CHUNK):
    xa, xb, b_row0 = src
    return pl.pallas_call(
        functools.partial(_dispatch_kernel, tm=tm, chunk=chunk, p_steps=n_prompt // tm),
        grid=(n // tm,),
        in_specs=[
            pl.BlockSpec((1, 1, TOP_K * tm), lambda i: (i, 0, 0), memory_space=pltpu.SMEM),
        ] + _token_specs(tm, n_prompt, b_row0) + [
            pl.BlockSpec(memory_space=pl.ANY),
        ],
        out_specs=pl.BlockSpec(memory_space=pl.ANY),
        out_shape=jax.ShapeDtypeStruct(xs_buf.shape, xs_buf.dtype),
        scratch_shapes=[pltpu.VMEM((ROW_TILE * tm, LANES), F32), pltpu.SemaphoreType.DMA],
        input_output_aliases={3: 0},
        compiler_params=pltpu.CompilerParams(
            dimension_semantics=("arbitrary",), vmem_limit_bytes=VMEM_LIMIT),
        name="moe_dispatch",
    )(pos3, xa, xb, xs_buf)


def _expert_kernel(te_ref, nv_ref, xs_ref, wup_ref, bup_ref, wdn_ref, bdn_ref, y_ref):
    del te_ref
    i = pl.program_id(0)

    @pl.when(i < nv_ref[0])
    def _():
        x = _from_row_tiles(xs_ref, y_ref.shape[0] // ROW_TILE)
        h = _dot(x, wup_ref[...]) + bup_ref[...]
        glu = jnp.minimum(h[:, :D_EXPERT], SWIGLU_LIMIT)
        lin = jnp.clip(h[:, D_EXPERT:], -SWIGLU_LIMIT, SWIGLU_LIMIT)
        act = glu * jax.nn.sigmoid(SWIGLU_ALPHA * glu) * (lin + 1.0)
        _to_row_tiles(y_ref, _dot(act, wdn_ref[...]) + bdn_ref[...])

    @pl.when(i >= nv_ref[0])
    def _():
        y_ref[...] = jnp.zeros_like(y_ref)


def _experts(xs, tile_expert, n_valid, w_up, b_up, w_dn, b_dn, layer, tm=TM_EXPERT):
    rows = xs.shape[0] // ROW_TILE
    row_map = lambda i, te, nv: (jnp.minimum(i, nv[0] - 1), 0)
    wspec = lambda shape: pl.BlockSpec((None, None) + shape, lambda i, te, nv: (layer, te[i], 0, 0))
    grid_spec = pltpu.PrefetchScalarGridSpec(
        num_scalar_prefetch=2,
        grid=(rows // tm,),
        in_specs=[
            pl.BlockSpec((ROW_TILE * tm, LANES), row_map),
            wspec((D_MODEL, 2 * D_EXPERT)),
            wspec((1, 2 * D_EXPERT)),
            wspec((D_EXPERT, D_MODEL)),
            wspec((1, D_MODEL)),
        ],
        out_specs=pl.BlockSpec((ROW_TILE * tm, LANES), lambda i, te, nv: (i, 0)),
    )
    return pl.pallas_call(
        _expert_kernel,
        grid_spec=grid_spec,
        out_shape=jax.ShapeDtypeStruct(xs.shape, F32),
        compiler_params=pltpu.CompilerParams(
            dimension_semantics=("arbitrary",), vmem_limit_bytes=VMEM_LIMIT),
        name="moe_experts",
    )(tile_expert, n_valid, xs, w_up, b_up, w_dn, b_dn)


def _row_copy_in(ys_hbm, ybuf, sem, k, t, p):
    return pltpu.make_async_copy(_row_tile(ys_hbm, p), _row_tile(ybuf.at[k], t), sem)


def _combine_kernel(pos_ref, posn_ref, x1a_ref, x1b_ref, gate_ref, pp_ref, ps_ref, wg_ref, wp_ref,
                    g2_ref, b2_ref, *rest, tm, prompt_steps, emit_kv, fuse_gmlp, split_out):
    kv_ref = v_ref = x3s_ref = None
    if emit_kv:
        kvg_ref, kvb_ref, wkv_ref, ys_hbm, x3_ref, kv_ref, ybuf_a, ybuf_b, sem = rest
    elif fuse_gmlp:
        *gmlp_refs, ys_hbm, x3_ref, v_ref, ybuf_a, ybuf_b, sem = rest
    elif split_out:
        ys_hbm, x3_ref, x3s_ref, ybuf_a, ybuf_b, sem = rest
    else:
        ys_hbm, x3_ref, ybuf_a, ybuf_b, sem = rest
    s = pl.program_id(0)
    is_sample = s >= prompt_steps

    def issue(pref, half, ybuf, sm):
        for t in range(tm):
            for k in range(TOP_K):
                _row_copy_in(ys_hbm, ybuf, sm, k, t,
                             pref[0, 0, (half * TOP_K + k) * tm + t]).start(priority=k % 2)

    def wait_all(ybuf, sm):
        for k in range(TOP_K):
            pltpu.make_async_copy(ys_hbm.at[pl.ds(0, ROW_TILE * tm), :], ybuf.at[k], sm).wait()

    def half_math(half, ybuf):
        rows = slice(half * tm, (half + 1) * tm)
        gate = gate_ref[rows, :]
        f = gate[:, 0:1] * _from_row_tiles(ybuf.at[0], tm)
        for k in range(1, TOP_K):
            f = f + gate[:, k:k + 1] * _from_row_tiles(ybuf.at[k], tm)
        x1 = jnp.where(is_sample, x1b_ref[rows, :], x1a_ref[rows, :])
        x2 = _ln(DEEPNORM_ALPHA * x1 + f, g2_ref[...], b2_ref[...])
        p = jnp.where(is_sample, ps_ref[rows, :], pp_ref[rows, :])
        pg = jax.nn.sigmoid(_dot(x2.astype(BF16), wg_ref[...]))
        pp = _dot(p.astype(BF16), wp_ref[...])
        x3 = x2 + pg * pp
        if fuse_gmlp:
            x3_ref[rows, :], v_ref[rows, :] = _gmlp_math(x3, *gmlp_refs)
        elif split_out:
            @pl.when(is_sample)
            def _():
                x3s_ref[rows, :] = x3

            @pl.when(jnp.logical_not(is_sample))
            def _():
                x3_ref[rows, :] = x3
        else:
            x3_ref[rows, :] = x3
        if emit_kv:
            kv_ref[rows, :] = _dot(_ln(x3, kvg_ref[...], kvb_ref[...]).astype(BF16), wkv_ref[...])

    @pl.when(s == 0)
    def _():
        def first(t, c):
            for k in range(TOP_K):
                _row_copy_in(ys_hbm, ybuf_a, sem.at[0], k, t, pos_ref[0, 0, k * tm + t]).start()
            return c

        lax.fori_loop(0, tm, first, 0)

    wait_all(ybuf_a, sem.at[0])
    issue(pos_ref, 1, ybuf_b, sem.at[1])
    half_math(0, ybuf_a)
    wait_all(ybuf_b, sem.at[1])
    issue(posn_ref, 0, ybuf_a, sem.at[0])
    half_math(1, ybuf_b)

    @pl.when(s == pl.num_programs(0) - 1)
    def _():
        wait_all(ybuf_a, sem.at[0])


def _combine(src, n, pos3, gate_t, ys, p_prompt, p_sample, layer, n_prompt, w_gate, w_proj, g2, b2,
             kv_params=None, gmlp_weights=None, split_out=False, tm=TM_TOKEN):
    xa, xb, b_row0 = src
    emit_kv = kv_params is not None
    fuse_gmlp = gmlp_weights is not None
    assert emit_kv + fuse_gmlp + split_out <= 1
    tm2 = 2 * tm
    steps = n // tm2
    prompt_steps = n_prompt // tm2
    sample_steps = steps - prompt_steps
    row = lambda w: pl.BlockSpec((tm2, w), lambda i: (i, 0))
    smem_pos = lambda f: pl.BlockSpec((1, 1, 2 * TOP_K * tm), lambda i: (f(i), 0, 0),
                                      memory_space=pltpu.SMEM)
    in_specs = [
        smem_pos(lambda i: i),
        smem_pos(lambda i: jnp.minimum(i + 1, steps - 1)),
    ] + _token_specs(tm2, n_prompt, b_row0) + [
        row(TOP_K),
        pl.BlockSpec((tm2, PLE_DIM),
                     lambda i: (layer * prompt_steps + jnp.minimum(i, prompt_steps - 1), 0)),
        pl.BlockSpec((tm2, PLE_DIM),
                     lambda i: (layer * sample_steps + jnp.maximum(i - prompt_steps, 0), 0)),
        _full((D_MODEL, D_MODEL)),
        _full((PLE_DIM, D_MODEL)),
        _full((1, D_MODEL)),
        _full((1, D_MODEL)),
    ]
    args = [pos3, pos3, xa, xb, gate_t, p_prompt, p_sample, w_gate, w_proj, g2, b2]
    if split_out:
        out_specs = [
            pl.BlockSpec((tm2, D_MODEL), lambda i: (jnp.minimum(i, prompt_steps - 1), 0)),
            pl.BlockSpec((tm2, D_MODEL), lambda i: (jnp.maximum(i - prompt_steps, 0), 0)),
        ]
        out_shape = [jax.ShapeDtypeStruct((n_prompt, D_MODEL), F32),
                     jax.ShapeDtypeStruct((n - n_prompt, D_MODEL), F32)]
    else:
        out_specs = [row(D_MODEL)]
        out_shape = [jax.ShapeDtypeStruct((n, D_MODEL), F32)]
    if emit_kv:
        in_specs += [_full((1, D_MODEL)), _full((1, D_MODEL)), _full((D_MODEL, 2 * KV_DIM))]
        args += list(kv_params)
        out_specs.append(row(2 * KV_DIM))
        out_shape.append(jax.ShapeDtypeStruct((n, 2 * KV_DIM), F32))
    if fuse_gmlp:
        in_specs += _gmlp_weight_specs(lambda i: jnp.where(i >= prompt_steps, 1, 0))
        args += list(gmlp_weights)
        out_specs.append(pl.BlockSpec((tm2, D_MODEL),
                                      lambda i: (jnp.maximum(i - (prompt_steps - 1), 0), 0)))
        out_shape.append(jax.ShapeDtypeStruct(((sample_steps + 1) * tm2, D_MODEL), F32))
    in_specs.append(pl.BlockSpec(memory_space=pl.ANY))
    args.append(ys)
    outs = pl.pallas_call(
        functools.partial(_combine_kernel, tm=tm, prompt_steps=prompt_steps, emit_kv=emit_kv,
                          fuse_gmlp=fuse_gmlp, split_out=split_out),
        grid=(steps,),
        in_specs=in_specs,
        out_specs=out_specs,
        out_shape=out_shape,
        scratch_shapes=[pltpu.VMEM((TOP_K, ROW_TILE * tm, LANES), F32),
                        pltpu.VMEM((TOP_K, ROW_TILE * tm, LANES), F32),
                        pltpu.SemaphoreType.DMA((2,))],
        compiler_params=pltpu.CompilerParams(
            dimension_semantics=("arbitrary",), vmem_limit_bytes=VMEM_LIMIT),
        name="moe_combine",
    )(*args)
    if fuse_gmlp:
        return outs[0], outs[1][tm2:]
    return outs if (emit_kv or split_out) else (outs[0], None)


def _tile_pos(pos, tm):
    n = pos.shape[1]
    return pos.reshape(TOP_K, n // tm, tm).transpose(1, 0, 2).reshape(n // tm, 1, TOP_K * tm)


def _pair_pos(pos, tm):
    n = pos.shape[1]
    p = pos.reshape(TOP_K, n // (2 * tm), 2, tm).transpose(1, 2, 0, 3)
    return p.reshape(n // (2 * tm), 1, 2 * TOP_K * tm)


def _moe_layer(src, n, xs_buf, p_prompt, p_sample, n_prompt, layer, router_wt, router_b, tri,
               w_up, b_up, w_dn, b_dn, w_gate, w_proj, g2, b2, kv_params=None, gmlp_weights=None,
               split_out=False):
    idx, gate, rank, counts = _route(src, n, n_prompt, router_wt, router_b, tri)
    padded = ((counts + TM_EXPERT - 1) // TM_EXPERT) * TM_EXPERT
    ends = jnp.cumsum(padded)
    starts = ends - padded
    onehot = idx[:, :, None] == jnp.arange(N_EXPERTS, dtype=jnp.int32)
    pos = jnp.sum(jnp.where(onehot, starts, 0), axis=-1) + rank
    n_tiles = xs_buf.shape[0] // (ROW_TILE * TM_EXPERT)
    n_valid = (ends[-1] // TM_EXPERT).astype(jnp.int32)
    tile_start = jnp.minimum(jnp.arange(n_tiles, dtype=jnp.int32), n_valid - 1) * TM_EXPERT
    tile_expert = jnp.sum(tile_start[:, None] >= ends[None, :], axis=1).astype(jnp.int32)
    xs_buf = _dispatch(src, n, n_prompt, _tile_pos(pos, TM_DISPATCH), xs_buf)
    ys = _experts(xs_buf, tile_expert, n_valid.reshape(1), w_up, b_up, w_dn, b_dn, layer)
    x3, extra = _combine(src, n, _pair_pos(pos, TM_TOKEN), gate.T, ys, p_prompt, p_sample, layer,
                         n_prompt, w_gate, w_proj, g2, b2, kv_params, gmlp_weights, split_out)
    return x3, extra, xs_buf


def _alibi_slopes():
    h = jnp.arange(1, N_HEADS + 1, dtype=F32)
    return 2.0 ** (-8.0 * h / N_HEADS)


def _prompt_bias():
    qi = jnp.arange(WINDOW)[:, None]
    kj = jnp.arange(2 * WINDOW)[None, :]
    dist = qi + WINDOW - kj
    valid = (dist >= 0) & (dist < WINDOW)
    slopes = _alibi_slopes()[:, None, None]
    tables = []
    for first in (True, False):
        ok = valid & (kj >= WINDOW) if first else valid
        tables.append(jnp.where(ok[None], -slopes * dist.astype(F32)[None], NEG_INF))
    return jnp.stack(tables).astype(F32)


def _sample_tables(t, sinks):
    g = jnp.arange(Q_PER_KV)[:, None, None]
    kh = jnp.arange(N_KV_HEADS)[None, :, None]
    tok = jnp.arange(t)[None, None, :]
    head = jnp.broadcast_to(kh * Q_PER_KV + g, (Q_PER_KV, N_KV_HEADS, t)).reshape(-1)
    tok = jnp.broadcast_to(tok, (Q_PER_KV, N_KV_HEADS, t)).reshape(-1)
    kj = jnp.arange(KEY_PAD)[None, :]
    dist = tok[:, None] + WINDOW - kj
    valid = (dist >= 0) & (dist < WINDOW) & (kj < WINDOW + t)
    slopes = _alibi_slopes()[head][:, None]
    bias = jnp.where(valid, -slopes * dist.astype(F32), NEG_INF).astype(F32)
    sink_col = sinks[head][:, None].astype(F32)
    row_kh = jnp.repeat(jnp.arange(N_KV_HEADS), t)[:, None]
    lane_kh = (jnp.arange(KV_DIM) // HEAD_DIM)[None, :]
    mask = (row_kh == lane_kh).astype(F32)
    return bias, sink_col, mask


def _gkd_cols(w):
    r = w.shape[0]
    w4 = w.reshape(r, N_KV_HEADS, Q_PER_KV, HEAD_DIM)
    return w4.transpose(0, 2, 1, 3).reshape(r, N_HEADS * HEAD_DIM)


def _spatial_tables(w_s, b_s, t):
    tril = jnp.tril(jnp.ones((CHUNK, CHUNK), dtype=bool))
    ws_p = jnp.where(tril[None], w_s, 0.0)
    bs_p = jnp.repeat(b_s.T, GROUP_DIM_A, axis=1)
    r = jnp.arange(CHUNK)
    same = (r[:, None] // t) == (r[None, :] // t)
    small = jnp.where(tril[None, :t, :t], w_s[:, :t, :t], 0.0)
    ws_s = jnp.where(same[None], jnp.tile(small, (1, CHUNK // t, CHUNK // t)), 0.0)
    bs_s = jnp.repeat(jnp.tile(b_s[:, :t], (1, CHUNK // t)).T, GROUP_DIM_A, axis=1)
    return jnp.stack([ws_p, ws_s]).astype(BF16), jnp.stack([bs_p, bs_s]).astype(F32)


def kernel(x_prompt, x_sample, cache_k, cache_v, p_prompt, p_sample, ln1_g, ln1_b, ln2_g, ln2_b,
           a_w_in, a_b_in, a_ln_g, a_ln_b, a_w_s, a_b_s, a_w_out, kv_ln_g, kv_ln_b, w_kv,
           b_w_q, b_sinks, b_w_o, router_w, router_b, exp_w_up, exp_b_up, exp_w_dn, exp_b_dn,
           ple_w_proj, ple_w_gate):
    bsz, seq, _ = x_prompt.shape
    n_seq, t, _ = x_sample.shape
    n_p = bsz * seq
    n_s = n_seq * t
    n = n_p + n_s
    depth = ln1_g.shape[0]
    n_a = a_w_in.shape[0]

    xp2 = x_prompt.reshape(n_p, D_MODEL)
    xs2 = x_sample.reshape(n_s, D_MODEL)
    pp2 = p_prompt.reshape(depth * n_p, PLE_DIM)
    ps2 = p_sample.reshape(depth * n_s, PLE_DIM)
    row = lambda v: v.reshape(1, -1).astype(F32)

    xs_rows = ROW_TILE * (n * TOP_K + N_EXPERTS * TM_EXPERT)
    xs_buf = None
    tri = (jnp.arange(TM_ROUTER)[:, None] <= jnp.arange(TM_ROUTER)[None, :]).astype(BF16)
    b_up4 = exp_b_up.reshape(depth, N_EXPERTS, 1, 2 * D_EXPERT)
    b_dn4 = exp_b_dn.reshape(depth, N_EXPERTS, 1, D_MODEL)
    prompt_bias = _prompt_bias()

    def gmlp_weights(i):
        ws2, bs2 = _spatial_tables(a_w_s[i], a_b_s[i], t)
        return (a_w_in[i].astype(BF16), row(a_b_in[i]), row(a_ln_g[i]), row(a_ln_b[i]), ws2, bs2,
                a_w_out[i].astype(BF16), row(ln1_g[i]), row(ln1_b[i]))

    chunk_v = []
    kv_params = (row(kv_ln_g), row(kv_ln_b), w_kv.astype(BF16))
    x = x1_next = x_out_p = x_out_s = None
    kp = vp = k_buf = v_buf = None
    kpad = vpad = kbuf_pad = vbuf_pad = None
    for i in range(depth):
        fuse_next = i + 1 < n_a
        last = i == depth - 1
        if i < n_a:
            if i == 0:
                x1, v_rows, xs_buf = _gmlp_layer(xp2, xs2, 0, n_p, n_s, gmlp_weights(0), xs_rows)
                chunk_v.append(v_rows.reshape(n_seq, t, D_MODEL))
            else:
                x1 = x1_next
            src = (x1, x1, n_p)
        else:
            j = i - n_a
            wq = b_w_q[j].astype(BF16)
            wo = b_w_o[j].astype(BF16)
            x1_p = _swa_prompt_layer(x, bsz, seq, kpad, vpad, prompt_bias,
                                     b_sinks[j].astype(F32), wq, wo, row(ln1_g[i]), row(ln1_b[i]))
            s_bias, s_sink, s_mask = _sample_tables(t, b_sinks[j])
            x1_s = _swa_sample_layer(x, n_p, n_seq, t, kbuf_pad, vbuf_pad, s_bias, s_sink, s_mask,
                                     _gkd_cols(wq), _gkd_cols(wo.T).T, row(ln1_g[i]), row(ln1_b[i]))
            src = (x1_p, x1_s, 0)
        out, extra, xs_buf = _moe_layer(
            src, n, xs_buf, pp2, ps2, n_p, i, router_w[i].T, router_b[i].reshape(N_EXPERTS, 1), tri,
            exp_w_up, b_up4, exp_w_dn, b_dn4, ple_w_gate[i].astype(BF16),
            ple_w_proj[i].astype(BF16), row(ln2_g[i]), row(ln2_b[i]),
            kv_params if i == n_a - 1 else None,
            gmlp_weights(i + 1) if fuse_next else None,
            split_out=last and i != n_a - 1 and not fuse_next)
        if fuse_next:
            x1_next = out
            chunk_v.append(extra.reshape(n_seq, t, D_MODEL))
        elif last and i != n_a - 1:
            x_out_p, x_out_s = out, extra
        else:
            x, kv = out, extra
        if last and i == n_a - 1:
            x_out_p, x_out_s = x[:n_p], x[n_p:]
        if i == n_a - 1:
            k_all, v_all = kv[:, :KV_DIM], kv[:, KV_DIM:]
            kp = k_all[:n_p].reshape(bsz, seq, KV_DIM)
            vp = v_all[:n_p].reshape(bsz, seq, KV_DIM)
            k_buf = jnp.concatenate([cache_k.reshape(n_seq, WINDOW, KV_DIM),
                                     k_all[n_p:].reshape(n_seq, t, KV_DIM)], axis=1)
            v_buf = jnp.concatenate([cache_v.reshape(n_seq, WINDOW, KV_DIM),
                                     v_all[n_p:].reshape(n_seq, t, KV_DIM)], axis=1)
            front = ((0, 0), (WINDOW, 0), (0, 0))
            kpad = jnp.pad(kp, front).astype(BF16)
            vpad = jnp.pad(vp, front).astype(BF16)
            tail = ((0, 0), (0, KEY_PAD - WINDOW - t), (0, 0))
            kbuf_pad = jnp.pad(k_buf, tail).astype(BF16)
            vbuf_pad = jnp.pad(v_buf, tail).astype(BF16)

    heads = (N_KV_HEADS, HEAD_DIM)
    return (x_out_p.reshape(bsz, seq, D_MODEL),
            x_out_s.reshape(n_seq, t, D_MODEL),
            jnp.stack(chunk_v),
            kp[:, -WINDOW:].reshape(bsz, WINDOW, *heads),
            vp[:, -WINDOW:].reshape(bsz, WINDOW, *heads),
            k_buf[:, -WINDOW:].reshape(n_seq, WINDOW, *heads),
            v_buf[:, -WINDOW:].reshape(n_seq, WINDOW, *heads))
```

```python
import functools

import jax
import jax.numpy as jnp
from jax import lax
from jax.experimental import pallas as pl
from jax.experimental.pallas import tpu as pltpu

F32 = jnp.float32
BF16 = jnp.bfloat16

D_MODEL = 1024
DEPTH = 4
N_A_LAYERS = 2
CHUNK = 128
N_GROUPS_A = 8
GROUP_DIM_A = D_MODEL // N_GROUPS_A
HEAD_DIM = 64
N_HEADS = 16
N_KV_HEADS = 4
Q_PER_KV = 4
KV_DIM = N_KV_HEADS * HEAD_DIM
WINDOW = 128
N_EXPERTS = 32
TOP_K = 4
D_EXPERT = 1024
SWIGLU_LIMIT = 7.0
SWIGLU_ALPHA = 1.702
PLE_DIM = 256
DEEPNORM_ALPHA = (2 * DEPTH) ** 0.25
LN_EPS = 1e-5
NEG_INF = -1e30

TM_TOKEN = 256
TM_MIXER = 512
TM_DISPATCH = 512
DISPATCH_CHUNK = 128
TM_ROUTER = 512
TM_EXPERT = 512
SAMPLE_SEQ_BLOCK = 16
KEY_PAD = 256
VMEM_LIMIT = 56 * 1024 * 1024


def _ln(x, g, b):
    mu = jnp.mean(x, axis=-1, keepdims=True)
    xc = x - mu
    var = jnp.mean(xc * xc, axis=-1, keepdims=True)
    return xc * lax.rsqrt(var + LN_EPS) * g + b


def _dot(a, b):
    return jnp.dot(a, b, preferred_element_type=F32)


def _dot_mixed(a, b):
    return lax.dot_general(a, b, (((1,), (0,)), ((), ())), preferred_element_type=F32)


def _dot_nt(a, b):
    return lax.dot_general(a, b, (((1,), (1,)), ((), ())), preferred_element_type=F32)


def _full(shape):
    n = len(shape)
    return pl.BlockSpec(shape, lambda *_: (0,) * n)


def _gmlp_math(x, win_ref, bin_ref, lng_ref, lnb_ref, ws_ref, bs_ref, wout_ref, g1_ref, b1_ref):
    h = _dot(x.astype(BF16), win_ref[...]) + bin_ref[...]
    h = 0.5 * h * (1.0 + lax.erf(h * (2.0 ** -0.5)))
    u = h[:, :D_MODEL]
    v = _ln(h[:, D_MODEL:], lng_ref[...], lnb_ref[...])
    vb = v.astype(BF16)
    rows = []
    for c in range(x.shape[0] // CHUNK):
        cols = []
        for g in range(N_GROUPS_A):
            blk = vb[c * CHUNK:(c + 1) * CHUNK, g * GROUP_DIM_A:(g + 1) * GROUP_DIM_A]
            cols.append(_dot(ws_ref[g], blk))
        rows.append(jnp.concatenate(cols, axis=1) + bs_ref[...])
    s = jnp.concatenate(rows, axis=0)
    gated = (u * s).astype(BF16)
    m = _dot(gated, wout_ref[...])
    return _ln(DEEPNORM_ALPHA * x + m, g1_ref[...], b1_ref[...]), v


ZERO_ROWS = 4096


def _gmlp_kernel(xa_ref, xb_ref, *refs, p_tiles, zero_chunks, chunks_per_step):
    *w_refs, x1_ref, v_ref, zs_hbm, zbuf, sem = refs
    i = pl.program_id(0)

    @pl.when(i == 0)
    def _():
        zbuf[...] = jnp.zeros_like(zbuf)

    def zero_copy(c):
        start = pl.multiple_of(c * ZERO_ROWS, ZERO_ROWS)
        return pltpu.make_async_copy(zbuf, zs_hbm.at[pl.ds(start, ZERO_ROWS), :], sem)

    def for_my_chunks(fn):
        for j in range(chunks_per_step):
            c = i * chunks_per_step + j

            @pl.when(c < zero_chunks)
            def _():
                fn(zero_copy(c))

    for_my_chunks(lambda cp: cp.start())
    x = jnp.where(i >= p_tiles, xb_ref[...], xa_ref[...])
    x1_ref[...], v_ref[...] = _gmlp_math(x, *w_refs)
    for_my_chunks(lambda cp: cp.wait())


def _gmlp_weight_specs(kind):
    return [
        _full((D_MODEL, 2 * D_MODEL)),
        _full((1, 2 * D_MODEL)),
        _full((1, D_MODEL)),
        _full((1, D_MODEL)),
        pl.BlockSpec((None, N_GROUPS_A, CHUNK, CHUNK), lambda i: (kind(i), 0, 0, 0)),
        pl.BlockSpec((None, CHUNK, D_MODEL), lambda i: (kind(i), 0, 0)),
        _full((D_MODEL, D_MODEL)),
        _full((1, D_MODEL)),
        _full((1, D_MODEL)),
    ]


def _gmlp_layer(xa, xb, xb_row0, n_prompt, n_sample, weights, zero_rows, tm=TM_MIXER):
    n = n_prompt + n_sample
    p_tiles = n_prompt // tm
    n_tiles = n // tm
    s_tiles = n_tiles - p_tiles
    b_tile0 = xb_row0 // tm
    kind = lambda i: jnp.where(i >= p_tiles, 1, 0)
    assert zero_rows % ZERO_ROWS == 0
    zero_chunks = zero_rows // ZERO_ROWS
    x1, v, zs = pl.pallas_call(
        functools.partial(_gmlp_kernel, p_tiles=p_tiles, zero_chunks=zero_chunks,
                          chunks_per_step=-(-zero_chunks // n_tiles)),
        grid=(n_tiles,),
        in_specs=[
            pl.BlockSpec((tm, D_MODEL), lambda i: (jnp.minimum(i, p_tiles - 1), 0)),
            pl.BlockSpec((tm, D_MODEL), lambda i: (jnp.maximum(i - p_tiles, 0) + b_tile0, 0)),
        ] + _gmlp_weight_specs(kind),
        out_specs=[
            pl.BlockSpec((tm, D_MODEL), lambda i: (i, 0)),
            pl.BlockSpec((tm, D_MODEL), lambda i: (jnp.maximum(i - (p_tiles - 1), 0), 0)),
            pl.BlockSpec(memory_space=pl.ANY),
        ],
        out_shape=[
            jax.ShapeDtypeStruct((n, D_MODEL), F32),
            jax.ShapeDtypeStruct(((s_tiles + 1) * tm, D_MODEL), F32),
            jax.ShapeDtypeStruct((zero_rows, LANES), F32),
        ],
        scratch_shapes=[pltpu.VMEM((ZERO_ROWS, LANES), F32), pltpu.SemaphoreType.DMA],
        compiler_params=pltpu.CompilerParams(
            dimension_semantics=("arbitrary",), vmem_limit_bytes=VMEM_LIMIT),
        name="gmlp_mixer",
    )(xa, xb, *weights)
    return x1, v[tm:], zs


def _swa_prompt_kernel(sink_ref, x_ref, k_ref, v_ref, bias_ref, wq_ref, wo_ref, g1_ref, b1_ref,
                       x1_ref, *, tq):
    i = pl.program_id(1)
    x = x_ref[...]
    qb = (_dot(x.astype(BF16), wq_ref[...]) * (HEAD_DIM ** -0.5)).astype(BF16)
    blocks = []
    for j in range(tq // WINDOW):
        blk = i * (tq // WINDOW) + j
        kstart = pl.multiple_of(blk * WINDOW, WINDOW)
        kb = k_ref[pl.ds(kstart, 2 * WINDOW), :]
        vb = v_ref[pl.ds(kstart, 2 * WINDOW), :]
        sel = jnp.where(blk == 0, 0, 1)
        outs = []
        for h in range(N_HEADS):
            kh = h // Q_PER_KV
            qh = qb[j * WINDOW:(j + 1) * WINDOW, h * HEAD_DIM:(h + 1) * HEAD_DIM]
            s = _dot_nt(qh, kb[:, kh * HEAD_DIM:(kh + 1) * HEAD_DIM])
            l = s + bias_ref[sel, h]
            sink = sink_ref[h]
            m = jnp.maximum(jnp.max(l, axis=1, keepdims=True), sink)
            p = jnp.exp(l - m)
            den = jnp.sum(p, axis=1, keepdims=True) + jnp.exp(sink - m)
            o = _dot(p.astype(BF16), vb[:, kh * HEAD_DIM:(kh + 1) * HEAD_DIM])
            outs.append(o / den)
        blocks.append(jnp.concatenate(outs, axis=1))
    attn = jnp.concatenate(blocks, axis=0)
    m_out = _dot(attn.astype(BF16), wo_ref[...])
    x1_ref[...] = _ln(DEEPNORM_ALPHA * x + m_out, g1_ref[...], b1_ref[...])


def _swa_prompt_layer(x, bsz, seq, kpad, vpad, bias, sinks, w_q, w_o, g1, b1, tq=TM_MIXER):
    nq = seq // tq
    return pl.pallas_call(
        functools.partial(_swa_prompt_kernel, tq=tq),
        grid=(bsz, nq),
        in_specs=[
            pl.BlockSpec(memory_space=pltpu.SMEM),
            pl.BlockSpec((tq, D_MODEL), lambda b, i: (b * nq + i, 0)),
            pl.BlockSpec((None, seq + WINDOW, KV_DIM), lambda b, i: (b, 0, 0)),
            pl.BlockSpec((None, seq + WINDOW, KV_DIM), lambda b, i: (b, 0, 0)),
            _full((2, N_HEADS, WINDOW, 2 * WINDOW)),
            _full((D_MODEL, D_MODEL)),
            _full((D_MODEL, D_MODEL)),
            _full((1, D_MODEL)),
            _full((1, D_MODEL)),
        ],
        out_specs=pl.BlockSpec((tq, D_MODEL), lambda b, i: (b * nq + i, 0)),
        out_shape=jax.ShapeDtypeStruct((bsz * seq, D_MODEL), F32),
        compiler_params=pltpu.CompilerParams(
            dimension_semantics=("arbitrary", "arbitrary"), vmem_limit_bytes=VMEM_LIMIT),
        name="swa_prompt",
    )(sinks, x, kpad, vpad, bias, w_q, w_o, g1, b1)


def _swa_sample_kernel(x_ref, k_ref, v_ref, bias_ref, sink_ref, mask_ref, wq_ref, wo_ref,
                       g1_ref, b1_ref, x1_ref, *, sb, t):
    x = x_ref[...]
    q = _dot(x.astype(BF16), wq_ref[...]) * (HEAD_DIM ** -0.5)
    mask = mask_ref[...]
    bias = bias_ref[...]
    sink = sink_ref[...]
    outs = []
    for s in range(sb):
        qs = q[s * t:(s + 1) * t, :]
        parts = []
        for g in range(Q_PER_KV):
            qg = qs[:, g * KV_DIM:(g + 1) * KV_DIM]
            parts.append(jnp.concatenate([qg] * N_KV_HEADS, axis=0) * mask)
        qexp = jnp.concatenate(parts, axis=0).astype(BF16)
        l = _dot_nt(qexp, k_ref[s]) + bias
        m = jnp.maximum(jnp.max(l, axis=1, keepdims=True), sink)
        p = jnp.exp(l - m)
        den = jnp.sum(p, axis=1, keepdims=True) + jnp.exp(sink - m)
        r = _dot(p.astype(BF16), v_ref[s]) / den
        og = []
        for g in range(Q_PER_KV):
            rg = r[g * N_KV_HEADS * t:(g + 1) * N_KV_HEADS * t, :] * mask
            acc = rg[0:t]
            for kh in range(1, N_KV_HEADS):
                acc = acc + rg[kh * t:(kh + 1) * t]
            og.append(acc)
        outs.append(jnp.concatenate(og, axis=1))
    attn = jnp.concatenate(outs, axis=0)
    m_out = _dot(attn.astype(BF16), wo_ref[...])
    x1_ref[...] = _ln(DEEPNORM_ALPHA * x + m_out, g1_ref[...], b1_ref[...])


def _swa_sample_layer(x, row0, n_seq, t, kbuf, vbuf, bias, sink_col, mask, w_q, w_o, g1, b1,
                      sb=SAMPLE_SEQ_BLOCK):
    rows = sb * t
    hr = N_HEADS * t
    blk0 = row0 // rows
    return pl.pallas_call(
        functools.partial(_swa_sample_kernel, sb=sb, t=t),
        grid=(n_seq // sb,),
        in_specs=[
            pl.BlockSpec((rows, D_MODEL), lambda i: (blk0 + i, 0)),
            pl.BlockSpec((sb, KEY_PAD, KV_DIM), lambda i: (i, 0, 0)),
            pl.BlockSpec((sb, KEY_PAD, KV_DIM), lambda i: (i, 0, 0)),
            _full((hr, KEY_PAD)),
            _full((hr, 1)),
            _full((N_KV_HEADS * t, KV_DIM)),
            _full((D_MODEL, D_MODEL)),
            _full((D_MODEL, D_MODEL)),
            _full((1, D_MODEL)),
            _full((1, D_MODEL)),
        ],
        out_specs=pl.BlockSpec((rows, D_MODEL), lambda i: (i, 0)),
        out_shape=jax.ShapeDtypeStruct((n_seq * t, D_MODEL), F32),
        compiler_params=pltpu.CompilerParams(
            dimension_semantics=("arbitrary",), vmem_limit_bytes=VMEM_LIMIT),
        name="swa_sample",
    )(x, kbuf, vbuf, bias, sink_col, mask, w_q, w_o, g1, b1)


def _token_specs(tm, n_prompt, b_row0):
    p_steps = n_prompt // tm
    b0 = b_row0 // tm
    return [pl.BlockSpec((tm, D_MODEL), lambda i: (jnp.minimum(i, p_steps - 1), 0)),
            pl.BlockSpec((tm, D_MODEL), lambda i: (jnp.maximum(i - p_steps, 0) + b0, 0))]


def _router_kernel(xa_ref, xb_ref, wt_ref, b_ref, tri_ref, idx_ref, gate_ref, rank_ref, cnt_ref,
                   carry, *, tm, p_steps):
    i = pl.program_id(0)

    @pl.when(i == 0)
    def _():
        carry[...] = jnp.zeros_like(carry)

    x = jnp.where(i >= p_steps, xb_ref[...], xa_ref[...])
    x_hi = x.astype(BF16)
    x_lo = (x - x_hi.astype(F32)).astype(BF16)
    w = wt_ref[...]
    w_hi = w.astype(BF16)
    w_lo = (w - w_hi.astype(F32)).astype(BF16)
    logits = (_dot_nt(w_hi, x_hi) + (_dot_nt(w_hi, x_lo) + _dot_nt(w_lo, x_hi))) + b_ref[...]
    eidx = lax.broadcasted_iota(jnp.int32, (N_EXPERTS, tm), 0)
    l = logits
    vals, idxs, sels = [], [], []
    for _ in range(TOP_K):
        m = jnp.max(l, axis=0, keepdims=True)
        idx = jnp.min(jnp.where(l == m, eidx, N_EXPERTS), axis=0, keepdims=True)
        sel = eidx == idx
        vals.append(m)
        idxs.append(idx)
        sels.append(sel)
        l = jnp.where(sel, -jnp.inf, l)
    exps = [jnp.exp(v - vals[0]) for v in vals]
    den = exps[0] + exps[1] + exps[2] + exps[3]
    onehot = jnp.zeros((N_EXPERTS, tm), F32)
    for sel in sels:
        onehot = onehot + sel.astype(F32)
    incl = _dot(onehot.astype(BF16), tri_ref[...])
    excl = incl - onehot + carry[...]
    for k in range(TOP_K):
        idx_ref[k:k + 1, :] = idxs[k]
        gate_ref[k:k + 1, :] = exps[k] / den
        rank = jnp.sum(jnp.where(sels[k], excl, 0.0), axis=0, keepdims=True)
        rank_ref[k:k + 1, :] = rank.astype(jnp.int32)
    carry[...] = carry[...] + jnp.sum(onehot, axis=1, keepdims=True)
    cnt_ref[...] = jnp.broadcast_to(carry[...], cnt_ref.shape)


def _route(src, n, n_prompt, router_wt, router_b, tri, tm=TM_ROUTER):
    xa, xb, b_row0 = src
    spec_kn = pl.BlockSpec((TOP_K, tm), lambda i: (0, i))
    idx, gate, rank, cnt = pl.pallas_call(
        functools.partial(_router_kernel, tm=tm, p_steps=n_prompt // tm),
        grid=(n // tm,),
        in_specs=_token_specs(tm, n_prompt, b_row0) + [
            _full((N_EXPERTS, D_MODEL)),
            _full((N_EXPERTS, 1)),
            _full((tm, tm)),
        ],
        out_specs=[spec_kn, spec_kn, spec_kn, _full((N_EXPERTS, 128))],
        out_shape=[
            jax.ShapeDtypeStruct((TOP_K, n), jnp.int32),
            jax.ShapeDtypeStruct((TOP_K, n), F32),
            jax.ShapeDtypeStruct((TOP_K, n), jnp.int32),
            jax.ShapeDtypeStruct((N_EXPERTS, 128), F32),
        ],
        scratch_shapes=[pltpu.VMEM((N_EXPERTS, 1), F32)],
        compiler_params=pltpu.CompilerParams(
            dimension_semantics=("arbitrary",), vmem_limit_bytes=VMEM_LIMIT),
        name="moe_router",
    )(xa, xb, router_wt, router_b, tri)
    return idx, gate, rank, cnt[:, 0].astype(jnp.int32)


ROW_TILE = 8
LANES = D_MODEL // ROW_TILE


def _to_row_tiles(ref, x):
    rows = x.shape[0]
    for c in range(ROW_TILE):
        ref[pl.ds(c, rows, stride=ROW_TILE), :] = x[:, c * LANES:(c + 1) * LANES]


def _from_row_tiles(ref, rows):
    return jnp.concatenate([ref[pl.ds(c, rows, stride=ROW_TILE), :] for c in range(ROW_TILE)], axis=1)


def _row_tile(ref, p):
    start = p * ROW_TILE
    if not isinstance(p, int):
        start = pl.multiple_of(start, ROW_TILE)
    return ref.at[pl.ds(start, ROW_TILE), :]


def _row_copy_out(x_ref, xs_hbm, sem, t, p):
    return pltpu.make_async_copy(_row_tile(x_ref, t), _row_tile(xs_hbm, p), sem)


def _dispatch_kernel(pos_ref, xa_ref, xb_ref, xs_in_hbm, xs_hbm, x_ref, sem, *, tm, chunk, p_steps):
    del xs_in_hbm
    _to_row_tiles(x_ref, jnp.where(pl.program_id(0) >= p_steps, xb_ref[...], xa_ref[...]))

    def issue(c, carry):
        t0 = c * chunk
        for t in range(chunk):
            for k in range(TOP_K):
                _row_copy_out(x_ref, xs_hbm, sem, t0 + t,
                              pos_ref[0, 0, k * tm + t0 + t]).start(priority=k % 2)
        return carry

    lax.fori_loop(0, tm // chunk, issue, 0)
    for _ in range(TOP_K):
        pltpu.make_async_copy(x_ref, xs_hbm.at[pl.ds(0, ROW_TILE * tm), :], sem).wait()


def _dispatch(src, n, n_prompt, pos3, xs_buf, tm=TM_DISPATCH, chunk=DISPATCH_CHUNK):
    xa, xb, b_row0 = src
    return pl.pallas_call(
        functools.partial(_dispatch_kernel, tm=tm, chunk=chunk, p_steps=n_prompt // tm),
        grid=(n // tm,),
        in_specs=[
            pl.BlockSpec((1, 1, TOP_K * tm), lambda i: (i, 0, 0), memory_space=pltpu.SMEM),
        ] + _token_specs(tm, n_prompt, b_row0) + [
            pl.BlockSpec(memory_space=pl.ANY),
        ],
        out_specs=pl.BlockSpec(memory_space=pl.ANY),
        out_shape=jax.ShapeDtypeStruct(xs_buf.shape, xs_buf.dtype),
        scratch_shapes=[pltpu.VMEM((ROW_TILE * tm, LANES), F32), pltpu.SemaphoreType.DMA],
        input_output_aliases={3: 0},
        compiler_params=pltpu.CompilerParams(
            dimension_semantics=("arbitrary",), vmem_limit_bytes=VMEM_LIMIT),
        name="moe_dispatch",
    )(pos3, xa, xb, xs_buf)


def _expert_kernel(te_ref, nv_ref, xs_ref, wup_ref, bup_ref, wdn_ref, bdn_ref, y_ref):
    del te_ref
    i = pl.program_id(0)

    @pl.when(i < nv_ref[0])
    def _():
        x = _from_row_tiles(xs_ref, y_ref.shape[0] // ROW_TILE)
        h = _dot_mixed(x.astype(BF16), wup_ref[...]) + bup_ref[...]
        glu = jnp.minimum(h[:, :D_EXPERT], SWIGLU_LIMIT)
        lin = jnp.clip(h[:, D_EXPERT:], -SWIGLU_LIMIT, SWIGLU_LIMIT)
        act = glu * jax.nn.sigmoid(SWIGLU_ALPHA * glu) * (lin + 1.0)
        _to_row_tiles(y_ref, _dot_mixed(act.astype(BF16), wdn_ref[...]) + bdn_ref[...])

    @pl.when(i >= nv_ref[0])
    def _():
        y_ref[...] = jnp.zeros_like(y_ref)


def _experts(xs, tile_expert, n_valid, w_up, b_up, w_dn, b_dn, layer, tm=TM_EXPERT):
    rows = xs.shape[0] // ROW_TILE
    row_map = lambda i, te, nv: (jnp.minimum(i, nv[0] - 1), 0)
    wspec = lambda shape: pl.BlockSpec((None, None) + shape, lambda i, te, nv: (layer, te[i], 0, 0))
    grid_spec = pltpu.PrefetchScalarGridSpec(
        num_scalar_prefetch=2,
        grid=(rows // tm,),
        in_specs=[
            pl.BlockSpec((ROW_TILE * tm, LANES), row_map),
            wspec((D_MODEL, 2 * D_EXPERT)),
            wspec((1, 2 * D_EXPERT)),
            wspec((D_EXPERT, D_MODEL)),
            wspec((1, D_MODEL)),
        ],
        out_specs=pl.BlockSpec((ROW_TILE * tm, LANES), lambda i, te, nv: (i, 0)),
    )
    return pl.pallas_call(
        _expert_kernel,
        grid_spec=grid_spec,
        out_shape=jax.ShapeDtypeStruct(xs.shape, F32),
        compiler_params=pltpu.CompilerParams(
            dimension_semantics=("arbitrary",), vmem_limit_bytes=VMEM_LIMIT),
        name="moe_experts",
    )(tile_expert, n_valid, xs, w_up, b_up, w_dn, b_dn)


def _row_copy_in(ys_hbm, ybuf, sem, k, t, p):
    return pltpu.make_async_copy(_row_tile(ys_hbm, p), _row_tile(ybuf.at[k], t), sem)


def _combine_kernel(pos_ref, posn_ref, x1a_ref, x1b_ref, gate_ref, pp_ref, ps_ref, wg_ref, wp_ref,
                    g2_ref, b2_ref, *rest, tm, prompt_steps, emit_kv, fuse_gmlp, split_out):
    kv_ref = v_ref = x3s_ref = None
    if emit_kv:
        kvg_ref, kvb_ref, wkv_ref, ys_hbm, x3_ref, kv_ref, ybuf_a, ybuf_b, sem = rest
    elif fuse_gmlp:
        *gmlp_refs, ys_hbm, x3_ref, v_ref, ybuf_a, ybuf_b, sem = rest
    elif split_out:
        ys_hbm, x3_ref, x3s_ref, ybuf_a, ybuf_b, sem = rest
    else:
        ys_hbm, x3_ref, ybuf_a, ybuf_b, sem = rest
    s = pl.program_id(0)
    is_sample = s >= prompt_steps

    def issue(pref, half, ybuf, sm):
        for t in range(tm):
            for k in range(TOP_K):
                _row_copy_in(ys_hbm, ybuf, sm, k, t,
                             pref[0, 0, (half * TOP_K + k) * tm + t]).start(priority=k % 2)

    def wait_all(ybuf, sm):
        for k in range(TOP_K):
            pltpu.make_async_copy(ys_hbm.at[pl.ds(0, ROW_TILE * tm), :], ybuf.at[k], sm).wait()

    def half_math(half, ybuf):
        rows = slice(half * tm, (half + 1) * tm)
        gate = gate_ref[rows, :]
        f = gate[:, 0:1] * _from_row_tiles(ybuf.at[0], tm)
        for k in range(1, TOP_K):
            f = f + gate[:, k:k + 1] * _from_row_tiles(ybuf.at[k], tm)
        x1 = jnp.where(is_sample, x1b_ref[rows, :], x1a_ref[rows, :])
        x2 = _ln(DEEPNORM_ALPHA * x1 + f, g2_ref[...], b2_ref[...])
        p = jnp.where(is_sample, ps_ref[rows, :], pp_ref[rows, :])
        pg = jax.nn.sigmoid(_dot(x2.astype(BF16), wg_ref[...]))
        pp = _dot(p.astype(BF16), wp_ref[...])
        x3 = x2 + pg * pp
        if fuse_gmlp:
            x3_ref[rows, :], v_ref[rows, :] = _gmlp_math(x3, *gmlp_refs)
        elif split_out:
            @pl.when(is_sample)
            def _():
                x3s_ref[rows, :] = x3

            @pl.when(jnp.logical_not(is_sample))
            def _():
                x3_ref[rows, :] = x3
        else:
            x3_ref[rows, :] = x3
        if emit_kv:
            kv_ref[rows, :] = _dot(_ln(x3, kvg_ref[...], kvb_ref[...]).astype(BF16), wkv_ref[...])

    @pl.when(s == 0)
    def _():
        def first(t, c):
            for k in range(TOP_K):
                _row_copy_in(ys_hbm, ybuf_a, sem.at[0], k, t, pos_ref[0, 0, k * tm + t]).start()
            return c

        lax.fori_loop(0, tm, first, 0)

    wait_all(ybuf_a, sem.at[0])
    issue(pos_ref, 1, ybuf_b, sem.at[1])
    half_math(0, ybuf_a)
    wait_all(ybuf_b, sem.at[1])
    issue(posn_ref, 0, ybuf_a, sem.at[0])
    half_math(1, ybuf_b)

    @pl.when(s == pl.num_programs(0) - 1)
    def _():
        wait_all(ybuf_a, sem.at[0])


def _combine(src, n, pos3, gate_t, ys, p_prompt, p_sample, layer, n_prompt, w_gate, w_proj, g2, b2,
             kv_params=None, gmlp_weights=None, split_out=False, tm=TM_TOKEN):
    xa, xb, b_row0 = src
    emit_kv = kv_params is not None
    fuse_gmlp = gmlp_weights is not None
    assert emit_kv + fuse_gmlp + split_out <= 1
    tm2 = 2 * tm
    steps = n // tm2
    prompt_steps = n_prompt // tm2
    sample_steps = steps - prompt_steps
    row = lambda w: pl.BlockSpec((tm2, w), lambda i: (i, 0))
    smem_pos = lambda f: pl.BlockSpec((1, 1, 2 * TOP_K * tm), lambda i: (f(i), 0, 0),
                                      memory_space=pltpu.SMEM)
    in_specs = [
        smem_pos(lambda i: i),
        smem_pos(lambda i: jnp.minimum(i + 1, steps - 1)),
    ] + _token_specs(tm2, n_prompt, b_row0) + [
        row(TOP_K),
        pl.BlockSpec((tm2, PLE_DIM),
                     lambda i: (layer * prompt_steps + jnp.minimum(i, prompt_steps - 1), 0)),
        pl.BlockSpec((tm2, PLE_DIM),
                     lambda i: (layer * sample_steps + jnp.maximum(i - prompt_steps, 0), 0)),
        _full((D_MODEL, D_MODEL)),
        _full((PLE_DIM, D_MODEL)),
        _full((1, D_MODEL)),
        _full((1, D_MODEL)),
    ]
    args = [pos3, pos3, xa, xb, gate_t, p_prompt, p_sample, w_gate, w_proj, g2, b2]
    if split_out:
        out_specs = [
            pl.BlockSpec((tm2, D_MODEL), lambda i: (jnp.minimum(i, prompt_steps - 1), 0)),
            pl.BlockSpec((tm2, D_MODEL), lambda i: (jnp.maximum(i - prompt_steps, 0), 0)),
        ]
        out_shape = [jax.ShapeDtypeStruct((n_prompt, D_MODEL), F32),
                     jax.ShapeDtypeStruct((n - n_prompt, D_MODEL), F32)]
    else:
        out_specs = [row(D_MODEL)]
        out_shape = [jax.ShapeDtypeStruct((n, D_MODEL), F32)]
    if emit_kv:
        in_specs += [_full((1, D_MODEL)), _full((1, D_MODEL)), _full((D_MODEL, 2 * KV_DIM))]
        args += list(kv_params)
        out_specs.append(row(2 * KV_DIM))
        out_shape.append(jax.ShapeDtypeStruct((n, 2 * KV_DIM), F32))
    if fuse_gmlp:
        in_specs += _gmlp_weight_specs(lambda i: jnp.where(i >= prompt_steps, 1, 0))
        args += list(gmlp_weights)
        out_specs.append(pl.BlockSpec((tm2, D_MODEL),
                                      lambda i: (jnp.maximum(i - (prompt_steps - 1), 0), 0)))
        out_shape.append(jax.ShapeDtypeStruct(((sample_steps + 1) * tm2, D_MODEL), F32))
    in_specs.append(pl.BlockSpec(memory_space=pl.ANY))
    args.append(ys)
    outs = pl.pallas_call(
        functools.partial(_combine_kernel, tm=tm, prompt_steps=prompt_steps, emit_kv=emit_kv,
                          fuse_gmlp=fuse_gmlp, split_out=split_out),
        grid=(steps,),
        in_specs=in_specs,
        out_specs=out_specs,
        out_shape=out_shape,
        scratch_shapes=[pltpu.VMEM((TOP_K, ROW_TILE * tm, LANES), F32),
                        pltpu.VMEM((TOP_K, ROW_TILE * tm, LANES), F32),
                        pltpu.SemaphoreType.DMA((2,))],
        compiler_params=pltpu.CompilerParams(
            dimension_semantics=("arbitrary",), vmem_limit_bytes=VMEM_LIMIT),
        name="moe_combine",
    )(*args)
    if fuse_gmlp:
        return outs[0], outs[1][tm2:]
    return outs if (emit_kv or split_out) else (outs[0], None)


def _tile_pos(pos, tm):
    n = pos.shape[1]
    return pos.reshape(TOP_K, n // tm, tm).transpose(1, 0, 2).reshape(n // tm, 1, TOP_K * tm)


def _pair_pos(pos, tm):
    n = pos.shape[1]
    p = pos.reshape(TOP_K, n // (2 * tm), 2, tm).transpose(1, 2, 0, 3)
    return p.reshape(n // (2 * tm), 1, 2 * TOP_K * tm)


def _moe_layer(src, n, xs_buf, p_prompt, p_sample, n_prompt, layer, router_wt, router_b, tri,
               w_up, b_up, w_dn, b_dn, w_gate, w_proj, g2, b2, kv_params=None, gmlp_weights=None,
               split_out=False):
    idx, gate, rank, counts = _route(src, n, n_prompt, router_wt, router_b, tri)
    padded = ((counts + TM_EXPERT - 1) // TM_EXPERT) * TM_EXPERT
    ends = jnp.cumsum(padded)
    starts = ends - padded
    onehot = idx[:, :, None] == jnp.arange(N_EXPERTS, dtype=jnp.int32)
    pos = jnp.sum(jnp.where(onehot, starts, 0), axis=-1) + rank
    n_tiles = xs_buf.shape[0] // (ROW_TILE * TM_EXPERT)
    n_valid = (ends[-1] // TM_EXPERT).astype(jnp.int32)
    tile_start = jnp.minimum(jnp.arange(n_tiles, dtype=jnp.int32), n_valid - 1) * TM_EXPERT
    tile_expert = jnp.sum(tile_start[:, None] >= ends[None, :], axis=1).astype(jnp.int32)
    xs_buf = _dispatch(src, n, n_prompt, _tile_pos(pos, TM_DISPATCH), xs_buf)
    ys = _experts(xs_buf, tile_expert, n_valid.reshape(1), w_up, b_up, w_dn, b_dn, layer)
    x3, extra = _combine(src, n, _pair_pos(pos, TM_TOKEN), gate.T, ys, p_prompt, p_sample, layer,
                         n_prompt, w_gate, w_proj, g2, b2, kv_params, gmlp_weights, split_out)
    return x3, extra, xs_buf


def _alibi_slopes():
    h = jnp.arange(1, N_HEADS + 1, dtype=F32)
    return 2.0 ** (-8.0 * h / N_HEADS)


def _prompt_bias():
    qi = jnp.arange(WINDOW)[:, None]
    kj = jnp.arange(2 * WINDOW)[None, :]
    dist = qi + WINDOW - kj
    valid = (dist >= 0) & (dist < WINDOW)
    slopes = _alibi_slopes()[:, None, None]
    tables = []
    for first in (True, False):
        ok = valid & (kj >= WINDOW) if first else valid
        tables.append(jnp.where(ok[None], -slopes * dist.astype(F32)[None], NEG_INF))
    return jnp.stack(tables).astype(F32)


def _sample_tables(t, sinks):
    g = jnp.arange(Q_PER_KV)[:, None, None]
    kh = jnp.arange(N_KV_HEADS)[None, :, None]
    tok = jnp.arange(t)[None, None, :]
    head = jnp.broadcast_to(kh * Q_PER_KV + g, (Q_PER_KV, N_KV_HEADS, t)).reshape(-1)
    tok = jnp.broadcast_to(tok, (Q_PER_KV, N_KV_HEADS, t)).reshape(-1)
    kj = jnp.arange(KEY_PAD)[None, :]
    dist = tok[:, None] + WINDOW - kj
    valid = (dist >= 0) & (dist < WINDOW) & (kj < WINDOW + t)
    slopes = _alibi_slopes()[head][:, None]
    bias = jnp.where(valid, -slopes * dist.astype(F32), NEG_INF).astype(F32)
    sink_col = sinks[head][:, None].astype(F32)
    row_kh = jnp.repeat(jnp.arange(N_KV_HEADS), t)[:, None]
    lane_kh = (jnp.arange(KV_DIM) // HEAD_DIM)[None, :]
    mask = (row_kh == lane_kh).astype(F32)
    return bias, sink_col, mask


def _gkd_cols(w):
    r = w.shape[0]
    w4 = w.reshape(r, N_KV_HEADS, Q_PER_KV, HEAD_DIM)
    return w4.transpose(0, 2, 1, 3).reshape(r, N_HEADS * HEAD_DIM)


def _spatial_tables(w_s, b_s, t):
    tril = jnp.tril(jnp.ones((CHUNK, CHUNK), dtype=bool))
    ws_p = jnp.where(tril[None], w_s, 0.0)
    bs_p = jnp.repeat(b_s.T, GROUP_DIM_A, axis=1)
    r = jnp.arange(CHUNK)
    same = (r[:, None] // t) == (r[None, :] // t)
    small = jnp.where(tril[None, :t, :t], w_s[:, :t, :t], 0.0)
    ws_s = jnp.where(same[None], jnp.tile(small, (1, CHUNK // t, CHUNK // t)), 0.0)
    bs_s = jnp.repeat(jnp.tile(b_s[:, :t], (1, CHUNK // t)).T, GROUP_DIM_A, axis=1)
    return jnp.stack([ws_p, ws_s]).astype(BF16), jnp.stack([bs_p, bs_s]).astype(F32)


def kernel(x_prompt, x_sample, cache_k, cache_v, p_prompt, p_sample, ln1_g, ln1_b, ln2_g, ln2_b,
           a_w_in, a_b_in, a_ln_g, a_ln_b, a_w_s, a_b_s, a_w_out, kv_ln_g, kv_ln_b, w_kv,
           b_w_q, b_sinks, b_w_o, router_w, router_b, exp_w_up, exp_b_up, exp_w_dn, exp_b_dn,
           ple_w_proj, ple_w_gate):
    bsz, seq, _ = x_prompt.shape
    n_seq, t, _ = x_sample.shape
    n_p = bsz * seq
    n_s = n_seq * t
    n = n_p + n_s
    depth = ln1_g.shape[0]
    n_a = a_w_in.shape[0]

    xp2 = x_prompt.reshape(n_p, D_MODEL)
    xs2 = x_sample.reshape(n_s, D_MODEL)
    pp2 = p_prompt.reshape(depth * n_p, PLE_DIM)
    ps2 = p_sample.reshape(depth * n_s, PLE_DIM)
    row = lambda v: v.reshape(1, -1).astype(F32)

    xs_rows = ROW_TILE * (n * TOP_K + N_EXPERTS * TM_EXPERT)
    xs_buf = None
    tri = (jnp.arange(TM_ROUTER)[:, None] <= jnp.arange(TM_ROUTER)[None, :]).astype(BF16)
    b_up4 = exp_b_up.reshape(depth, N_EXPERTS, 1, 2 * D_EXPERT)
    b_dn4 = exp_b_dn.reshape(depth, N_EXPERTS, 1, D_MODEL)
    prompt_bias = _prompt_bias()

    def gmlp_weights(i):
        ws2, bs2 = _spatial_tables(a_w_s[i], a_b_s[i], t)
        return (a_w_in[i].astype(BF16), row(a_b_in[i]), row(a_ln_g[i]), row(a_ln_b[i]), ws2, bs2,
                a_w_out[i].astype(BF16), row(ln1_g[i]), row(ln1_b[i]))

    chunk_v = []
    kv_params = (row(kv_ln_g), row(kv_ln_b), w_kv.astype(BF16))
    x = x1_next = x_out_p = x_out_s = None
    kp = vp = k_buf = v_buf = None
    kpad = vpad = kbuf_pad = vbuf_pad = None
    for i in range(depth):
        fuse_next = i + 1 < n_a
        last = i == depth - 1
        if i < n_a:
            if i == 0:
                x1, v_rows, xs_buf = _gmlp_layer(xp2, xs2, 0, n_p, n_s, gmlp_weights(0), xs_rows)
                chunk_v.append(v_rows.reshape(n_seq, t, D_MODEL))
            else:
                x1 = x1_next
            src = (x1, x1, n_p)
        else:
            j = i - n_a
            wq = b_w_q[j].astype(BF16)
            wo = b_w_o[j].astype(BF16)
            x1_p = _swa_prompt_layer(x, bsz, seq, kpad, vpad, prompt_bias,
                                     b_sinks[j].astype(F32), wq, wo, row(ln1_g[i]), row(ln1_b[i]))
            s_bias, s_sink, s_mask = _sample_tables(t, b_sinks[j])
            x1_s = _swa_sample_layer(x, n_p, n_seq, t, kbuf_pad, vbuf_pad, s_bias, s_sink, s_mask,
                                     _gkd_cols(wq), _gkd_cols(wo.T).T, row(ln1_g[i]), row(ln1_b[i]))
            src = (x1_p, x1_s, 0)
        out, extra, xs_buf = _moe_layer(
            src, n, xs_buf, pp2, ps2, n_p, i, router_w[i].T, router_b[i].reshape(N_EXPERTS, 1), tri,
            exp_w_up, b_up4, exp_w_dn, b_dn4, ple_w_gate[i].astype(BF16),
            ple_w_proj[i].astype(BF16), row(ln2_g[i]), row(ln2_b[i]),
            kv_params if i == n_a - 1 else None,
            gmlp_weights(i + 1) if fuse_next else None,
            split_out=last and i != n_a - 1 and not fuse_next)
        if fuse_next:
            x1_next = out
            chunk_v.append(extra.reshape(n_seq, t, D_MODEL))
        elif last and i != n_a - 1:
            x_out_p, x_out_s = out, extra
        else:
            x, kv = out, extra
        if last and i == n_a - 1:
            x_out_p, x_out_s = x[:n_p], x[n_p:]
        if i == n_a - 1:
            k_all, v_all = kv[:, :KV_DIM], kv[:, KV_DIM:]
            kp = k_all[:n_p].reshape(bsz, seq, KV_DIM)
            vp = v_all[:n_p].reshape(bsz, seq, KV_DIM)
            k_buf = jnp.concatenate([cache_k.reshape(n_seq, WINDOW, KV_DIM),
                                     k_all[n_p:].reshape(n_seq, t, KV_DIM)], axis=1)
            v_buf = jnp.concatenate([cache_v.reshape(n_seq, WINDOW, KV_DIM),
                                     v_all[n_p:].reshape(n_seq, t, KV_DIM)], axis=1)
            front = ((0, 0), (WINDOW, 0), (0, 0))
            kpad = jnp.pad(kp, front).astype(BF16)
            vpad = jnp.pad(vp, front).astype(BF16)
            tail = ((0, 0), (0, KEY_PAD - WINDOW - t), (0, 0))
            kbuf_pad = jnp.pad(k_buf, tail).astype(BF16)
            vbuf_pad = jnp.pad(v_buf, tail).astype(BF16)

    heads = (N_KV_HEADS, HEAD_DIM)
    return (x_out_p.reshape(bsz, seq, D_MODEL),
            x_out_s.reshape(n_seq, t, D_MODEL),
            jnp.stack(chunk_v),
            kp[:, -WINDOW:].reshape(bsz, WINDOW, *heads),
            vp[:, -WINDOW:].reshape(bsz, WINDOW, *heads),
            k_buf[:, -WINDOW:].reshape(n_seq, WINDOW, *heads),
            v_buf[:, -WINDOW:].reshape(n_seq, WINDOW, *heads))
```
